```python
import math
import jax
import jax.numpy as jnp
from jax import lax
import numpy as np

D_MODEL = 1024
BATCH = 2
SEQ = 8192
DEPTH = 2
DEC_BATCH = 32
DEC_SEQ = 8
PAST_LEN = 16384
PAGE_SIZE = 128

N_BRANCH = 4
BR_WIDTH = D_MODEL // N_BRANCH
GLA_H = 4
GLA_DV = BR_WIDTH // GLA_H
GLA_DK = GLA_DV // 2
GLA_RANK = 16
GLA_TAU = 16.0
GLA_CHUNK = 64
NSA_H = 4
NSA_DH = BR_WIDTH // NSA_H
CMP_BLOCK = 32
CMP_STRIDE = 16
SEL_BLOCK = 64
SEL_TOPK = 16
WINDOW = 512
FORCE_SCORE = 1.0e4
RW_H = 4
RW_N = BR_WIDTH // RW_H
RW_DECAY_RANK = 64
RW_A_RANK = 64
RW_G_RANK = 128
RW_WIDTHS = (BR_WIDTH, BR_WIDTH, BR_WIDTH, RW_DECAY_RANK, RW_A_RANK, RW_G_RANK)
RW_PROJ = sum(RW_WIDTHS)
RW_SPLITS = tuple(int(s) for s in np.cumsum(RW_WIDTHS)[:-1])
DF_H = 4
DF_D = BR_WIDTH // DF_H // 2
DF_DV = 2 * DF_D
REL_BUCKETS = 32
REL_MAX_DIST = 128
D_FF = 2816
N_EXPERTS = 8
TOP_K = 2
MOE_FF = 1024
Q_BLOCK = 128
EPS = 1e-6
IN_WIDTHS = (
    GLA_H * GLA_DK, GLA_H * GLA_DK, GLA_H * GLA_DV, GLA_RANK, GLA_H * GLA_DV,
    NSA_H * NSA_DH, 6 * NSA_DH, 3 * NSA_H,
    DF_H * 2 * DF_D, DF_H * 2 * DF_D, DF_H * DF_DV,
    RW_PROJ,
    N_BRANCH * D_MODEL,
)
IN_TOTAL = sum(IN_WIDTHS)
IN_SPLITS = tuple(int(s) for s in np.cumsum(IN_WIDTHS)[:-1])

kernel_name = 'hybrid_gla_nsa_rwkv7_diffattn_decode_step'


def rmsnorm(x, g):
    xf = x.astype(jnp.float32)
    y = xf * lax.rsqrt(jnp.mean(xf * xf, axis=-1, keepdims=True) + EPS)
    return (y * g.astype(jnp.float32)).astype(x.dtype)


def masked_softmax(s, mask):
    s = jnp.where(mask, s.astype(jnp.float32), -1e30)
    m = jnp.max(s, axis=-1, keepdims=True)
    p = jnp.where(mask, jnp.exp(s - m), 0.0)
    return p / jnp.maximum(jnp.sum(p, axis=-1, keepdims=True), 1e-30)


def rel_bucket(dist):
    n = jnp.maximum(dist, 0)
    exact = REL_BUCKETS // 2
    nf = jnp.maximum(n, 1).astype(jnp.float32)
    large = exact + (jnp.log(nf / exact) / math.log(REL_MAX_DIST / exact) * (REL_BUCKETS - exact)).astype(jnp.int32)
    return jnp.where(n < exact, n, jnp.minimum(large, REL_BUCKETS - 1))


def t5_bias(dist, table):
    return jnp.moveaxis(table[rel_bucket(dist)].astype(jnp.float32), -1, -2)


def swiglu(h, w1, w3, w2):
    return (jax.nn.silu(h @ w1) * (h @ w3)) @ w2


def moe_ffn(h, wr, w1, w3, w2):
    logits = (h @ wr).astype(jnp.float32)
    top_v, top_i = lax.top_k(logits, TOP_K)
    gate = jax.nn.softmax(top_v, axis=-1)
    comb = jnp.einsum('btk,btke->bte', gate, jax.nn.one_hot(top_i, N_EXPERTS, dtype=jnp.float32))
    y = jnp.zeros_like(h)
    for e in range(N_EXPERTS):
        y = y + comb[..., e:e + 1].astype(h.dtype) * swiglu(h, w1[e], w3[e], w2[e])
    return y


def gla_chunked(q, k, v, g, s0):
    B, T, H, _ = q.shape
    C = GLA_CHUNK if T % GLA_CHUNK == 0 else T
    n = T // C

    def chunks(a):
        return a.astype(jnp.float32).reshape(B, n, C, H, a.shape[-1]).transpose(1, 0, 3, 2, 4)

    causal = jnp.tril(jnp.ones((C, C), dtype=bool))[:, :, None]

    def step(S, inp):
        qi, ki, vi, gi = inp
        b = jnp.cumsum(gi, axis=2)
        o = jnp.einsum('bhtk,bhkv->bhtv', qi * jnp.exp(b), S)
        dec = jnp.exp(jnp.where(causal, b[:, :, :, None, :] - b[:, :, None, :, :], -jnp.inf))
        att = jnp.einsum('bhtk,bhsk,bhtsk->bhts', qi, ki, dec)
        o = o + jnp.einsum('bhts,bhsv->bhtv', att, vi)
        bl = b[:, :, -1:, :]
        S = S * jnp.exp(bl)[:, :, 0, :, None] + jnp.einsum('bhsk,bhsv->bhkv', ki * jnp.exp(bl - b), vi)
        return S, o

    S, o = lax.scan(step, s0.astype(jnp.float32), tuple(chunks(a) for a in (q, k, v, g)))
    return o.transpose(1, 0, 3, 2, 4).reshape(B, T, H, -1), S


def gla_mixer(q, k, v, a_in, r, wa2, ba, gn, s0):
    B, T, _ = q.shape
    q = q.reshape(B, T, GLA_H, GLA_DK) * (GLA_DK ** -0.5)
    k = k.reshape(B, T, GLA_H, GLA_DK)
    v = v.reshape(B, T, GLA_H, GLA_DV)
    g = (jax.nn.log_sigmoid((a_in @ wa2 + ba).astype(jnp.float32)) / GLA_TAU).reshape(B, T, GLA_H, GLA_DK)
    o, S = gla_chunked(q, k, v, g, s0)
    o = rmsnorm(o, gn).reshape(B, T, GLA_H * GLA_DV) * jax.nn.silu(r)
    return o, S


def rwkv7_scan(r, w, k, v, kk, a, s0):
    def step(S, inp):
        rt, wt, kt, vt, kkt, at = inp
        sa = jnp.einsum('bhvk,bhk->bhv', S, kkt)
        S = S * wt[:, :, None, :] - sa[..., None] * (kkt * at)[:, :, None, :] + vt[..., None] * kt[:, :, None, :]
        return S, jnp.einsum('bhvk,bhk->bhv', S, rt)

    xs = tuple(jnp.moveaxis(t, 1, 0) for t in (r, w, k, v, kk, a))
    S, o = lax.scan(step, s0.astype(jnp.float32), xs)
    return jnp.moveaxis(o, 0, 1), S


def rwkv_mixer(p, prev, s0, mu, w0, w2, a0, a2, g2, k_k, k_a, r_k, gn):
    B, T, _ = p.shape
    p_prev = jnp.concatenate([prev[:, None, :].astype(p.dtype), p[:, :-1]], axis=1)
    r, k, v, xw, xa, xg = jnp.split(p + (p_prev - p) * mu, RW_SPLITS, axis=-1)
    decay = jnp.exp(-math.exp(-0.5) * jax.nn.sigmoid((w0 + jnp.tanh(xw) @ w2).astype(jnp.float32)))
    a = jax.nn.sigmoid((a0 + xa @ a2).astype(jnp.float32))
    g = jax.nn.sigmoid(xg) @ g2

    def heads(t):
        return t.astype(jnp.float32).reshape(B, T, RW_H, RW_N)

    kk = heads(k * k_k)
    kk = kk * lax.rsqrt(jnp.maximum(jnp.sum(kk * kk, axis=-1, keepdims=True), 1e-12))
    k = heads(k * (1.0 + (a - 1.0) * k_a))
    r, v, decay, a = heads(r), heads(v), heads(decay), heads(a)
    o, S = rwkv7_scan(r, decay, k, v, kk, a, s0)
    o = rmsnorm(o, gn) + jnp.sum(r * k * r_k, axis=-1, keepdims=True) * v
    return o.reshape(B, T, RW_H * RW_N) * g, S, p[:, -1]


def nsa_compress(k, w):
    B, T, D = k.shape
    sub = k.reshape(B, T // CMP_STRIDE, CMP_STRIDE, D)
    r = CMP_BLOCK // CMP_STRIDE
    n = T // CMP_STRIDE - r + 1
    out = jnp.einsum('bnsd,s->bnd', sub[:, 0:n], w[0:CMP_STRIDE])
    for j in range(1, r):
        out = out + jnp.einsum('bnsd,s->bnd', sub[:, j:j + n], w[j * CMP_STRIDE:(j + 1) * CMP_STRIDE])
    ends = jnp.arange(n) * CMP_STRIDE + CMP_BLOCK - 1
    return out, ends


def cmp_to_sel(n_cmp, n_sel):
    start = jnp.arange(n_cmp)[:, None] * CMP_STRIDE
    sel = jnp.arange(n_sel)[None, :] * SEL_BLOCK
    return ((start < sel + SEL_BLOCK) & (start + CMP_BLOCK > sel)).astype(jnp.float32)


def nsa_cmp_branch(q, qpos, kc, vc, cend, ov, btab):
    dist = qpos[:, None] - cend[None, :]
    s = jnp.einsum('bqhd,bnd->bqhn', q, kc) * NSA_DH ** -0.5 + t5_bias(dist, btab)[None]
    p = masked_softmax(s, (dist >= 0)[None, :, None, :])
    o = jnp.einsum('bqhn,bnd->bqhd', p.astype(vc.dtype), vc)
    score = jnp.einsum('bqhn,ns->bqs', p, ov)
    return o, score


def nsa_select(score, qpos, n_sel):
    cur = (qpos // SEL_BLOCK)[:, None]
    j = jnp.arange(n_sel)[None, :]
    forced = (j == 0) | (j == cur) | (j == cur - 1)
    sc = jnp.where(j <= cur, score + jnp.where(forced, FORCE_SCORE, 0.0), -1.0)
    return lax.top_k(sc, min(SEL_TOPK, n_sel))[1]


def nsa_sel_branch(q, qpos, ks, vs, idx, btab):
    B, Q, K = idx.shape
    kpos = (idx[..., None] * SEL_BLOCK + jnp.arange(SEL_BLOCK)).reshape(B, Q, K * SEL_BLOCK)
    dist = qpos[None, :, None] - kpos
    ks = ks.reshape(B, Q, K * SEL_BLOCK, NSA_DH)
    vs = vs.reshape(B, Q, K * SEL_BLOCK, NSA_DH)
    s = jnp.einsum('bqhd,bqld->bqhl', q, ks) * NSA_DH ** -0.5 + t5_bias(dist, btab)
    p = masked_softmax(s, (dist >= 0)[:, :, None, :])
    return jnp.einsum('bqhl,bqld->bqhd', p.astype(vs.dtype), vs)


def nsa_win_branch(q, qpos, kw, vw, kpos, btab):
    dist = qpos[:, None] - kpos[None, :]
    mask = (dist >= 0) & (dist <= WINDOW) & (kpos >= 0)[None, :]
    s = jnp.einsum('bqhd,bld->bqhl', q, kw) * NSA_DH ** -0.5 + t5_bias(dist, btab)[None]
    p = masked_softmax(s, mask[None, :, None, :])
    return jnp.einsum('bqhl,bld->bqhd', p.astype(vw.dtype), vw)


def nsa_combine(g, o_c, o_s, o_w):
    return g[..., 0:1] * o_c + g[..., 1:2] * o_s + g[..., 2:3] * o_w


def nsa_prompt(q, kv, gates, w_cmp, btab):
    B, T = q.shape[:2]
    kc, cend = nsa_compress(kv[:, :, 0], w_cmp[0])
    vc, _ = nsa_compress(kv[:, :, 1], w_cmp[1])
    n_sel = T // SEL_BLOCK
    ov = cmp_to_sel(kc.shape[1], n_sel)
    ks = kv[:, :, 2].reshape(B, n_sel, SEL_BLOCK, NSA_DH)
    vs = kv[:, :, 3].reshape(B, n_sel, SEL_BLOCK, NSA_DH)
    kw = jnp.pad(kv[:, :, 4], ((0, 0), (WINDOW, 0), (0, 0)))
    vw = jnp.pad(kv[:, :, 5], ((0, 0), (WINDOW, 0), (0, 0)))
    bidx = jnp.arange(B)[:, None, None]
    n_blk = T // Q_BLOCK

    def block(args):
        i, qi, gi = args
        start = i * Q_BLOCK
        qpos = start + jnp.arange(Q_BLOCK)
        o_c, score = nsa_cmp_branch(qi, qpos, kc, vc, cend, ov, btab)
        idx = nsa_select(score, qpos, n_sel)
        o_s = nsa_sel_branch(qi, qpos, ks[bidx, idx], vs[bidx, idx], idx, btab)
        kpos = start - WINDOW + jnp.arange(WINDOW + Q_BLOCK)
        kwi = lax.dynamic_slice_in_dim(kw, start, WINDOW + Q_BLOCK, axis=1)
        vwi = lax.dynamic_slice_in_dim(vw, start, WINDOW + Q_BLOCK, axis=1)
        o_w = nsa_win_branch(qi, qpos, kwi, vwi, kpos, btab)
        return nsa_combine(gi, o_c, o_s, o_w)

    qb = q.reshape(B, n_blk, Q_BLOCK, NSA_H, NSA_DH).swapaxes(0, 1)
    gb = gates.reshape(B, n_blk, Q_BLOCK, NSA_H, 3).swapaxes(0, 1)
    o = lax.map(block, (jnp.arange(n_blk), qb, gb))
    return o.swapaxes(0, 1).reshape(B, T, NSA_H * NSA_DH)


def nsa_sample(q, kv, gates, w_cmp, btab, pool_cmp, pool_sel, page_table, win_buf):
    B, T = q.shape[:2]
    qpos = PAST_LEN + jnp.arange(T)
    total = PAST_LEN + T
    past_cmp = pool_cmp[page_table].reshape(B, PAST_LEN, 2, NSA_DH)
    seq_cmp = jnp.concatenate([past_cmp, kv[:, :, 0:2].astype(past_cmp.dtype)], axis=1)
    seq_cmp = jnp.pad(seq_cmp, ((0, 0), (0, (-total) % CMP_STRIDE), (0, 0), (0, 0)))
    kc, cend = nsa_compress(seq_cmp[:, :, 0], w_cmp[0])
    vc, _ = nsa_compress(seq_cmp[:, :, 1], w_cmp[1])
    n_sel = -(-total // SEL_BLOCK)
    ov = cmp_to_sel(kc.shape[1], n_sel)
    o_c, score = nsa_cmp_branch(q, qpos, kc, vc, cend, ov, btab)
    idx = nsa_select(score, qpos, n_sel)
    nb_past = PAST_LEN // SEL_BLOCK
    nb_new = n_sel - nb_past
    bpp = PAGE_SIZE // SEL_BLOCK
    bidx = jnp.arange(B)[:, None, None]
    jp = jnp.minimum(idx, nb_past - 1)
    phys = page_table[bidx, jp // bpp]
    rows = (jp % bpp)[..., None] * SEL_BLOCK + jnp.arange(SEL_BLOCK)
    g_past = pool_sel[phys[..., None], rows]
    new_sel = jnp.pad(kv[:, :, 2:4], ((0, 0), (0, nb_new * SEL_BLOCK - T), (0, 0), (0, 0)))
    new_sel = new_sel.reshape(B, nb_new, SEL_BLOCK, 2, NSA_DH)
    g_new = new_sel[bidx, jnp.clip(idx - nb_past, 0, nb_new - 1)]
    g = jnp.where((idx < nb_past)[..., None, None, None], g_past, g_new.astype(g_past.dtype))
    o_s = nsa_sel_branch(q, qpos, g[..., 0, :], g[..., 1, :], idx, btab)
    wb = win_buf.shape[1]
    wseq = jnp.concatenate([win_buf, kv[:, :, 4:6].astype(win_buf.dtype)], axis=1)
    kpos = PAST_LEN - wb + jnp.arange(wb + T)
    o_w = nsa_win_branch(q, qpos, wseq[:, :, 0], wseq[:, :, 1], kpos, btab)
    o = nsa_combine(gates, o_c, o_s, o_w)
    return o.reshape(B, T, NSA_H * NSA_DH), wseq[:, T:]


def diff_attend(q, qpos, segs, lam, btab):
    scores, masks = [], []
    for k, v, kpos in segs:
        dist = qpos[:, None] - kpos[None, :]
        s = jnp.einsum('bqhcd,blhcd->bqhcl', q, k) * DF_D ** -0.5
        scores.append(s + t5_bias(dist, btab)[None, :, :, None, :])
        masks.append(dist >= 0)
    p = masked_softmax(jnp.concatenate(scores, axis=-1), jnp.concatenate(masks, axis=-1)[None, :, None, None, :])
    w = p[..., 0, :] - lam * p[..., 1, :]
    o, off = 0.0, 0
    for k, v, kpos in segs:
        n = kpos.shape[0]
        o = o + jnp.einsum('bqhl,blhv->bqhv', w[..., off:off + n].astype(v.dtype), v)
        off += n
    return o


def diff_prompt(q, k, v, lam, btab):
    B, T = q.shape[:2]
    n_blk = T // Q_BLOCK
    segs = ((k, v, jnp.arange(T)),)

    def block(args):
        i, qi = args
        return diff_attend(qi, i * Q_BLOCK + jnp.arange(Q_BLOCK), segs, lam, btab)

    qb = q.reshape(B, n_blk, Q_BLOCK, DF_H, 2, DF_D).swapaxes(0, 1)
    o = lax.map(block, (jnp.arange(n_blk), qb))
    return o.swapaxes(0, 1).reshape(B, T, DF_H, DF_DV)


def diff_sample(q, k, v, lam, btab, pool, page_table):
    B, T = q.shape[:2]
    past = pool[page_table].reshape(B, PAST_LEN, 2, DF_H, DF_DV)
    segs = ((past[:, :, 0].reshape(B, PAST_LEN, DF_H, 2, DF_D), past[:, :, 1], jnp.arange(PAST_LEN)),
            (k, v, PAST_LEN + jnp.arange(T)))
    return diff_attend(q, PAST_LEN + jnp.arange(T), segs, lam, btab)


def token_mix(h, l, W, past, page_table):
    B, T, _ = h.shape
    (g_q, g_k, g_v, g_a, g_r, n_q, n_kv, n_g, d_q, d_k, d_v, rw, m_g) = jnp.split(h @ W['w_in'][l], IN_SPLITS, axis=-1)
    st = {}
    s0 = jnp.zeros((B, GLA_H, GLA_DK, GLA_DV), jnp.float32) if past is None else past['gla']
    o_a, st['gla'] = gla_mixer(g_q, g_k, g_v, g_a, g_r, W['gla_wa2'][l], W['gla_ba'][l], W['gla_norm_g'][l], s0)
    qn = n_q.reshape(B, T, NSA_H, NSA_DH)
    kvn = n_kv.reshape(B, T, 6, NSA_DH)
    gn = jax.nn.sigmoid(n_g.astype(jnp.float32)).reshape(B, T, NSA_H, 3)
    btab_n = W['rel_bias'][:, :NSA_H]
    if past is None:
        o_b = nsa_prompt(qn, kvn, gn, W['nsa_cmp_w'][l], btab_n)
        st['win'] = kvn[:, T - min(WINDOW, T):, 4:6]
    else:
        o_b, st['win'] = nsa_sample(qn, kvn, gn, W['nsa_cmp_w'][l], btab_n, past['cmp'], past['sel'], page_table, past['win'])
    st['cmp'] = kvn[:, :, 0:2]
    st['sel'] = kvn[:, :, 2:4]
    prev = jnp.zeros((B, RW_PROJ), h.dtype) if past is None else past['shift']
    s0 = jnp.zeros((B, RW_H, RW_N, RW_N), jnp.float32) if past is None else past['rwkv']
    o_c, st['rwkv'], st['shift'] = rwkv_mixer(
        rw, prev, s0, W['rw_mu'][l], W['rw_w0'][l], W['rw_w2'][l], W['rw_a0'][l], W['rw_a2'][l], W['rw_g2'][l],
        W['rw_kk'][l], W['rw_ka'][l], W['rw_rk'][l], W['rw_norm_g'][l])
    lam_init = 0.8 - 0.6 * math.exp(-0.3 * l)
    lv = W['df_lam'][l].astype(jnp.float32)
    lam = jnp.exp(jnp.sum(lv[0] * lv[1])) - jnp.exp(jnp.sum(lv[2] * lv[3])) + lam_init
    qd = d_q.reshape(B, T, DF_H, 2, DF_D)
    kd = d_k.reshape(B, T, DF_H, 2, DF_D)
    vd = d_v.reshape(B, T, DF_H, DF_DV)
    btab_d = W['rel_bias'][:, NSA_H:]
    if past is None:
        od = diff_prompt(qd, kd, vd, lam, btab_d)
    else:
        od = diff_sample(qd, kd, vd, lam, btab_d, past['diff'], page_table)
    o_d = (rmsnorm(od, W['df_norm_g'][l]) * (1.0 - lam_init)).reshape(B, T, BR_WIDTH)
    st['diff'] = jnp.stack([d_k.reshape(B, T, DF_H, 2 * DF_D), vd], axis=2)
    br = jnp.stack([t.astype(h.dtype) for t in (o_a, o_b, o_c, o_d)], axis=2)
    up = jnp.einsum('btnc,ncd->btnd', br, W['w_br'][l])
    gate = jax.nn.sigmoid(m_g).reshape(B, T, N_BRANCH, D_MODEL)
    return jnp.einsum('btnd,btnd->btd', gate, up) @ W['w_out'][l], st


def trunk(x, W, cache, page_table):
    new = {}
    for l in range(DEPTH):
        past = None if cache is None else {name: arr[l] for name, arr in cache.items()}
        h = rmsnorm(x, W['norm1_g'][l])
        mix, st = token_mix(h, l, W, past, page_table)
        x = x + mix.astype(x.dtype)
        h = rmsnorm(x, W['norm2_g'][l])
        if l % 2 == 0:
            j = l // 2
            x = x + swiglu(h, W['ffn_w1'][j], W['ffn_w3'][j], W['ffn_w2'][j])
        else:
            j = l // 2
            x = x + moe_ffn(h, W['moe_router'][j], W['moe_w1'][j], W['moe_w3'][j], W['moe_w2'][j])
        for name, arr in st.items():
            new.setdefault(name, []).append(arr)
    y = rmsnorm(x, W['final_norm_g'])
    return y, {name: jnp.stack(arrs) for name, arrs in new.items()}


def setup_inputs(seed: int = 0) -> dict:
    key = jax.random.key(seed)
    keys = jax.random.split(key, 64)
    counter = iter(range(64))
    f32 = jnp.float32

    def nk():
        return keys[next(counter)]

    def nrm(shape, scale):
        return jax.random.normal(nk(), shape, f32) * scale

    def gain(shape):
        return 1.0 + nrm(shape, 0.02)

    n_pages = PAST_LEN // PAGE_SIZE
    n_used = DEC_BATCH * n_pages
    n_pool = n_used + max(1, n_used // 4)
    w_buf = min(WINDOW, PAST_LEN)
    n_dense = (DEPTH + 1) // 2
    n_moe = DEPTH // 2
    return {
        'x_prompt': nrm((BATCH, SEQ, D_MODEL), 1.0),
        'x_sample': nrm((DEC_BATCH, DEC_SEQ, D_MODEL), 1.0),
        'cache_nsa_cmp': nrm((DEPTH, n_pool, PAGE_SIZE, 2, NSA_DH), 1.0),
        'cache_nsa_sel': nrm((DEPTH, n_pool, PAGE_SIZE, 2, NSA_DH), 1.0),
        'cache_diff': nrm((DEPTH, n_pool, PAGE_SIZE, 2, DF_H, DF_DV), 1.0),
        'state_nsa_win': nrm((DEPTH, DEC_BATCH, w_buf, 2, NSA_DH), 1.0),
        'state_gla': nrm((DEPTH, DEC_BATCH, GLA_H, GLA_DK, GLA_DV), 0.5),
        'state_rwkv': nrm((DEPTH, DEC_BATCH, RW_H, RW_N, RW_N), 0.5),
        'state_rwkv_shift': nrm((DEPTH, DEC_BATCH, RW_PROJ), 1.0),
        'page_table': jax.random.permutation(nk(), n_pool)[:n_used].reshape(DEC_BATCH, n_pages).astype(jnp.int32),
        'norm1_g': gain((DEPTH, D_MODEL)),
        'norm2_g': gain((DEPTH, D_MODEL)),
        'final_norm_g': gain((D_MODEL,)),
        'w_in': nrm((DEPTH, D_MODEL, IN_TOTAL), D_MODEL ** -0.5),
        'gla_wa2': nrm((DEPTH, GLA_RANK, GLA_H * GLA_DK), GLA_RANK ** -0.5),
        'gla_ba': nrm((DEPTH, GLA_H * GLA_DK), 0.1),
        'gla_norm_g': gain((DEPTH, GLA_DV)),
        'nsa_cmp_w': (1.0 + nrm((DEPTH, 2, CMP_BLOCK), 0.1)) / CMP_BLOCK,
        'rw_mu': jax.random.uniform(nk(), (DEPTH, RW_PROJ), f32),
        'rw_w0': nrm((DEPTH, BR_WIDTH), 0.5),
        'rw_w2': nrm((DEPTH, RW_DECAY_RANK, BR_WIDTH), 0.1),
        'rw_a0': nrm((DEPTH, BR_WIDTH), 0.1),
        'rw_a2': nrm((DEPTH, RW_A_RANK, BR_WIDTH), 0.1),
        'rw_g2': nrm((DEPTH, RW_G_RANK, BR_WIDTH), RW_G_RANK ** -0.5),
        'rw_kk': 0.85 + nrm((DEPTH, BR_WIDTH), 0.02),
        'rw_ka': gain((DEPTH, BR_WIDTH)),
        'rw_rk': nrm((DEPTH, RW_H, RW_N), 0.1),
        'rw_norm_g': gain((DEPTH, RW_N)),
        'df_lam': nrm((DEPTH, 4, DF_D), 0.1),
        'df_norm_g': gain((DEPTH, DF_DV)),
        'w_br': nrm((DEPTH, N_BRANCH, BR_WIDTH, D_MODEL), BR_WIDTH ** -0.5),
        'w_out': nrm((DEPTH, D_MODEL, D_MODEL), D_MODEL ** -0.5),
        'rel_bias': nrm((REL_BUCKETS, NSA_H + DF_H), 0.5),
        'ffn_w1': nrm((n_dense, D_MODEL, D_FF), D_MODEL ** -0.5),
        'ffn_w3': nrm((n_dense, D_MODEL, D_FF), D_MODEL ** -0.5),
        'ffn_w2': nrm((n_dense, D_FF, D_MODEL), D_FF ** -0.5),
        'moe_router': nrm((n_moe, D_MODEL, N_EXPERTS), D_MODEL ** -0.5),
        'moe_w1': nrm((n_moe, N_EXPERTS, D_MODEL, MOE_FF), D_MODEL ** -0.5),
        'moe_w3': nrm((n_moe, N_EXPERTS, D_MODEL, MOE_FF), D_MODEL ** -0.5),
        'moe_w2': nrm((n_moe, N_EXPERTS, MOE_FF, D_MODEL), MOE_FF ** -0.5),
    }


def reference(x_prompt, x_sample, cache_nsa_cmp, cache_nsa_sel, cache_diff, state_nsa_win, state_gla, state_rwkv,
              state_rwkv_shift, page_table, norm1_g, norm2_g, final_norm_g, w_in, gla_wa2, gla_ba, gla_norm_g,
              nsa_cmp_w, rw_mu, rw_w0, rw_w2, rw_a0, rw_a2, rw_g2, rw_kk, rw_ka, rw_rk, rw_norm_g, df_lam,
              df_norm_g, w_br, w_out, rel_bias, ffn_w1, ffn_w3, ffn_w2, moe_router, moe_w1, moe_w3, moe_w2):
    W = dict(norm1_g=norm1_g, norm2_g=norm2_g, final_norm_g=final_norm_g, w_in=w_in, gla_wa2=gla_wa2,
             gla_ba=gla_ba, gla_norm_g=gla_norm_g, nsa_cmp_w=nsa_cmp_w, rw_mu=rw_mu, rw_w0=rw_w0, rw_w2=rw_w2,
             rw_a0=rw_a0, rw_a2=rw_a2, rw_g2=rw_g2, rw_kk=rw_kk, rw_ka=rw_ka, rw_rk=rw_rk, rw_norm_g=rw_norm_g,
             df_lam=df_lam, df_norm_g=df_norm_g, w_br=w_br, w_out=w_out, rel_bias=rel_bias, ffn_w1=ffn_w1,
             ffn_w3=ffn_w3, ffn_w2=ffn_w2, moe_router=moe_router, moe_w1=moe_w1, moe_w3=moe_w3, moe_w2=moe_w2)
    cache = dict(cmp=cache_nsa_cmp, sel=cache_nsa_sel, diff=cache_diff, win=state_nsa_win, gla=state_gla,
                 rwkv=state_rwkv, shift=state_rwkv_shift)
    y_prompt, sp = trunk(x_prompt, W, None, None)
    y_sample, ss = trunk(x_sample, W, cache, page_table)
    return (y_prompt, y_sample,
            sp['cmp'], sp['sel'], sp['diff'], sp['win'], sp['gla'], sp['rwkv'], sp['shift'],
            ss['cmp'], ss['sel'], ss['diff'], ss['win'], ss['gla'], ss['rwkv'], ss['shift'])
```

```python
import functools
import math

import numpy as np
import jax
import jax.numpy as jnp
from jax import lax
from jax.experimental import pallas as pl
from jax.experimental.pallas import tpu as pltpu

F32 = jnp.float32
BF16 = jnp.bfloat16
I32 = jnp.int32

D_MODEL = 1024
DEPTH = 2
PAST_LEN = 16384
PAGE_SIZE = 128
N_BRANCH = 4
BR_WIDTH = 256
GLA_H, GLA_DK, GLA_DV, GLA_RANK = 4, 32, 64, 16
GLA_TAU = 16.0
GLA_CHUNK = 64
NSA_H, NSA_DH = 4, 64
CMP_BLOCK, CMP_STRIDE, SEL_BLOCK, SEL_TOPK, WINDOW = 32, 16, 64, 16, 512
FORCE_SCORE = 1.0e4
RW_H, RW_N = 4, 64
RW_PROJ = 1024
DF_H, DF_D, DF_DV = 4, 32, 64
REL_BUCKETS, REL_MAX_DIST = 32, 128
N_EXPERTS, TOP_K = 8, 2
Q_BLOCK = 128
EPS = 1e-6
NEG = -1e30

IN_WIDTHS = (128, 128, 256, 16, 256, 256, 384, 12, 256, 256, 256, RW_PROJ, N_BRANCH * D_MODEL)
IN_OFFS = tuple(int(s) for s in np.cumsum((0,) + IN_WIDTHS))
MAIN_COLS = IN_OFFS[12]
MAIN_PAD = 3328

VMEM_LIMIT_BYTES = 56 * 1024 * 1024


def _cparams(*sem):
    return pltpu.CompilerParams(dimension_semantics=sem, vmem_limit_bytes=VMEM_LIMIT_BYTES)


def _dot(a, b):
    return jnp.dot(a, b, preferred_element_type=F32)


def _dot_nt(a, b):
    return lax.dot_general(a, b, (((1,), (1,)), ((), ())), preferred_element_type=F32)


def _dot_tn(a, b):
    return lax.dot_general(a, b, (((0,), (0,)), ((), ())), preferred_element_type=F32)


def _split2(x):
    hi = x.astype(BF16)
    lo = (x - hi.astype(F32)).astype(BF16)
    return hi, lo


def _split3(x):
    hi = x.astype(BF16)
    r = x - hi.astype(F32)
    mid = r.astype(BF16)
    lo = (r - mid.astype(F32)).astype(BF16)
    return hi, mid, lo


def _dot_x2(x, e):
    hi, lo = _split2(x)
    return _dot(hi, e) + _dot(lo, e)


def _dot_e3(e, x):
    hi, mid, lo = _split3(x)
    return _dot(e, hi) + _dot(e, mid) + _dot(e, lo)


def _dot_3x(a, b, dot=_dot):
    ah, al = _split2(a)
    bh, bl = _split2(b)
    return dot(ah, bh) + dot(al, bh) + dot(ah, bl)


def _iota(shape, dim):
    return lax.broadcasted_iota(I32, shape, dim)


def _block_ones(n, seg):
    r = _iota((n, n), 0) // seg
    c = _iota((n, n), 1) // seg
    return (r == c).astype(BF16)


def _rms_rows(x, g):
    ms = jnp.mean(x * x, axis=-1, keepdims=True)
    return x * lax.rsqrt(ms + EPS) * g


def _log_sigmoid(x):
    return -(jnp.maximum(-x, 0.0) + jnp.log1p(jnp.exp(-jnp.abs(x))))


def _pick_tile(n, pref):
    t = min(n, pref)
    while n % t:
        t //= 2
    return t


def _rmsnorm_kernel(x_ref, g_ref, o_ref):
    o_ref[...] = _rms_rows(x_ref[...], g_ref[...]).astype(o_ref.dtype)


def rmsnorm(x, g, out_dtype):
    m, d = x.shape
    tm = _pick_tile(m, 512)
    return pl.pallas_call(
        _rmsnorm_kernel,
        grid=(m // tm,),
        in_specs=[pl.BlockSpec((tm, d), lambda i: (i, 0)), pl.BlockSpec((1, d), lambda i: (0, 0))],
        out_specs=pl.BlockSpec((tm, d), lambda i: (i, 0)),
        out_shape=jax.ShapeDtypeStruct((m, d), out_dtype),
        compiler_params=_cparams("parallel"),
        name="rmsnorm",
    )(x, g.reshape(1, d))


def _mm_kernel(a_ref, b_ref, o_ref):
    o_ref[...] = _dot(a_ref[...], b_ref[...])


def matmul(a, b, tn):
    m, k = a.shape
    n = b.shape[1]
    tm = _pick_tile(m, 512)
    return pl.pallas_call(
        _mm_kernel,
        grid=(m // tm, n // tn),
        in_specs=[pl.BlockSpec((tm, k), lambda i, j: (i, 0)), pl.BlockSpec((k, tn), lambda i, j: (0, j))],
        out_specs=pl.BlockSpec((tm, tn), lambda i, j: (i, j)),
        out_shape=jax.ShapeDtypeStruct((m, n), F32),
        compiler_params=_cparams("parallel", "arbitrary"),
        name="in_proj",
    )(a, b)


def _merge_kernel(h_ref, a_ref, b_ref, c_ref, d_ref, x_ref, wg_ref, wbr_ref, wout_ref, o_ref):
    h = h_ref[...]
    acc = None
    for n, br_ref in enumerate((a_ref, b_ref, c_ref, d_ref)):
        gate = jax.nn.sigmoid(_dot(h, wg_ref[n]))
        up = _dot(br_ref[...].astype(BF16), wbr_ref[n])
        acc = gate * up if acc is None else acc + gate * up
    o_ref[...] = x_ref[...] + _dot(acc.astype(BF16), wout_ref[...])


def merge(h, brs, x, wg, wbr, wout):
    m, d = x.shape
    tm = _pick_tile(m, 256)
    row = lambda i: (i, 0)
    return pl.pallas_call(
        _merge_kernel,
        grid=(m // tm,),
        in_specs=[pl.BlockSpec((tm, d), row)] + [pl.BlockSpec((tm, BR_WIDTH), row)] * 4 + [
            pl.BlockSpec((tm, d), row),
            pl.BlockSpec((N_BRANCH, d, d), lambda i: (0, 0, 0)),
            pl.BlockSpec((N_BRANCH, BR_WIDTH, d), lambda i: (0, 0, 0)),
            pl.BlockSpec((d, d), lambda i: (0, 0)),
        ],
        out_specs=pl.BlockSpec((tm, d), row),
        out_shape=jax.ShapeDtypeStruct((m, d), F32),
        compiler_params=_cparams("parallel"),
        name="merge",
    )(h, *brs, x, wg, wbr, wout)


def _ffn_kernel(x_ref, g_ref, w1_ref, w3_ref, w2_ref, o_ref, h_sc, acc_sc):
    j = pl.program_id(1)

    @pl.when(j == 0)
    def _():
        h_sc[...] = _rms_rows(x_ref[...], g_ref[...]).astype(BF16)
        acc_sc[...] = jnp.zeros_like(acc_sc)

    h = h_sc[...]
    a = _dot(h, w1_ref[...])
    b = _dot(h, w3_ref[...])
    t = (a * jax.nn.sigmoid(a)) * b
    acc_sc[...] += _dot(t.astype(BF16), w2_ref[...])

    @pl.when(j == pl.num_programs(1) - 1)
    def _():
        o_ref[...] = x_ref[...] + acc_sc[...]


def ffn(x, g, w1, w3, w2):
    m, d = x.shape
    ff = w1.shape[1]
    tm = _pick_tile(m, 512)
    tf = 256
    return pl.pallas_call(
        _ffn_kernel,
        grid=(m // tm, ff // tf),
        in_specs=[
            pl.BlockSpec((tm, d), lambda i, j: (i, 0)),
            pl.BlockSpec((1, d), lambda i, j: (0, 0)),
            pl.BlockSpec((d, tf), lambda i, j: (0, j)),
            pl.BlockSpec((d, tf), lambda i, j: (0, j)),
            pl.BlockSpec((tf, d), lambda i, j: (j, 0)),
        ],
        out_specs=pl.BlockSpec((tm, d), lambda i, j: (i, 0)),
        out_shape=jax.ShapeDtypeStruct((m, d), F32),
        scratch_shapes=[pltpu.VMEM((tm, d), BF16), pltpu.VMEM((tm, d), F32)],
        compiler_params=_cparams("parallel", "arbitrary"),
        name="ffn",
    )(x, g.reshape(1, d), w1, w3, w2)


def _moe_kernel(x_ref, g_ref, wr_ref, w1_ref, w3_ref, w2_ref, o_ref, h_sc, acc_sc, comb_sc):
    e = pl.program_id(1)
    j = pl.program_id(2)
    first = (e == 0) & (j == 0)
    last = (e == pl.num_programs(1) - 1) & (j == pl.num_programs(2) - 1)

    @pl.when(first)
    def _():
        hf = _rms_rows(x_ref[...], g_ref[...])
        h_sc[...] = hf.astype(BF16)
        acc_sc[...] = jnp.zeros_like(acc_sc)
        logits = _dot_3x(hf, wr_ref[...])
        lane = _iota(logits.shape, 1)
        lg = jnp.where(lane < N_EXPERTS, logits, -jnp.inf)
        m1 = jnp.max(lg, axis=-1, keepdims=True)
        i1 = jnp.min(jnp.where(lg == m1, lane, 128), axis=-1, keepdims=True)
        lg2 = jnp.where(lane == i1, -jnp.inf, lg)
        m2 = jnp.max(lg2, axis=-1, keepdims=True)
        i2 = jnp.min(jnp.where(lg2 == m2, lane, 128), axis=-1, keepdims=True)
        e2 = jnp.exp(m2 - m1)
        den = 1.0 + e2
        comb_sc[...] = jnp.where(lane == i1, 1.0 / den, 0.0) + jnp.where(lane == i2, e2 / den, 0.0)

    h = h_sc[...]
    a = _dot(h, w1_ref[0])
    b = _dot(h, w3_ref[0])
    t = (a * jax.nn.sigmoid(a)) * b
    comb = comb_sc[...]
    c = jnp.sum(jnp.where(_iota(comb.shape, 1) == e, comb, 0.0), axis=-1, keepdims=True)
    acc_sc[...] += c * _dot(t.astype(BF16), w2_ref[0])

    @pl.when(last)
    def _():
        o_ref[...] = x_ref[...] + acc_sc[...]


def moe(x, g, wr_pad, w1, w3, w2):
    m, d = x.shape
    ne, _, ff = w1.shape
    tm = _pick_tile(m, 512)
    tf = 256
    return pl.pallas_call(
        _moe_kernel,
        grid=(m // tm, ne, ff // tf),
        in_specs=[
            pl.BlockSpec((tm, d), lambda i, e, j: (i, 0)),
            pl.BlockSpec((1, d), lambda i, e, j: (0, 0)),
            pl.BlockSpec((d, 128), lambda i, e, j: (0, 0)),
            pl.BlockSpec((1, d, tf), lambda i, e, j: (e, 0, j)),
            pl.BlockSpec((1, d, tf), lambda i, e, j: (e, 0, j)),
            pl.BlockSpec((1, tf, d), lambda i, e, j: (e, j, 0)),
        ],
        out_specs=pl.BlockSpec((tm, d), lambda i, e, j: (i, 0)),
        out_shape=jax.ShapeDtypeStruct((m, d), F32),
        scratch_shapes=[pltpu.VMEM((tm, d), BF16), pltpu.VMEM((tm, d), F32), pltpu.VMEM((tm, 128), F32)],
        compiler_params=_cparams("parallel", "arbitrary", "arbitrary"),
        name="moe",
    )(x, g.reshape(1, d), wr_pad, w1, w3, w2)


def _gla_kernel(q_ref, k_ref, v_ref, a_ref, r_ref, wa2_ref, wa2t_ref, ba_ref, bacol_ref, gn_ref, s0_ref,
                o_ref, sout_ref, s_sc, b_sc, k_sc, v_sc):
    nb, c, _ = q_ref.shape
    ci = pl.program_id(1)

    @pl.when(ci == 0)
    def _():
        s_sc[...] = s0_ref[...]

    hk = GLA_H * GLA_DK
    hv = GLA_H * GLA_DV
    tri = (_iota((c, c), 0) >= _iota((c, c), 1)).astype(BF16)
    expand = (_iota((hk, hv), 0) // GLA_DK == _iota((hk, hv), 1) // GLA_DV)
    expand_bf = expand.astype(BF16)
    ones_v = _block_ones(hv, GLA_DV)
    wa2 = wa2_ref[...].astype(BF16)
    wa2t = wa2t_ref[...].astype(BF16)
    rowi = _iota((nb, c, hk), 1)

    qs, bs, os1 = [], [], []
    for n in range(nb):
        a_in = a_ref[n].astype(BF16)
        g = _log_sigmoid(_dot(a_in, wa2) + ba_ref[...]) / GLA_TAU
        b = _dot_e3(tri, g)
        gt = _log_sigmoid(_dot_nt(wa2t, a_in) + bacol_ref[...]) / GLA_TAU
        bl_col = jnp.sum(gt, axis=1, keepdims=True)
        q = q_ref[n] * (GLA_DK ** -0.5)
        k = k_ref[n]
        v = v_ref[n]
        s_old = s_sc[n]
        os1.append(_dot_3x(q * jnp.exp(b), s_old))
        kd = k * jnp.exp(b[c - 1:c, :] - b)
        upd = _dot_3x(kd, v, dot=_dot_tn)
        s_sc[n] = s_old * jnp.exp(bl_col) + jnp.where(expand, upd, 0.0)
        qs.append(q)
        b_sc[n] = b
        k_sc[n] = k
        v_sc[n] = v
        bs.append(b)
    q3 = jnp.stack(qs)
    b3 = jnp.stack(bs)

    def body(s, o2):
        b_s = b_sc[:, pl.ds(s, 1), :]
        k_s = k_sc[:, pl.ds(s, 1), :]
        v_s = v_sc[:, pl.ds(s, 1), :]
        dec = jnp.exp(jnp.where(rowi >= s, b3 - b_s, -jnp.inf))
        contrib = (q3 * k_s * dec).reshape(nb * c, hk)
        att = _dot_x2(contrib, expand_bf).reshape(nb, c, hv)
        return o2 + att * v_s

    o2 = lax.fori_loop(0, c, body, jnp.zeros((nb, c, hv), F32))
    for n in range(nb):
        o = os1[n] + o2[n]
        ms = _dot_x2(o * o, ones_v) * (1.0 / GLA_DV)
        o = o * lax.rsqrt(ms + EPS) * gn_ref[...]
        r = r_ref[n]
        o_ref[n] = o * (r * jax.nn.sigmoid(r))

    @pl.when(ci == pl.num_programs(1) - 1)
    def _():
        sout_ref[...] = s_sc[...]


def gla_mixer(q, k, v, a_in, r, wa2, ba, gn, s0_bd, nb):
    bsz, t, hk = q.shape
    hv = v.shape[-1]
    c = GLA_CHUNK if t % GLA_CHUNK == 0 else t
    tok = lambda w: pl.BlockSpec((nb, c, w), lambda b, i: (b, i, 0))
    full = lambda s: pl.BlockSpec(s, lambda b, i: (0,) * len(s))
    st = pl.BlockSpec((nb, hk, hv), lambda b, i: (b, 0, 0))
    return pl.pallas_call(
        _gla_kernel,
        grid=(bsz // nb, t // c),
        in_specs=[tok(hk), tok(hk), tok(hv), tok(GLA_RANK), tok(hv),
                  full((GLA_RANK, hk)), full((hk, GLA_RANK)), full((1, hk)), full((hk, 1)), full((1, hv)), st],
        out_specs=[tok(hv), st],
        out_shape=[jax.ShapeDtypeStruct((bsz, t, hv), F32), jax.ShapeDtypeStruct((bsz, hk, hv), F32)],
        scratch_shapes=[pltpu.VMEM((nb, hk, hv), F32), pltpu.VMEM((nb, c, hk), F32),
                        pltpu.VMEM((nb, c, hk), F32), pltpu.VMEM((nb, c, hv), F32)],
        compiler_params=_cparams("parallel", "arbitrary"),
        name="gla",
    )(q, k, v, a_in, r, wa2, wa2.T, ba.reshape(1, hk), ba.reshape(hk, 1), jnp.tile(gn, GLA_H).reshape(1, hv), s0_bd)


def gla_state_to_bd(s):
    b = s.shape[0]
    eye = jnp.eye(GLA_H, dtype=s.dtype)
    return jnp.einsum('bhkv,hg->bhkgv', s, eye).reshape(b, GLA_H * GLA_DK, GLA_H * GLA_DV)


def gla_state_from_bd(sbd):
    b = sbd.shape[0]
    s5 = sbd.reshape(b, GLA_H, GLA_DK, GLA_H, GLA_DV)
    return jnp.stack([s5[:, h, :, h, :] for h in range(GLA_H)], axis=1)


def _rwkv_prep_kernel(p_ref, first_ref, mu_ref, w0_ref, w2_ref, a0_ref, a2_ref, g2_ref, kk_ref, ka_ref, rk_ref,
                      r_o, w_o, k_o, v_o, kk_o, kka_o, g_o, bonus_o):
    p = p_ref[0]
    prev = jnp.where(_iota(p.shape, 0) == 0, first_ref[0, 0], pltpu.roll(p, 1, 0))
    xm = p + (prev - p) * mu_ref[...]
    n = BR_WIDTH
    r, k, v = xm[:, 0:n], xm[:, n:2 * n], xm[:, 2 * n:3 * n]
    xwa = xm[:, 3 * n:3 * n + 128]
    xg = xm[:, 3 * n + 128:]
    decay = jnp.exp(-math.exp(-0.5) * jax.nn.sigmoid(w0_ref[...] + _dot(jnp.tanh(xwa).astype(BF16), w2_ref[...])))
    a = jax.nn.sigmoid(a0_ref[...] + _dot(xwa.astype(BF16), a2_ref[...]))
    g = _dot(jax.nn.sigmoid(xg).astype(BF16), g2_ref[...])
    ones = _block_ones(n, RW_N)
    kk = k * kk_ref[...]
    kk = kk * lax.rsqrt(jnp.maximum(_dot_x2(kk * kk, ones), 1e-12))
    k2 = k * (1.0 + (a - 1.0) * ka_ref[...])
    r_o[0] = r
    w_o[0] = decay
    k_o[0] = k2
    v_o[0] = v
    kk_o[0] = kk
    kka_o[0] = kk * a
    g_o[0] = g
    bonus_o[0] = _dot_x2(r * k2 * rk_ref[...], ones) * v


def rwkv_prep(p, prev, mu, w0, w2, a0, a2, g2, k_k, k_a, r_k):
    bsz, t, d = p.shape
    tm = _pick_tile(t, 512)
    nt = t // tm
    first = jnp.concatenate([prev[:, None, :], p[:, tm - 1:t - 1:tm, :]], axis=1).reshape(bsz, nt, 1, d)
    n = BR_WIDTH
    w2p = jnp.concatenate([w2, jnp.zeros_like(w2)], axis=0).astype(BF16)
    a2p = jnp.concatenate([jnp.zeros_like(a2), a2], axis=0).astype(BF16)
    row = lambda v: v.reshape(1, -1)
    full = lambda s: pl.BlockSpec(s, lambda b, i: (0,) * len(s))
    tok = pl.BlockSpec((1, tm, n), lambda b, i: (b, i, 0))
    return pl.pallas_call(
        _rwkv_prep_kernel,
        grid=(bsz, nt),
        in_specs=[pl.BlockSpec((1, tm, d), lambda b, i: (b, i, 0)),
                  pl.BlockSpec((1, 1, 1, d), lambda b, i: (b, i, 0, 0)),
                  full((1, d)), full((1, n)), full((128, n)), full((1, n)), full((128, n)), full((128, n)),
                  full((1, n)), full((1, n)), full((1, n))],
        out_specs=[tok] * 8,
        out_shape=[jax.ShapeDtypeStruct((bsz, t, n), F32)] * 8,
        compiler_params=_cparams("parallel", "parallel"),
        name="rwkv_prep",
    )(p, first, row(mu), row(w0), w2p, row(a0), a2p, g2.astype(BF16), row(k_k), row(k_a), row(r_k))


def _rwkv_scan_kernel(r_ref, w_ref, k_ref, v_ref, kk_ref, kka_ref, s0_ref, o_ref, sout_ref, s_sc):
    _, tc, nb, n = r_ref.shape
    ti = pl.program_id(1)

    @pl.when(ti == 0)
    def _():
        s_sc[...] = s0_ref[0]

    ones = _block_ones(n, RW_N)
    diag = (_iota((RW_N, n), 0) == (_iota((RW_N, n), 1) % RW_N)).astype(F32)

    def seg(x3):
        return _dot_x2(x3.reshape(nb * RW_N, n), ones).reshape(nb, RW_N, n)

    def body(t, s):
        row = lambda ref: ref[0, t][:, None, :]
        kk, w, kka, k, r, v = row(kk_ref), row(w_ref), row(kka_ref), row(k_ref), row(r_ref), row(v_ref)
        vcol = seg(v * diag)
        sa = seg(s * kk)
        s = s * w - sa * kka + vcol * k
        ocol = seg(s * r)
        o_ref[0, t] = jnp.sum(ocol * diag, axis=1)
        return s

    s = lax.fori_loop(0, tc, body, s_sc[...])
    s_sc[...] = s

    @pl.when(ti == pl.num_programs(1) - 1)
    def _():
        sout_ref[0] = s


def rwkv_scan(r, w, k, v, kk, kka, s0, nb):
    bsz, t, n = r.shape
    bg = bsz // nb
    tc = _pick_tile(t, 256)
    tm = lambda x: x.reshape(bg, nb, t, n).transpose(0, 2, 1, 3)
    tok = pl.BlockSpec((1, tc, nb, n), lambda b, i: (b, i, 0, 0))
    st = pl.BlockSpec((1, nb, RW_N, n), lambda b, i: (b, 0, 0, 0))
    o, s = pl.pallas_call(
        _rwkv_scan_kernel,
        grid=(bg, t // tc),
        in_specs=[tok] * 6 + [st],
        out_specs=[tok, st],
        out_shape=[jax.ShapeDtypeStruct((bg, t, nb, n), F32), jax.ShapeDtypeStruct((bg, nb, RW_N, n), F32)],
        scratch_shapes=[pltpu.VMEM((nb, RW_N, n), F32)],
        compiler_params=_cparams("parallel", "arbitrary"),
        name="rwkv_scan",
    )(tm(r), tm(w), tm(k), tm(v), tm(kk), tm(kka), s0.reshape(bg, nb, RW_N, n))
    return o.transpose(0, 2, 1, 3).reshape(bsz, t, n), s.reshape(bsz, RW_N, n)


def _rwkv_post_kernel(o_ref, bonus_ref, g_ref, gn_ref, out_ref):
    o = o_ref[...]
    ms = _dot_x2(o * o, _block_ones(BR_WIDTH, RW_N)) * (1.0 / RW_N)
    out_ref[...] = (o * lax.rsqrt(ms + EPS) * gn_ref[...] + bonus_ref[...]) * g_ref[...]


def rwkv_post(o, bonus, g, gn):
    m, n = o.shape
    tm = _pick_tile(m, 1024)
    row = pl.BlockSpec((tm, n), lambda i: (i, 0))
    return pl.pallas_call(
        _rwkv_post_kernel,
        grid=(m // tm,),
        in_specs=[row, row, row, pl.BlockSpec((1, n), lambda i: (0, 0))],
        out_specs=row,
        out_shape=jax.ShapeDtypeStruct((m, n), F32),
        compiler_params=_cparams("parallel"),
        name="rwkv_post",
    )(o, bonus, g, jnp.tile(gn, RW_H).reshape(1, n))


def rwkv_mixer(p, prev, s0, mu, w0, w2, a0, a2, g2, k_k, k_a, r_k, gn, nb):
    bsz, t, _ = p.shape
    r, w, k, v, kk, kka, g, bonus = rwkv_prep(p, prev, mu, w0, w2, a0, a2, g2, k_k, k_a, r_k.reshape(-1))
    s0l = s0.transpose(0, 2, 1, 3).reshape(bsz, RW_N, BR_WIDTH)
    o, s = rwkv_scan(r, w, k, v, kk, kka, s0l, nb)
    out = rwkv_post(o.reshape(bsz * t, BR_WIDTH), bonus.reshape(bsz * t, BR_WIDTH), g.reshape(bsz * t, BR_WIDTH), gn)
    s_new = s.reshape(bsz, RW_N, RW_H, RW_N).transpose(0, 2, 1, 3)
    return out.reshape(bsz, t, BR_WIDTH), s_new, p[:, -1]


def _rel_bucket(dist):
    n = jnp.maximum(dist, 0)
    exact = REL_BUCKETS // 2
    nf = jnp.maximum(n, 1).astype(F32)
    large = exact + (jnp.log(nf / exact) / math.log(REL_MAX_DIST / exact) * (REL_BUCKETS - exact)).astype(I32)
    return jnp.where(n < exact, n, jnp.minimum(large, REL_BUCKETS - 1))


def _bias_from_bucket(bucket, tab_ref, head):
    val = jnp.zeros(bucket.shape, F32)
    for b in range(REL_BUCKETS):
        val = jnp.where(bucket == b, tab_ref[b, head], val)
    return val


def _t5_tiles_kernel(tab_ref, o_ref, *, t, head0, window):
    h = pl.program_id(0)
    d = pl.program_id(1)
    dist = d * t + _iota((t, t), 0) - _iota((t, t), 1)
    val = _bias_from_bucket(_rel_bucket(dist), tab_ref, head0 + h)
    valid = dist >= 0
    if window:
        valid = valid & (dist <= WINDOW)
    o_ref[0, 0] = jnp.where(valid, val, NEG)


def t5_tiles(rel_bias, head0, nh, t, window):
    return pl.pallas_call(
        functools.partial(_t5_tiles_kernel, t=t, head0=head0, window=window),
        grid=(nh, 3),
        in_specs=[pl.BlockSpec(memory_space=pltpu.SMEM)],
        out_specs=pl.BlockSpec((1, 1, t, t), lambda h, d: (h, d, 0, 0)),
        out_shape=jax.ShapeDtypeStruct((nh, 3, t, t), F32),
        compiler_params=_cparams("parallel", "parallel"),
        name="t5_tiles",
    )(rel_bias)


def _pair_tables(nq, back):
    qi, kj, bt, fl = [], [], [], []
    for q in range(nq):
        lo = 0 if back is None else max(q - back, 0)
        for k in range(lo, q + 1):
            qi.append(q)
            kj.append(k)
            bt.append(min(q - k, 2))
            fl.append((1 if k == lo else 0) | (2 if k == q else 0))
    return tuple(jnp.asarray(np.asarray(a, np.int32)) for a in (qi, kj, bt, fl))


def _flash_kernel(qi_t, kj_t, bt_t, fl_t, q_ref, k_ref, v_ref, bias_ref, *rest, nrow, t, use_sel, epi, lam_init):
    rest = list(rest)
    sel_ref = rest.pop(0) if use_sel else None
    m_sc, l_sc, acc_sc = rest[-3:]
    o_ref = rest[-4]
    extras = rest[:-4]
    p = pl.program_id(2)
    flags = fl_t[p]
    dh = q_ref.shape[-1]

    @pl.when((flags & 1) != 0)
    def _():
        m_sc[...] = jnp.full_like(m_sc, NEG)
        l_sc[...] = jnp.zeros_like(l_sc)
        acc_sc[...] = jnp.zeros_like(acc_sc)

    q = q_ref[0, 0].reshape(nrow * t, dh)
    s = _dot_nt(q, k_ref[0, 0])
    s3 = s.reshape(nrow, t, t) + bias_ref[0, bt_t[p]]
    if use_sel:
        ns = sel_ref.shape[-1]
        blk = kj_t[p] * (t // SEL_BLOCK) + _iota((ns, t), 1) // SEL_BLOCK
        expand = (_iota((ns, t), 0) == blk).astype(BF16)
        chosen = _dot(sel_ref[0].astype(BF16), expand) > 0.5
        s3 = jnp.where(chosen[None], s3, NEG)
    s = s3.reshape(nrow * t, t)
    m_prev = m_sc[...]
    m_new = jnp.maximum(m_prev, jnp.max(s, axis=-1, keepdims=True))
    alpha = jnp.exp(m_prev - m_new)
    pr = jnp.exp(s - m_new)
    if use_sel:
        pr = jnp.where(s > 0.5 * NEG, pr, 0.0)
    l_sc[...] = alpha * l_sc[...] + jnp.sum(pr, axis=-1, keepdims=True)
    acc_sc[...] = alpha * acc_sc[...] + _dot(pr.astype(BF16), v_ref[0, 0])
    m_sc[...] = m_new

    @pl.when((flags & 2) != 0)
    def _():
        o = acc_sc[...] / l_sc[...]
        if epi == "plain":
            o_ref[0, 0] = o.reshape(nrow, t, dh)
        elif epi == "diff":
            lam_ref, gn_ref = extras
            lv = lam_ref[...]
            lam = (jnp.exp(jnp.sum(lv[0:1] * lv[1:2], keepdims=True)) - jnp.exp(jnp.sum(lv[2:3] * lv[3:4], keepdims=True))
                   + lam_init)
            od = o[:t] - lam * o[t:]
            o_ref[0, 0, 0] = _rms_rows(od, gn_ref[...]) * (1.0 - lam_init)
        else:
            gate_ref, oc_ref, os_ref = extras
            g = jax.nn.sigmoid(gate_ref[0, 0])
            o_ref[0, 0] = (g[..., 0:1] * oc_ref[0, 0] + g[..., 1:2] * os_ref[0, 0]
                           + g[..., 2:3] * o.reshape(nrow, t, dh))


def flash(q, k, v, bias, tables, *, t, sel=None, epi="plain", extras=(), extra_specs=(), lam_init=None):
    bsz, hg, nrow, tq, dh = q.shape
    npairs = tables[0].shape[0]
    in_specs = [
        pl.BlockSpec((1, 1, nrow, t, dh), lambda b, h, p, qi, kj, bt, fl: (b, h, 0, qi[p], 0)),
        pl.BlockSpec((1, 1, t, dh), lambda b, h, p, qi, kj, bt, fl: (b, h, kj[p], 0)),
        pl.BlockSpec((1, 1, t, dh), lambda b, h, p, qi, kj, bt, fl: (b, h, kj[p], 0)),
        pl.BlockSpec((1,) + bias.shape[1:], lambda b, h, p, qi, kj, bt, fl: (h, 0, 0, 0, 0)),
    ]
    args = [q, k, v, bias]
    if sel is not None:
        in_specs.append(pl.BlockSpec((1, t, sel.shape[-1]), lambda b, h, p, qi, kj, bt, fl: (b, qi[p], 0)))
        args.append(sel)
    in_specs += list(extra_specs)
    args += list(extras)
    n_out = 1 if epi == "diff" else nrow
    return pl.pallas_call(
        functools.partial(_flash_kernel, nrow=nrow, t=t, use_sel=sel is not None, epi=epi, lam_init=lam_init),
        grid_spec=pltpu.PrefetchScalarGridSpec(
            num_scalar_prefetch=4,
            grid=(bsz, hg, npairs),
            in_specs=in_specs,
            out_specs=pl.BlockSpec((1, 1, n_out, t, dh), lambda b, h, p, qi, kj, bt, fl: (b, h, 0, qi[p], 0)),
            scratch_shapes=[pltpu.VMEM((nrow * t, 1), F32), pltpu.VMEM((nrow * t, 1), F32),
                            pltpu.VMEM((nrow * t, dh), F32)],
        ),
        out_shape=jax.ShapeDtypeStruct((bsz, hg, n_out, tq, dh), F32),
        compiler_params=_cparams("parallel", "parallel", "arbitrary"),
        name="flash_" + epi + ("_sel" if sel is not None else ""),
    )(*tables, *args)


def _compress_kernel(pt_ref, *refs, npp):
    del pt_ref
    wt_ref = refs[npp]
    a_ref, b_ref = refs[npp + 1:]
    wt = wt_ref[...]
    for i in range(npp):
        x3 = refs[i][0].reshape(PAGE_SIZE // CMP_STRIDE, CMP_STRIDE, 2 * NSA_DH)
        a_ref[0, i * 8:(i + 1) * 8, :] = jnp.sum(x3 * wt[None, 0:CMP_STRIDE], axis=1)
        b_ref[0, i * 8:(i + 1) * 8, :] = jnp.sum(x3 * wt[None, CMP_STRIDE:], axis=1)


def compress(pool, pt, wt, npp):
    bsz, n_pages = pt.shape
    g = PAGE_SIZE // CMP_STRIDE
    page = lambda i: pl.BlockSpec((1, PAGE_SIZE, 2 * NSA_DH), lambda b, j, pt_ref: (pt_ref[b, j * npp + i], 0, 0))
    out = pl.BlockSpec((1, npp * g, 2 * NSA_DH), lambda b, j, pt_ref: (b, j, 0))
    shape = jax.ShapeDtypeStruct((bsz, n_pages * g, 2 * NSA_DH), F32)
    return pl.pallas_call(
        functools.partial(_compress_kernel, npp=npp),
        grid_spec=pltpu.PrefetchScalarGridSpec(
            num_scalar_prefetch=1,
            grid=(bsz, n_pages // npp),
            in_specs=[page(i) for i in range(npp)] + [pl.BlockSpec((CMP_BLOCK, 2 * NSA_DH), lambda b, j, pt_ref: (0, 0))],
            out_specs=[out, out],
        ),
        out_shape=[shape, shape],
        compiler_params=_cparams("parallel", "arbitrary"),
        name="nsa_compress",
    )(pt, *([pool] * npp), wt)


def _nsa_cmp_kernel(tab_ref, q_ref, a_ref, b_ref, tail_ref, o_ref, sel_ref, *, t, qpos0, n_cmp, n_sel):
    nc = a_ref.shape[1]
    ns = sel_ref.shape[-1]
    qi = pl.program_id(1)
    rown = _iota((nc, 2 * NSA_DH), 0)
    bsh = jnp.where(rown == nc - 1, tail_ref[0], pltpu.roll(b_ref[0], nc - 1, 0))
    kcv = jnp.where(rown < n_cmp, a_ref[0] + bsh, 0.0)
    vc = kcv[:, NSA_DH:].astype(BF16)
    q = q_ref[0, 0].reshape(NSA_H * t, NSA_DH)
    s = _dot_3x(q, kcv[:, :NSA_DH], dot=_dot_nt)
    qpos = qpos0 + qi * t + _iota((t, nc), 0)
    n = _iota((t, nc), 1)
    dist = qpos - (n * CMP_STRIDE + CMP_BLOCK - 1)
    valid = (dist >= 0) & (n < n_cmp)
    bucket = _rel_bucket(dist)
    ps = []
    psum = jnp.zeros((t, nc), F32)
    for h in range(NSA_H):
        sh = jnp.where(valid, s[h * t:(h + 1) * t] + _bias_from_bucket(bucket, tab_ref, h), NEG)
        m = jnp.max(sh, axis=-1, keepdims=True)
        p = jnp.where(valid, jnp.exp(sh - m), 0.0)
        p = p / jnp.maximum(jnp.sum(p, axis=-1, keepdims=True), 1e-30)
        ps.append(p)
        psum = psum + p
    o = _dot(jnp.concatenate(ps, axis=0).astype(BF16), vc)
    o_ref[0, 0] = o.reshape(NSA_H, t, NSA_DH)
    start = _iota((nc, ns), 0) * CMP_STRIDE
    sblk = _iota((nc, ns), 1) * SEL_BLOCK
    ov = ((start < sblk + SEL_BLOCK) & (start + CMP_BLOCK > sblk)).astype(BF16)
    hi, mid, lo = _split3(psum)
    score = _dot(hi, ov) + _dot(mid, ov) + _dot(lo, ov)
    j = _iota((t, ns), 1)
    cur = (qpos0 + qi * t + _iota((t, ns), 0)) // SEL_BLOCK
    forced = (j == 0) | (j == cur) | (j == cur - 1)
    sc = jnp.where(j <= cur, score + jnp.where(forced, FORCE_SCORE, 0.0), -1.0)
    sc = jnp.where(j < n_sel, sc, -jnp.inf)
    chosen = jnp.zeros((t, ns), F32)
    for _ in range(min(SEL_TOPK, n_sel)):
        m = jnp.max(sc, axis=-1, keepdims=True)
        idx = jnp.min(jnp.where(sc == m, j, ns), axis=-1, keepdims=True)
        hit = j == idx
        chosen = jnp.where(hit, 1.0, chosen)
        sc = jnp.where(hit, -jnp.inf, sc)
    sel_ref[0] = chosen


def nsa_cmp(q4, a, b, tail, rel_bias_nsa, *, t, qpos0, n_cmp, n_sel):
    bsz, _, _, tq, dh = q4.shape
    nc = a.shape[1]
    ns = -(-n_sel // 128) * 128
    return pl.pallas_call(
        functools.partial(_nsa_cmp_kernel, t=t, qpos0=qpos0, n_cmp=n_cmp, n_sel=n_sel),
        grid=(bsz, tq // t),
        in_specs=[
            pl.BlockSpec(memory_space=pltpu.SMEM),
            pl.BlockSpec((1, 1, NSA_H, t, dh), lambda b, i: (b, 0, 0, i, 0)),
            pl.BlockSpec((1, nc, 2 * dh), lambda b, i: (b, 0, 0)),
            pl.BlockSpec((1, nc, 2 * dh), lambda b, i: (b, 0, 0)),
            pl.BlockSpec((1, 1, 2 * dh), lambda b, i: (b, 0, 0)),
        ],
        out_specs=[pl.BlockSpec((1, 1, NSA_H, t, dh), lambda b, i: (b, 0, 0, i, 0)),
                   pl.BlockSpec((1, t, ns), lambda b, i: (b, i, 0))],
        out_shape=[jax.ShapeDtypeStruct((bsz, 1, NSA_H, tq, dh), F32), jax.ShapeDtypeStruct((bsz, tq, ns), F32)],
        compiler_params=_cparams("parallel", "parallel"),
        name="nsa_cmp",
    )(rel_bias_nsa, q4, a, b, tail)


def _cmp_weight_tile(w_cmp):
    return jnp.repeat(w_cmp.T, NSA_DH, axis=1)


def nsa_prompt(qn, kvn, n_g, w_cmp, Wc):
    bsz, t_len, _ = qn.shape
    t = _pick_tile(t_len, 256)
    q4f = (qn.reshape(bsz, t_len, NSA_H, NSA_DH) * NSA_DH ** -0.5).transpose(0, 2, 1, 3)[:, None]
    q4 = q4f.astype(BF16)
    n_pages = t_len // PAGE_SIZE
    pool = kvn[:, :, 0:2 * NSA_DH].reshape(bsz * n_pages, PAGE_SIZE, 2 * NSA_DH)
    pt = jnp.arange(bsz * n_pages, dtype=I32).reshape(bsz, n_pages)
    a, b = compress(pool, pt, _cmp_weight_tile(w_cmp), _pick_tile(n_pages, 16))
    o_c, chosen = nsa_cmp(q4f, a, b, jnp.zeros((bsz, 1, 2 * NSA_DH), F32), Wc['rel_nsa'], t=t, qpos0=0,
                          n_cmp=t_len // CMP_STRIDE - 1, n_sel=t_len // SEL_BLOCK)
    kv = lambda i: kvn[:, :, i * NSA_DH:(i + 1) * NSA_DH].astype(BF16)[:, None]
    nq = t_len // t
    o_s = flash(q4, kv(2), kv(3), Wc['tiles_nsa'], _pair_tables(nq, None), t=t, sel=chosen)
    gates = n_g.reshape(bsz, t_len, NSA_H, 3).transpose(0, 2, 1, 3)[:, None]
    tok = lambda w: pl.BlockSpec((1, 1, NSA_H, t, w), lambda b, h, p, qi, kj, bt, fl: (b, h, 0, qi[p], 0))
    o = flash(q4, kv(4), kv(5), Wc['tiles_win'], _pair_tables(nq, WINDOW // t), t=t, epi="win",
              extras=(gates, o_c, o_s), extra_specs=(tok(3), tok(NSA_DH), tok(NSA_DH)))
    return o[:, 0].transpose(0, 2, 1, 3).reshape(bsz, t_len, NSA_H * NSA_DH)


def _lam_init(l):
    return 0.8 - 0.6 * math.exp(-0.3 * l)


def diff_prompt(d_q, d_k, d_v, lam_rows, lam_init, gn, Wc):
    bsz, t_len, _ = d_q.shape
    t = _pick_tile(t_len, 512)
    q = d_q.reshape(bsz, t_len, DF_H, 2 * DF_D).transpose(0, 2, 1, 3) * DF_D ** -0.5
    lane = jnp.arange(2 * DF_D) < DF_D
    q2 = jnp.stack([jnp.where(lane, q, 0.0), jnp.where(lane, 0.0, q)], axis=2).astype(BF16)
    k = d_k.reshape(bsz, t_len, DF_H, 2 * DF_D).transpose(0, 2, 1, 3).astype(BF16)
    v = d_v.reshape(bsz, t_len, DF_H, DF_DV).transpose(0, 2, 1, 3).astype(BF16)
    full = lambda s: pl.BlockSpec(s, lambda b, h, p, qi, kj, bt, fl: (0,) * len(s))
    o = flash(q2, k, v, Wc['tiles_diff'], _pair_tables(t_len // t, None), t=t, epi="diff",
              lam_init=lam_init, extras=(lam_rows, gn.reshape(1, DF_DV)),
              extra_specs=(full((4, DF_D)), full((1, DF_DV))))
    return o[:, :, 0].transpose(0, 2, 1, 3).reshape(bsz, t_len, BR_WIDTH)


NEW_PAD = 16


def _paged_attn_kernel(pt_ref, tab_ref, q_ref, *refs, npp, head_cols, kpos0, qpos0, t_new, window, use_sel):
    del pt_ref
    pages = refs[:npp]
    new_ref = refs[npp]
    sel_ref = refs[npp + 1] if use_sel else None
    o_ref, m_sc, l_sc, acc_sc = refs[-4:]
    j = pl.program_id(1)
    tq = t_new
    ng = len(head_cols)
    nrow = ng * tq
    ks = npp * PAGE_SIZE

    @pl.when(j == 0)
    def _():
        m_sc[...] = jnp.full_like(m_sc, NEG)
        l_sc[...] = jnp.zeros_like(l_sc)
        acc_sc[...] = jnp.zeros_like(acc_sc)

    q = q_ref[0]

    def update(s, kpos, extra_valid, blocks, pv):
        n = s.shape[-1]
        dist = (qpos0 + _iota((tq, n), 0)) - kpos
        valid = dist >= 0
        if window:
            valid = valid & (dist <= WINDOW)
        if extra_valid is not None:
            valid = valid & extra_valid
        if use_sel:
            nsb = sel_ref.shape[-1]
            expand = (_iota((nsb, n), 0) == blocks).astype(BF16)
            valid = valid & (_dot(sel_ref[0].astype(BF16), expand) > 0.5)
        bucket = _rel_bucket(dist)
        cols = sorted(set(head_cols))
        bias = {c: _bias_from_bucket(bucket, tab_ref, c) for c in cols}
        s3 = s.reshape(ng, tq, n) + jnp.stack([bias[c] for c in head_cols])
        s = jnp.where(valid[None], s3, NEG).reshape(nrow, n)
        m_prev = m_sc[...]
        m_new = jnp.maximum(m_prev, jnp.max(s, axis=-1, keepdims=True))
        alpha = jnp.exp(m_prev - m_new)
        pr = jnp.where(s > 0.5 * NEG, jnp.exp(s - m_new), 0.0)
        l_sc[...] = alpha * l_sc[...] + jnp.sum(pr, axis=-1, keepdims=True)
        acc_sc[...] = alpha * acc_sc[...] + pv(pr.astype(BF16))
        m_sc[...] = m_new

    kb = [pages[i][0].astype(BF16) for i in range(npp)]
    s = jnp.concatenate([_dot_nt(q, kb[i]) for i in range(npp)], axis=1)
    kpos = kpos0 + j * ks + _iota((tq, ks), 1)
    blocks = (kpos0 + j * ks + _iota((1, ks), 1)) // SEL_BLOCK

    def pv_pages(pb):
        out = _dot(pb[:, 0:PAGE_SIZE], kb[0])
        for i in range(1, npp):
            out = out + _dot(pb[:, i * PAGE_SIZE:(i + 1) * PAGE_SIZE], kb[i])
        return out

    update(s, kpos, None, blocks, pv_pages)

    @pl.when(j == pl.num_programs(1) - 1)
    def _():
        nb = new_ref[0].astype(BF16)
        col = _iota((tq, NEW_PAD), 1)
        update(_dot_nt(q, nb), qpos0 + col, col < t_new, (qpos0 + _iota((1, NEW_PAD), 1)) // SEL_BLOCK,
               lambda pb: _dot(pb, nb))
        o_ref[0] = acc_sc[...] / l_sc[...]


def paged_attn(q, pool, pt, new, tab, *, npp, head_cols, kpos0, qpos0, window=False, sel=None):
    bsz, nrow, lw = q.shape
    n_pages = pt.shape[1]
    t_new = nrow // len(head_cols)
    page = lambda i: pl.BlockSpec((1, PAGE_SIZE, lw), lambda b, j, pt_ref: (pt_ref[b, j * npp + i], 0, 0))
    in_specs = [pl.BlockSpec(memory_space=pltpu.SMEM), pl.BlockSpec((1, nrow, lw), lambda b, j, pt_ref: (b, 0, 0))]
    in_specs += [page(i) for i in range(npp)]
    in_specs.append(pl.BlockSpec((1, NEW_PAD, lw), lambda b, j, pt_ref: (b, 0, 0)))
    args = [tab, q] + [pool] * npp + [new]
    if sel is not None:
        in_specs.append(pl.BlockSpec((1, t_new, sel.shape[-1]), lambda b, j, pt_ref: (b, 0, 0)))
        args.append(sel)
    return pl.pallas_call(
        functools.partial(_paged_attn_kernel, npp=npp, head_cols=tuple(head_cols), kpos0=kpos0, qpos0=qpos0,
                          t_new=t_new, window=window, use_sel=sel is not None),
        grid_spec=pltpu.PrefetchScalarGridSpec(
            num_scalar_prefetch=1,
            grid=(bsz, n_pages // npp),
            in_specs=in_specs,
            out_specs=pl.BlockSpec((1, nrow, lw), lambda b, j, pt_ref: (b, 0, 0)),
            scratch_shapes=[pltpu.VMEM((nrow, 1), F32), pltpu.VMEM((nrow, 1), F32), pltpu.VMEM((nrow, lw), F32)],
        ),
        out_shape=jax.ShapeDtypeStruct((bsz, nrow, lw), F32),
        compiler_params=_cparams("parallel", "arbitrary"),
        name="paged_attn",
    )(pt, *args)


def _nsa_combine_kernel(g_ref, oc_ref, os_ref, ow_ref, o_ref):
    g = jax.nn.sigmoid(g_ref[...])
    o_ref[...] = g[..., 0:1] * oc_ref[...] + g[..., 1:2] * os_ref[...] + g[..., 2:3] * ow_ref[...]


def nsa_combine(gates, o_c, o_s, o_w):
    n, dh = o_c.shape
    full = lambda w: pl.BlockSpec((n, w), lambda i: (0, 0))
    return pl.pallas_call(
        _nsa_combine_kernel, grid=(1,),
        in_specs=[full(3), full(dh), full(dh), full(dh)], out_specs=full(dh),
        out_shape=jax.ShapeDtypeStruct((n, dh), F32), name="nsa_combine",
    )(gates, o_c, o_s, o_w)


def _pad_rows(x, n):
    return jnp.pad(x, ((0, 0), (0, n - x.shape[1]), (0, 0)))


def nsa_sample(qn, kvn, n_g, w_cmp, Wc, pool_cmp, pool_sel, page_table, win_buf):
    bsz, t_len, _ = qn.shape
    n_pool = pool_cmp.shape[0]
    lw = 2 * NSA_DH
    total = PAST_LEN + t_len
    n_grp = -(-total // CMP_STRIDE)
    n_cmp = n_grp - CMP_BLOCK // CMP_STRIDE + 1
    n_sel = -(-total // SEL_BLOCK)
    qf = (qn.reshape(bsz, t_len, NSA_H, NSA_DH) * NSA_DH ** -0.5).transpose(0, 2, 1, 3)
    qs = qf.astype(BF16)
    wt = _cmp_weight_tile(w_cmp)
    a, b = compress(pool_cmp.reshape(n_pool, PAGE_SIZE, lw), page_table, wt, 16)
    new_page = _pad_rows(kvn[:, :, 0:lw], PAGE_SIZE)
    _, b_new = compress(new_page, jnp.arange(bsz, dtype=I32).reshape(bsz, 1), wt, 1)
    o_c, chosen = nsa_cmp(qf[:, None], a, b, b_new[:, 0:1], Wc['rel_nsa'], t=t_len, qpos0=PAST_LEN,
                          n_cmp=n_cmp, n_sel=n_sel)
    q128 = jnp.pad(qs.reshape(bsz, NSA_H * t_len, NSA_DH), ((0, 0), (0, 0), (0, NSA_DH)))
    heads = tuple(range(NSA_H))
    o_s = paged_attn(q128, pool_sel.reshape(n_pool, PAGE_SIZE, lw), page_table, _pad_rows(kvn[:, :, lw:2 * lw], NEW_PAD),
                     Wc['rel_nsa'], npp=16, head_cols=heads, kpos0=0, qpos0=PAST_LEN, sel=chosen)
    wb = win_buf.shape[1]
    wpages = wb // PAGE_SIZE
    pt_w = jnp.arange(bsz * wpages, dtype=I32).reshape(bsz, wpages)
    o_w = paged_attn(q128, win_buf.reshape(bsz * wpages, PAGE_SIZE, lw), pt_w, _pad_rows(kvn[:, :, 2 * lw:3 * lw], NEW_PAD),
                     Wc['rel_nsa'], npp=wpages, head_cols=heads, kpos0=PAST_LEN - wb, qpos0=PAST_LEN, window=True)
    n = bsz * NSA_H * t_len
    gates = n_g.reshape(bsz, t_len, NSA_H, 3).transpose(0, 2, 1, 3).reshape(n, 3)
    o = nsa_combine(gates, o_c.reshape(n, NSA_DH), o_s[:, :, NSA_DH:].reshape(n, NSA_DH), o_w[:, :, NSA_DH:].reshape(n, NSA_DH))
    o = o.reshape(bsz, NSA_H, t_len, NSA_DH).transpose(0, 2, 1, 3).reshape(bsz, t_len, NSA_H * NSA_DH)
    wseq = jnp.concatenate([win_buf, kvn[:, :, 2 * lw:3 * lw].reshape(bsz, t_len, 2, NSA_DH)], axis=1)
    return o, wseq[:, t_len:]


def _diff_post_kernel(o_ref, lam_ref, gn_ref, out_ref, *, lam_init):
    lv = lam_ref[...]
    lam = (jnp.exp(jnp.sum(lv[0:1] * lv[1:2], keepdims=True)) - jnp.exp(jnp.sum(lv[2:3] * lv[3:4], keepdims=True))
           + lam_init)
    od = o_ref[0] - lam * o_ref[1]
    out_ref[...] = _rms_rows(od, gn_ref[...]) * (1.0 - lam_init)


def diff_post(o2, lam_rows, lam_init, gn):
    _, n, dv = o2.shape
    return pl.pallas_call(
        functools.partial(_diff_post_kernel, lam_init=lam_init), grid=(1,),
        in_specs=[pl.BlockSpec((2, n, dv), lambda i: (0, 0, 0)), pl.BlockSpec((4, DF_D), lambda i: (0, 0)),
                  pl.BlockSpec((1, dv), lambda i: (0, 0))],
        out_specs=pl.BlockSpec((n, dv), lambda i: (0, 0)),
        out_shape=jax.ShapeDtypeStruct((n, dv), F32), name="diff_post",
    )(o2, lam_rows, gn.reshape(1, dv))


def diff_sample(d_q, d_k, d_v, lam_rows, lam_init, gn, Wc, pool, page_table):
    bsz, t_len, _ = d_q.shape
    n_pool = pool.shape[0]
    kw = DF_H * 2 * DF_D
    lw = kw + DF_H * DF_DV
    q = (d_q.reshape(bsz, t_len, DF_H, 2, DF_D) * DF_D ** -0.5).transpose(0, 2, 3, 1, 4)
    eye = jnp.eye(DF_H * 2, dtype=F32).reshape(DF_H, 2, DF_H * 2)
    qk = jnp.einsum('bhctd,hcg->bhctgd', q, eye).reshape(bsz, DF_H * 2 * t_len, kw)
    q512 = jnp.pad(qk, ((0, 0), (0, 0), (0, lw - kw))).astype(BF16)
    new = _pad_rows(jnp.concatenate([d_k, d_v], axis=-1), NEW_PAD)
    head_cols = tuple(NSA_H + h for h in range(DF_H) for _ in range(2))
    o = paged_attn(q512, pool.reshape(n_pool, PAGE_SIZE, lw), page_table, new, Wc['rel_all'], npp=16,
                   head_cols=head_cols, kpos0=0, qpos0=PAST_LEN)
    o = o[:, :, kw:].reshape(bsz, DF_H, 2, t_len, DF_H, DF_DV)
    o = jnp.stack([o[:, h, :, :, h, :] for h in range(DF_H)], axis=1)
    n = bsz * DF_H * t_len
    o2 = o.transpose(2, 0, 1, 3, 4).reshape(2, n, DF_DV)
    od = diff_post(o2, lam_rows, lam_init, gn).reshape(bsz, DF_H, t_len, DF_DV)
    return od.transpose(0, 2, 1, 3).reshape(bsz, t_len, BR_WIDTH)


def _jx_masked_softmax(s, mask):
    s = jnp.where(mask, s.astype(F32), -1e30)
    m = jnp.max(s, axis=-1, keepdims=True)
    p = jnp.where(mask, jnp.exp(s - m), 0.0)
    return p / jnp.maximum(jnp.sum(p, axis=-1, keepdims=True), 1e-30)


def _jx_t5_bias(dist, table):
    return jnp.moveaxis(table[_rel_bucket(dist)].astype(F32), -1, -2)


def _jx_nsa_compress(k, w):
    B, T, D = k.shape
    sub = k.reshape(B, T // CMP_STRIDE, CMP_STRIDE, D)
    r = CMP_BLOCK // CMP_STRIDE
    n = T // CMP_STRIDE - r + 1
    out = jnp.einsum('bnsd,s->bnd', sub[:, 0:n], w[0:CMP_STRIDE])
    for j in range(1, r):
        out = out + jnp.einsum('bnsd,s->bnd', sub[:, j:j + n], w[j * CMP_STRIDE:(j + 1) * CMP_STRIDE])
    ends = jnp.arange(n) * CMP_STRIDE + CMP_BLOCK - 1
    return out, ends


def _jx_nsa_sample(q, kv, gates, w_cmp, btab, pool_cmp, pool_sel, page_table, win_buf):
    B, T = q.shape[:2]
    qpos = PAST_LEN + jnp.arange(T)
    total = PAST_LEN + T
    past_cmp = pool_cmp[page_table].reshape(B, PAST_LEN, 2, NSA_DH)
    seq_cmp = jnp.concatenate([past_cmp, kv[:, :, 0:2].astype(past_cmp.dtype)], axis=1)
    seq_cmp = jnp.pad(seq_cmp, ((0, 0), (0, (-total) % CMP_STRIDE), (0, 0), (0, 0)))
    kc, cend = _jx_nsa_compress(seq_cmp[:, :, 0], w_cmp[0])
    vc, _ = _jx_nsa_compress(seq_cmp[:, :, 1], w_cmp[1])
    n_sel = -(-total // SEL_BLOCK)
    start = jnp.arange(kc.shape[1])[:, None] * CMP_STRIDE
    selb = jnp.arange(n_sel)[None, :] * SEL_BLOCK
    ov = ((start < selb + SEL_BLOCK) & (start + CMP_BLOCK > selb)).astype(F32)
    dist = qpos[:, None] - cend[None, :]
    s = jnp.einsum('bqhd,bnd->bqhn', q, kc) * NSA_DH ** -0.5 + _jx_t5_bias(dist, btab)[None]
    p = _jx_masked_softmax(s, (dist >= 0)[None, :, None, :])
    o_c = jnp.einsum('bqhn,bnd->bqhd', p, vc)
    score = jnp.einsum('bqhn,ns->bqs', p, ov)
    cur = (qpos // SEL_BLOCK)[:, None]
    j = jnp.arange(n_sel)[None, :]
    forced = (j == 0) | (j == cur) | (j == cur - 1)
    sc = jnp.where(j <= cur, score + jnp.where(forced, FORCE_SCORE, 0.0), -1.0)
    idx = lax.top_k(sc, min(SEL_TOPK, n_sel))[1]
    nb_past = PAST_LEN // SEL_BLOCK
    nb_new = n_sel - nb_past
    bpp = PAGE_SIZE // SEL_BLOCK
    bidx = jnp.arange(B)[:, None, None]
    jp = jnp.minimum(idx, nb_past - 1)
    phys = page_table[bidx, jp // bpp]
    rows = (jp % bpp)[..., None] * SEL_BLOCK + jnp.arange(SEL_BLOCK)
    g_past = pool_sel[phys[..., None], rows]
    new_sel = jnp.pad(kv[:, :, 2:4], ((0, 0), (0, nb_new * SEL_BLOCK - T), (0, 0), (0, 0)))
    new_sel = new_sel.reshape(B, nb_new, SEL_BLOCK, 2, NSA_DH)
    g_new = new_sel[bidx, jnp.clip(idx - nb_past, 0, nb_new - 1)]
    g = jnp.where((idx < nb_past)[..., None, None, None], g_past, g_new.astype(g_past.dtype))
    Bq, Q, Kk = idx.shape
    kpos = (idx[..., None] * SEL_BLOCK + jnp.arange(SEL_BLOCK)).reshape(Bq, Q, Kk * SEL_BLOCK)
    dist_s = qpos[None, :, None] - kpos
    ks = g[..., 0, :].reshape(Bq, Q, Kk * SEL_BLOCK, NSA_DH)
    vs = g[..., 1, :].reshape(Bq, Q, Kk * SEL_BLOCK, NSA_DH)
    s2 = jnp.einsum('bqhd,bqld->bqhl', q, ks) * NSA_DH ** -0.5 + _jx_t5_bias(dist_s, btab)
    p2 = _jx_masked_softmax(s2, (dist_s >= 0)[:, :, None, :])
    o_s = jnp.einsum('bqhl,bqld->bqhd', p2, vs)
    wb = win_buf.shape[1]
    wseq = jnp.concatenate([win_buf, kv[:, :, 4:6].astype(win_buf.dtype)], axis=1)
    kposw = PAST_LEN - wb + jnp.arange(wb + T)
    dist_w = qpos[:, None] - kposw[None, :]
    mask = (dist_w >= 0) & (dist_w <= WINDOW) & (kposw >= 0)[None, :]
    s3 = jnp.einsum('bqhd,bld->bqhl', q, wseq[:, :, 0]) * NSA_DH ** -0.5 + _jx_t5_bias(dist_w, btab)[None]
    p3 = _jx_masked_softmax(s3, mask[None, :, None, :])
    o_w = jnp.einsum('bqhl,bld->bqhd', p3, wseq[:, :, 1])
    o = gates[..., 0:1] * o_c + gates[..., 1:2] * o_s + gates[..., 2:3] * o_w
    return o.reshape(B, T, NSA_H * NSA_DH), wseq[:, T:]


def _jx_diff_sample(q, k, v, lam, btab, pool, page_table):
    B, T = q.shape[:2]
    past = pool[page_table].reshape(B, PAST_LEN, 2, DF_H, DF_DV)
    segs = ((past[:, :, 0].reshape(B, PAST_LEN, DF_H, 2, DF_D), past[:, :, 1], jnp.arange(PAST_LEN)),
            (k, v, PAST_LEN + jnp.arange(T)))
    qpos = PAST_LEN + jnp.arange(T)
    scores, masks = [], []
    for kk, vv, kpos in segs:
        dist = qpos[:, None] - kpos[None, :]
        s = jnp.einsum('bqhcd,blhcd->bqhcl', q, kk) * DF_D ** -0.5
        scores.append(s + _jx_t5_bias(dist, btab)[None, :, :, None, :])
        masks.append(dist >= 0)
    p = _jx_masked_softmax(jnp.concatenate(scores, axis=-1), jnp.concatenate(masks, axis=-1)[None, :, None, None, :])
    w = p[..., 0, :] - lam * p[..., 1, :]
    o, off = 0.0, 0
    for kk, vv, kpos in segs:
        n = kpos.shape[0]
        o = o + jnp.einsum('bqhl,blhv->bqhv', w[..., off:off + n], vv)
        off += n
    return o


def prepare_weights(W):
    Wc = {}
    w_in = W['w_in']
    Wc['w_main'] = jnp.pad(w_in[:, :, :MAIN_COLS], ((0, 0), (0, 0), (0, MAIN_PAD - MAIN_COLS))).astype(BF16)
    Wc['w_gate'] = w_in[:, :, MAIN_COLS:].reshape(DEPTH, D_MODEL, N_BRANCH, D_MODEL).transpose(0, 2, 1, 3).astype(BF16)
    for name in ('w_br', 'w_out', 'ffn_w1', 'ffn_w3', 'ffn_w2', 'moe_w1', 'moe_w3', 'moe_w2'):
        Wc[name] = W[name].astype(BF16)
    Wc['moe_router'] = jnp.pad(W['moe_router'], ((0, 0), (0, 0), (0, 128 - N_EXPERTS)))
    rel = W['rel_bias']
    Wc['rel_nsa'] = rel[:, :NSA_H]
    Wc['rel_all'] = rel
    Wc['tiles_nsa'] = t5_tiles(rel, 0, NSA_H, 256, False).transpose(1, 0, 2, 3)[None]
    Wc['tiles_win'] = t5_tiles(rel, 0, NSA_H, 256, True).transpose(1, 0, 2, 3)[None]
    Wc['tiles_diff'] = t5_tiles(rel, NSA_H, DF_H, 512, False)[:, :, None]
    return Wc


def token_mix(h, l, W, Wc, past, page_table):
    B, T, _ = h.shape
    m = B * T
    nb_state = 2 if past is None else 8
    proj = matmul(h.reshape(m, D_MODEL), Wc['w_main'][l], MAIN_PAD // 2).reshape(B, T, MAIN_PAD)
    sl = lambda i: proj[:, :, IN_OFFS[i]:IN_OFFS[i + 1]]
    g_q, g_k, g_v, g_a, g_r, n_q, n_kv, n_g, d_q, d_k, d_v, rw = [sl(i) for i in range(12)]
    st = {}
    s0 = jnp.zeros((B, GLA_H * GLA_DK, GLA_H * GLA_DV), F32) if past is None else gla_state_to_bd(past['gla'])
    o_a, s_bd = gla_mixer(g_q, g_k, g_v, g_a, g_r, W['gla_wa2'][l], W['gla_ba'][l], W['gla_norm_g'][l], s0, nb_state)
    st['gla'] = gla_state_from_bd(s_bd)
    kvn = n_kv.reshape(B, T, 6, NSA_DH)
    if past is None:
        o_b = nsa_prompt(n_q, n_kv, n_g, W['nsa_cmp_w'][l], Wc)
        st['win'] = kvn[:, T - min(WINDOW, T):, 4:6]
    else:
        o_b, st['win'] = nsa_sample(n_q, n_kv, n_g, W['nsa_cmp_w'][l], Wc, past['cmp'], past['sel'], page_table,
                                    past['win'])
    st['cmp'] = kvn[:, :, 0:2]
    st['sel'] = kvn[:, :, 2:4]
    prev = jnp.zeros((B, RW_PROJ), F32) if past is None else past['shift']
    s0 = jnp.zeros((B, RW_H, RW_N, RW_N), F32) if past is None else past['rwkv']
    o_c, st['rwkv'], st['shift'] = rwkv_mixer(
        rw, prev, s0, W['rw_mu'][l], W['rw_w0'][l], W['rw_w2'][l], W['rw_a0'][l], W['rw_a2'][l], W['rw_g2'][l],
        W['rw_kk'][l], W['rw_ka'][l], W['rw_rk'][l], W['rw_norm_g'][l], nb_state)
    lam_init = _lam_init(l)
    if past is None:
        o_d = diff_prompt(d_q, d_k, d_v, W['df_lam'][l], lam_init, W['df_norm_g'][l], Wc)
    else:
        o_d = diff_sample(d_q, d_k, d_v, W['df_lam'][l], lam_init, W['df_norm_g'][l], Wc, past['diff'], page_table)
    st['diff'] = jnp.stack([d_k.reshape(B, T, DF_H, 2 * DF_D), d_v.reshape(B, T, DF_H, DF_DV)], axis=2)
    return [t.reshape(m, BR_WIDTH) for t in (o_a, o_b, o_c, o_d)], st


def trunk(x, W, Wc, cache, page_table):
    B, T, _ = x.shape
    x2 = x.reshape(B * T, D_MODEL)
    new = {}
    for l in range(DEPTH):
        past = None if cache is None else {name: arr[l] for name, arr in cache.items()}
        h = rmsnorm(x2, W['norm1_g'][l], BF16)
        brs, st = token_mix(h.reshape(B, T, D_MODEL), l, W, Wc, past, page_table)
        x2 = merge(h, brs, x2, Wc['w_gate'][l], Wc['w_br'][l], Wc['w_out'][l])
        j = l // 2
        if l % 2 == 0:
            x2 = ffn(x2, W['norm2_g'][l], Wc['ffn_w1'][j], Wc['ffn_w3'][j], Wc['ffn_w2'][j])
        else:
            x2 = moe(x2, W['norm2_g'][l], Wc['moe_router'][j], Wc['moe_w1'][j], Wc['moe_w3'][j], Wc['moe_w2'][j])
        for name, arr in st.items():
            new.setdefault(name, []).append(arr)
    y = rmsnorm(x2, W['final_norm_g'], F32).reshape(B, T, D_MODEL)
    return y, {name: jnp.stack(arrs) for name, arrs in new.items()}


def kernel(x_prompt, x_sample, cache_nsa_cmp, cache_nsa_sel, cache_diff, state_nsa_win, state_gla, state_rwkv, state_rwkv_shift, page_table, norm1_g, norm2_g, final_norm_g, w_in, gla_wa2, gla_ba, gla_norm_g, nsa_cmp_w, rw_mu, rw_w0, rw_w2, rw_a0, rw_a2, rw_g2, rw_kk, rw_ka, rw_rk, rw_norm_g, df_lam, df_norm_g, w_br, w_out, rel_bias, ffn_w1, ffn_w3, ffn_w2, moe_router, moe_w1, moe_w3, moe_w2):
    W = dict(norm1_g=norm1_g, norm2_g=norm2_g, final_norm_g=final_norm_g, w_in=w_in, gla_wa2=gla_wa2,
             gla_ba=gla_ba, gla_norm_g=gla_norm_g, nsa_cmp_w=nsa_cmp_w, rw_mu=rw_mu, rw_w0=rw_w0, rw_w2=rw_w2,
             rw_a0=rw_a0, rw_a2=rw_a2, rw_g2=rw_g2, rw_kk=rw_kk, rw_ka=rw_ka, rw_rk=rw_rk, rw_norm_g=rw_norm_g,
             df_lam=df_lam, df_norm_g=df_norm_g, w_br=w_br, w_out=w_out, rel_bias=rel_bias, ffn_w1=ffn_w1,
             ffn_w3=ffn_w3, ffn_w2=ffn_w2, moe_router=moe_router, moe_w1=moe_w1, moe_w3=moe_w3, moe_w2=moe_w2)
    cache = dict(cmp=cache_nsa_cmp, sel=cache_nsa_sel, diff=cache_diff, win=state_nsa_win, gla=state_gla,
                 rwkv=state_rwkv, shift=state_rwkv_shift)
    Wc = prepare_weights(W)
    y_prompt, sp = trunk(x_prompt, W, Wc, None, None)
    y_sample, ss = trunk(x_sample, W, Wc, cache, page_table)
    return (y_prompt, y_sample,
            sp['cmp'], sp['sel'], sp['diff'], sp['win'], sp['gla'], sp['rwkv'], sp['shift'],
            ss['cmp'], ss['sel'], ss['diff'], ss['win'], ss['gla'], ss['rwkv'], ss['shift'])
```

```python
import functools
import math

import numpy as np
import jax
import jax.numpy as jnp
from jax import lax
from jax.experimental import pallas as pl
from jax.experimental.pallas import tpu as pltpu

F32 = jnp.float32
BF16 = jnp.bfloat16
I32 = jnp.int32

D_MODEL = 1024
DEPTH = 2
PAST_LEN = 16384
PAGE_SIZE = 128
N_BRANCH = 4
BR_WIDTH = 256
GLA_H, GLA_DK, GLA_DV, GLA_RANK = 4, 32, 64, 16
GLA_TAU = 16.0
GLA_CHUNK = 64
NSA_H, NSA_DH = 4, 64
CMP_BLOCK, CMP_STRIDE, SEL_BLOCK, SEL_TOPK, WINDOW = 32, 16, 64, 16, 512
FORCE_SCORE = 1.0e4
RW_H, RW_N = 4, 64
RW_PROJ = 1024
DF_H, DF_D, DF_DV = 4, 32, 64
REL_BUCKETS, REL_MAX_DIST = 32, 128
N_EXPERTS, TOP_K = 8, 2
Q_BLOCK = 128
EPS = 1e-6
NEG = -1e30

IN_WIDTHS = (128, 128, 256, 16, 256, 256, 384, 12, 256, 256, 256, RW_PROJ, N_BRANCH * D_MODEL)
IN_OFFS = tuple(int(s) for s in np.cumsum((0,) + IN_WIDTHS))
MAIN_COLS = IN_OFFS[12]
MAIN_PAD = 3328

VMEM_LIMIT_BYTES = 56 * 1024 * 1024
SCORE_CHUNK_ELEMS = 32 * 1024
PV_GROUP = 4


def _cparams(*sem):
    return pltpu.CompilerParams(dimension_semantics=sem, vmem_limit_bytes=VMEM_LIMIT_BYTES)


def _dot(a, b):
    return jnp.dot(a, b, preferred_element_type=F32)


def _dot_nt(a, b):
    return lax.dot_general(a, b, (((1,), (1,)), ((), ())), preferred_element_type=F32)


def _dot_tn(a, b):
    return lax.dot_general(a, b, (((0,), (0,)), ((), ())), preferred_element_type=F32)


def _split2(x):
    hi = x.astype(BF16)
    lo = (x - hi.astype(F32)).astype(BF16)
    return hi, lo


def _split3(x):
    hi = x.astype(BF16)
    r = x - hi.astype(F32)
    mid = r.astype(BF16)
    lo = (r - mid.astype(F32)).astype(BF16)
    return hi, mid, lo


def _dot_x2(x, e):
    hi, lo = _split2(x)
    return _dot(hi, e) + _dot(lo, e)


def _dot_e3(e, x):
    hi, mid, lo = _split3(x)
    return _dot(e, hi) + _dot(e, mid) + _dot(e, lo)


def _dot_3x(a, b, dot=_dot):
    ah, al = _split2(a)
    bh, bl = _split2(b)
    return dot(ah, bh) + dot(al, bh) + dot(ah, bl)


def _iota(shape, dim):
    return lax.broadcasted_iota(I32, shape, dim)


def _block_ones(n, seg):
    r = _iota((n, n), 0) // seg
    c = _iota((n, n), 1) // seg
    return (r == c).astype(BF16)


def _rms_rows(x, g):
    ms = jnp.mean(x * x, axis=-1, keepdims=True)
    return x * lax.rsqrt(ms + EPS) * g


def _log_sigmoid(x):
    return -(jnp.maximum(-x, 0.0) + jnp.log1p(jnp.exp(-jnp.abs(x))))


def _pick_tile(n, pref):
    t = min(n, pref)
    while n % t:
        t //= 2
    return t


def _rmsnorm_kernel(x_ref, g_ref, o_ref):
    o_ref[...] = _rms_rows(x_ref[...], g_ref[...]).astype(o_ref.dtype)


def rmsnorm(x, g, out_dtype):
    m, d = x.shape
    tm = _pick_tile(m, 512)
    return pl.pallas_call(
        _rmsnorm_kernel,
        grid=(m // tm,),
        in_specs=[pl.BlockSpec((tm, d), lambda i: (i, 0)), pl.BlockSpec((1, d), lambda i: (0, 0))],
        out_specs=pl.BlockSpec((tm, d), lambda i: (i, 0)),
        out_shape=jax.ShapeDtypeStruct((m, d), out_dtype),
        compiler_params=_cparams("parallel"),
        name="rmsnorm",
    )(x, g.reshape(1, d))


def _mm_kernel(a_ref, b_ref, o_ref):
    o_ref[...] = _dot(a_ref[...], b_ref[...])


def matmul(a, b, tn):
    m, k = a.shape
    n = b.shape[1]
    tm = _pick_tile(m, 512)
    return pl.pallas_call(
        _mm_kernel,
        grid=(m // tm, n // tn),
        in_specs=[pl.BlockSpec((tm, k), lambda i, j: (i, 0)), pl.BlockSpec((k, tn), lambda i, j: (0, j))],
        out_specs=pl.BlockSpec((tm, tn), lambda i, j: (i, j)),
        out_shape=jax.ShapeDtypeStruct((m, n), F32),
        compiler_params=_cparams("parallel", "arbitrary"),
        name="in_proj",
    )(a, b)


def _merge_kernel(h_ref, a_ref, b_ref, c_ref, d_ref, x_ref, wg_ref, wbr_ref, wout_ref, o_ref):
    h = h_ref[...]
    acc = None
    for n, br_ref in enumerate((a_ref, b_ref, c_ref, d_ref)):
        gate = jax.nn.sigmoid(_dot(h, wg_ref[n]))
        up = _dot(br_ref[...].astype(BF16), wbr_ref[n])
        acc = gate * up if acc is None else acc + gate * up
    o_ref[...] = x_ref[...] + _dot(acc.astype(BF16), wout_ref[...])


def merge(h, brs, x, wg, wbr, wout):
    m, d = x.shape
    tm = _pick_tile(m, 256)
    row = lambda i: (i, 0)
    return pl.pallas_call(
        _merge_kernel,
        grid=(m // tm,),
        in_specs=[pl.BlockSpec((tm, d), row)] + [pl.BlockSpec((tm, BR_WIDTH), row)] * 4 + [
            pl.BlockSpec((tm, d), row),
            pl.BlockSpec((N_BRANCH, d, d), lambda i: (0, 0, 0)),
            pl.BlockSpec((N_BRANCH, BR_WIDTH, d), lambda i: (0, 0, 0)),
            pl.BlockSpec((d, d), lambda i: (0, 0)),
        ],
        out_specs=pl.BlockSpec((tm, d), row),
        out_shape=jax.ShapeDtypeStruct((m, d), F32),
        compiler_params=_cparams("parallel"),
        name="merge",
    )(h, *brs, x, wg, wbr, wout)


def _ffn_kernel(x_ref, g_ref, w1_ref, w3_ref, w2_ref, o_ref, h_sc, acc_sc):
    j = pl.program_id(1)

    @pl.when(j == 0)
    def _():
        h_sc[...] = _rms_rows(x_ref[...], g_ref[...]).astype(BF16)
        acc_sc[...] = jnp.zeros_like(acc_sc)

    h = h_sc[...]
    a = _dot(h, w1_ref[...])
    b = _dot(h, w3_ref[...])
    t = (a * jax.nn.sigmoid(a)) * b
    acc_sc[...] += _dot(t.astype(BF16), w2_ref[...])

    @pl.when(j == pl.num_programs(1) - 1)
    def _():
        o_ref[...] = x_ref[...] + acc_sc[...]


def ffn(x, g, w1, w3, w2):
    m, d = x.shape
    ff = w1.shape[1]
    tm = _pick_tile(m, 512)
    tf = 256
    return pl.pallas_call(
        _ffn_kernel,
        grid=(m // tm, ff // tf),
        in_specs=[
            pl.BlockSpec((tm, d), lambda i, j: (i, 0)),
            pl.BlockSpec((1, d), lambda i, j: (0, 0)),
            pl.BlockSpec((d, tf), lambda i, j: (0, j)),
            pl.BlockSpec((d, tf), lambda i, j: (0, j)),
            pl.BlockSpec((tf, d), lambda i, j: (j, 0)),
        ],
        out_specs=pl.BlockSpec((tm, d), lambda i, j: (i, 0)),
        out_shape=jax.ShapeDtypeStruct((m, d), F32),
        scratch_shapes=[pltpu.VMEM((tm, d), BF16), pltpu.VMEM((tm, d), F32)],
        compiler_params=_cparams("parallel", "arbitrary"),
        name="ffn",
    )(x, g.reshape(1, d), w1, w3, w2)


def _moe_kernel(x_ref, g_ref, wr_ref, w1_ref, w3_ref, w2_ref, o_ref, h_sc, acc_sc, comb_sc):
    e = pl.program_id(1)
    j = pl.program_id(2)
    first = (e == 0) & (j == 0)
    last = (e == pl.num_programs(1) - 1) & (j == pl.num_programs(2) - 1)

    @pl.when(first)
    def _():
        hf = _rms_rows(x_ref[...], g_ref[...])
        h_sc[...] = hf.astype(BF16)
        acc_sc[...] = jnp.zeros_like(acc_sc)
        logits = _dot_3x(hf, wr_ref[...])
        lane = _iota(logits.shape, 1)
        lg = jnp.where(lane < N_EXPERTS, logits, -jnp.inf)
        m1 = jnp.max(lg, axis=-1, keepdims=True)
        i1 = jnp.min(jnp.where(lg == m1, lane, 128), axis=-1, keepdims=True)
        lg2 = jnp.where(lane == i1, -jnp.inf, lg)
        m2 = jnp.max(lg2, axis=-1, keepdims=True)
        i2 = jnp.min(jnp.where(lg2 == m2, lane, 128), axis=-1, keepdims=True)
        e2 = jnp.exp(m2 - m1)
        den = 1.0 + e2
        comb_sc[...] = jnp.where(lane == i1, 1.0 / den, 0.0) + jnp.where(lane == i2, e2 / den, 0.0)

    h = h_sc[...]
    a = _dot(h, w1_ref[0])
    b = _dot(h, w3_ref[0])
    t = (a * jax.nn.sigmoid(a)) * b
    comb = comb_sc[...]
    c = jnp.sum(jnp.where(_iota(comb.shape, 1) == e, comb, 0.0), axis=-1, keepdims=True)
    acc_sc[...] += c * _dot(t.astype(BF16), w2_ref[0])

    @pl.when(last)
    def _():
        o_ref[...] = x_ref[...] + acc_sc[...]


def moe(x, g, wr_pad, w1, w3, w2):
    m, d = x.shape
    ne, _, ff = w1.shape
    tm = _pick_tile(m, 512)
    tf = 256
    return pl.pallas_call(
        _moe_kernel,
        grid=(m // tm, ne, ff // tf),
        in_specs=[
            pl.BlockSpec((tm, d), lambda i, e, j: (i, 0)),
            pl.BlockSpec((1, d), lambda i, e, j: (0, 0)),
            pl.BlockSpec((d, 128), lambda i, e, j: (0, 0)),
            pl.BlockSpec((1, d, tf), lambda i, e, j: (e, 0, j)),
            pl.BlockSpec((1, d, tf), lambda i, e, j: (e, 0, j)),
            pl.BlockSpec((1, tf, d), lambda i, e, j: (e, j, 0)),
        ],
        out_specs=pl.BlockSpec((tm, d), lambda i, e, j: (i, 0)),
        out_shape=jax.ShapeDtypeStruct((m, d), F32),
        scratch_shapes=[pltpu.VMEM((tm, d), BF16), pltpu.VMEM((tm, d), F32), pltpu.VMEM((tm, 128), F32)],
        compiler_params=_cparams("parallel", "arbitrary", "arbitrary"),
        name="moe",
    )(x, g.reshape(1, d), wr_pad, w1, w3, w2)


def _gla_kernel(q_ref, k_ref, v_ref, a_ref, r_ref, wa2_ref, wa2t_ref, ba_ref, bacol_ref, gn_ref, s0_ref,
                o_ref, sout_ref, s_sc, b_sc, k_sc, v_sc):
    nb, c, _ = q_ref.shape
    ci = pl.program_id(1)

    @pl.when(ci == 0)
    def _():
        s_sc[...] = s0_ref[...]

    hk = GLA_H * GLA_DK
    hv = GLA_H * GLA_DV
    tri = (_iota((c, c), 0) >= _iota((c, c), 1)).astype(BF16)
    expand = (_iota((hk, hv), 0) // GLA_DK == _iota((hk, hv), 1) // GLA_DV)
    expand_bf = expand.astype(BF16)
    ones_v = _block_ones(hv, GLA_DV)
    wa2 = wa2_ref[...].astype(BF16)
    wa2t = wa2t_ref[...].astype(BF16)
    rowi = _iota((nb, c, hk), 1)

    qs, bs, os1 = [], [], []
    for n in range(nb):
        a_in = a_ref[n].astype(BF16)
        g = _log_sigmoid(_dot(a_in, wa2) + ba_ref[...]) / GLA_TAU
        b = _dot_e3(tri, g)
        gt = _log_sigmoid(_dot_nt(wa2t, a_in) + bacol_ref[...]) / GLA_TAU
        bl_col = jnp.sum(gt, axis=1, keepdims=True)
        q = q_ref[n] * (GLA_DK ** -0.5)
        k = k_ref[n]
        v = v_ref[n]
        s_old = s_sc[n]
        os1.append(_dot_3x(q * jnp.exp(b), s_old))
        kd = k * jnp.exp(b[c - 1:c, :] - b)
        upd = _dot_3x(kd, v, dot=_dot_tn)
        s_sc[n] = s_old * jnp.exp(bl_col) + jnp.where(expand, upd, 0.0)
        qs.append(q)
        b_sc[n] = b
        k_sc[n] = k
        v_sc[n] = v
        bs.append(b)
    q3 = jnp.stack(qs)
    b3 = jnp.stack(bs)

    def body(s, o2):
        b_s = b_sc[:, pl.ds(s, 1), :]
        k_s = k_sc[:, pl.ds(s, 1), :]
        v_s = v_sc[:, pl.ds(s, 1), :]
        dec = jnp.exp(jnp.where(rowi >= s, b3 - b_s, -jnp.inf))
        contrib = (q3 * k_s * dec).reshape(nb * c, hk)
        att = _dot_x2(contrib, expand_bf).reshape(nb, c, hv)
        return o2 + att * v_s

    o2 = lax.fori_loop(0, c, body, jnp.zeros((nb, c, hv), F32), unroll=4)
    for n in range(nb):
        o = os1[n] + o2[n]
        ms = _dot_x2(o * o, ones_v) * (1.0 / GLA_DV)
        o = o * lax.rsqrt(ms + EPS) * gn_ref[...]
        r = r_ref[n]
        o_ref[n] = o * (r * jax.nn.sigmoid(r))

    @pl.when(ci == pl.num_programs(1) - 1)
    def _():
        sout_ref[...] = s_sc[...]


def gla_mixer(q, k, v, a_in, r, wa2, ba, gn, s0_bd, nb):
    bsz, t, hk = q.shape
    hv = v.shape[-1]
    c = GLA_CHUNK if t % GLA_CHUNK == 0 else t
    tok = lambda w: pl.BlockSpec((nb, c, w), lambda b, i: (b, i, 0))
    full = lambda s: pl.BlockSpec(s, lambda b, i: (0,) * len(s))
    st = pl.BlockSpec((nb, hk, hv), lambda b, i: (b, 0, 0))
    return pl.pallas_call(
        _gla_kernel,
        grid=(bsz // nb, t // c),
        in_specs=[tok(hk), tok(hk), tok(hv), tok(GLA_RANK), tok(hv),
                  full((GLA_RANK, hk)), full((hk, GLA_RANK)), full((1, hk)), full((hk, 1)), full((1, hv)), st],
        out_specs=[tok(hv), st],
        out_shape=[jax.ShapeDtypeStruct((bsz, t, hv), F32), jax.ShapeDtypeStruct((bsz, hk, hv), F32)],
        scratch_shapes=[pltpu.VMEM((nb, hk, hv), F32), pltpu.VMEM((nb, c, hk), F32),
                        pltpu.VMEM((nb, c, hk), F32), pltpu.VMEM((nb, c, hv), F32)],
        compiler_params=_cparams("parallel", "arbitrary"),
        name="gla",
    )(q, k, v, a_in, r, wa2, wa2.T, ba.reshape(1, hk), ba.reshape(hk, 1), jnp.tile(gn, GLA_H).reshape(1, hv), s0_bd)


def gla_state_to_bd(s):
    b = s.shape[0]
    eye = jnp.eye(GLA_H, dtype=s.dtype)
    return jnp.einsum('bhkv,hg->bhkgv', s, eye).reshape(b, GLA_H * GLA_DK, GLA_H * GLA_DV)


def gla_state_from_bd(sbd):
    b = sbd.shape[0]
    s5 = sbd.reshape(b, GLA_H, GLA_DK, GLA_H, GLA_DV)
    return jnp.stack([s5[:, h, :, h, :] for h in range(GLA_H)], axis=1)


def _rwkv_prep_kernel(p_ref, first_ref, mu_ref, w0_ref, w2_ref, a0_ref, a2_ref, g2_ref, kk_ref, ka_ref, rk_ref,
                      r_o, w_o, k_o, v_o, kk_o, kka_o, g_o, bonus_o):
    p = p_ref[0]
    prev = jnp.where(_iota(p.shape, 0) == 0, first_ref[0, 0], pltpu.roll(p, 1, 0))
    xm = p + (prev - p) * mu_ref[...]
    n = BR_WIDTH
    r, k, v = xm[:, 0:n], xm[:, n:2 * n], xm[:, 2 * n:3 * n]
    xwa = xm[:, 3 * n:3 * n + 128]
    xg = xm[:, 3 * n + 128:]
    decay = jnp.exp(-math.exp(-0.5) * jax.nn.sigmoid(w0_ref[...] + _dot(jnp.tanh(xwa).astype(BF16), w2_ref[...])))
    a = jax.nn.sigmoid(a0_ref[...] + _dot(xwa.astype(BF16), a2_ref[...]))
    g = _dot(jax.nn.sigmoid(xg).astype(BF16), g2_ref[...])
    ones = _block_ones(n, RW_N)
    kk = k * kk_ref[...]
    kk = kk * lax.rsqrt(jnp.maximum(_dot_x2(kk * kk, ones), 1e-12))
    k2 = k * (1.0 + (a - 1.0) * ka_ref[...])
    r_o[0] = r
    w_o[0] = decay
    k_o[0] = k2
    v_o[0] = v
    kk_o[0] = kk
    kka_o[0] = kk * a
    g_o[0] = g
    bonus_o[0] = _dot_x2(r * k2 * rk_ref[...], ones) * v


def rwkv_prep(p, prev, mu, w0, w2, a0, a2, g2, k_k, k_a, r_k):
    bsz, t, d = p.shape
    tm = _pick_tile(t, 512)
    nt = t // tm
    first = jnp.concatenate([prev[:, None, :], p[:, tm - 1:t - 1:tm, :]], axis=1).reshape(bsz, nt, 1, d)
    n = BR_WIDTH
    w2p = jnp.concatenate([w2, jnp.zeros_like(w2)], axis=0).astype(BF16)
    a2p = jnp.concatenate([jnp.zeros_like(a2), a2], axis=0).astype(BF16)
    row = lambda v: v.reshape(1, -1)
    full = lambda s: pl.BlockSpec(s, lambda b, i: (0,) * len(s))
    tok = pl.BlockSpec((1, tm, n), lambda b, i: (b, i, 0))
    return pl.pallas_call(
        _rwkv_prep_kernel,
        grid=(bsz, nt),
        in_specs=[pl.BlockSpec((1, tm, d), lambda b, i: (b, i, 0)),
                  pl.BlockSpec((1, 1, 1, d), lambda b, i: (b, i, 0, 0)),
                  full((1, d)), full((1, n)), full((128, n)), full((1, n)), full((128, n)), full((128, n)),
                  full((1, n)), full((1, n)), full((1, n))],
        out_specs=[tok] * 8,
        out_shape=[jax.ShapeDtypeStruct((bsz, t, n), F32)] * 8,
        compiler_params=_cparams("parallel", "parallel"),
        name="rwkv_prep",
    )(p, first, row(mu), row(w0), w2p, row(a0), a2p, g2.astype(BF16), row(k_k), row(k_a), row(r_k))


def _rwkv_scan_kernel(r_ref, w_ref, k_ref, v_ref, kk_ref, kka_ref, s0_ref, o_ref, sout_ref, s_sc):
    _, tc, nb, n = r_ref.shape
    ti = pl.program_id(1)

    @pl.when(ti == 0)
    def _():
        s_sc[...] = s0_ref[0]

    ones = _block_ones(n, RW_N)
    diag = (_iota((RW_N, n), 0) == (_iota((RW_N, n), 1) % RW_N)).astype(F32)

    def seg(x3):
        return _dot_x2(x3.reshape(nb * RW_N, n), ones).reshape(nb, RW_N, n)

    def body(t, s):
        row = lambda ref: ref[0, t][:, None, :]
        kk, w, kka, k, r, v = row(kk_ref), row(w_ref), row(kka_ref), row(k_ref), row(r_ref), row(v_ref)
        vcol = seg(v * diag)
        sa = seg(s * kk)
        s = s * w - sa * kka + vcol * k
        ocol = seg(s * r)
        o_ref[0, t] = jnp.sum(ocol * diag, axis=1)
        return s

    s = lax.fori_loop(0, tc, body, s_sc[...], unroll=4)
    s_sc[...] = s

    @pl.when(ti == pl.num_programs(1) - 1)
    def _():
        sout_ref[0] = s


def rwkv_scan(r, w, k, v, kk, kka, s0, nb):
    bsz, t, n = r.shape
    bg = bsz // nb
    tc = _pick_tile(t, 256)
    tm = lambda x: x.reshape(bg, nb, t, n).transpose(0, 2, 1, 3)
    tok = pl.BlockSpec((1, tc, nb, n), lambda b, i: (b, i, 0, 0))
    st = pl.BlockSpec((1, nb, RW_N, n), lambda b, i: (b, 0, 0, 0))
    o, s = pl.pallas_call(
        _rwkv_scan_kernel,
        grid=(bg, t // tc),
        in_specs=[tok] * 6 + [st],
        out_specs=[tok, st],
        out_shape=[jax.ShapeDtypeStruct((bg, t, nb, n), F32), jax.ShapeDtypeStruct((bg, nb, RW_N, n), F32)],
        scratch_shapes=[pltpu.VMEM((nb, RW_N, n), F32)],
        compiler_params=_cparams("parallel", "arbitrary"),
        name="rwkv_scan",
    )(tm(r), tm(w), tm(k), tm(v), tm(kk), tm(kka), s0.reshape(bg, nb, RW_N, n))
    return o.transpose(0, 2, 1, 3).reshape(bsz, t, n), s.reshape(bsz, RW_N, n)


def _rwkv_post_kernel(o_ref, bonus_ref, g_ref, gn_ref, out_ref):
    o = o_ref[...]
    ms = _dot_x2(o * o, _block_ones(BR_WIDTH, RW_N)) * (1.0 / RW_N)
    out_ref[...] = (o * lax.rsqrt(ms + EPS) * gn_ref[...] + bonus_ref[...]) * g_ref[...]


def rwkv_post(o, bonus, g, gn):
    m, n = o.shape
    tm = _pick_tile(m, 1024)
    row = pl.BlockSpec((tm, n), lambda i: (i, 0))
    return pl.pallas_call(
        _rwkv_post_kernel,
        grid=(m // tm,),
        in_specs=[row, row, row, pl.BlockSpec((1, n), lambda i: (0, 0))],
        out_specs=row,
        out_shape=jax.ShapeDtypeStruct((m, n), F32),
        compiler_params=_cparams("parallel"),
        name="rwkv_post",
    )(o, bonus, g, jnp.tile(gn, RW_H).reshape(1, n))


def rwkv_mixer(p, prev, s0, mu, w0, w2, a0, a2, g2, k_k, k_a, r_k, gn, nb):
    bsz, t, _ = p.shape
    r, w, k, v, kk, kka, g, bonus = rwkv_prep(p, prev, mu, w0, w2, a0, a2, g2, k_k, k_a, r_k.reshape(-1))
    s0l = s0.transpose(0, 2, 1, 3).reshape(bsz, RW_N, BR_WIDTH)
    o, s = rwkv_scan(r, w, k, v, kk, kka, s0l, nb)
    out = rwkv_post(o.reshape(bsz * t, BR_WIDTH), bonus.reshape(bsz * t, BR_WIDTH), g.reshape(bsz * t, BR_WIDTH), gn)
    s_new = s.reshape(bsz, RW_N, RW_H, RW_N).transpose(0, 2, 1, 3)
    return out.reshape(bsz, t, BR_WIDTH), s_new, p[:, -1]


def _rel_bucket(dist):
    n = jnp.maximum(dist, 0)
    exact = REL_BUCKETS // 2
    nf = jnp.maximum(n, 1).astype(F32)
    large = exact + (jnp.log(nf / exact) / math.log(REL_MAX_DIST / exact) * (REL_BUCKETS - exact)).astype(I32)
    return jnp.where(n < exact, n, jnp.minimum(large, REL_BUCKETS - 1))


def _bucket_bits(bucket):
    return [((bucket >> i) & 1) == 1 for i in range(REL_BUCKETS.bit_length() - 1)]


def _bias_from_bits(bits, tab_ref, head):
    level = [tab_ref[b, head] for b in range(REL_BUCKETS)]
    for bit in bits:
        level = [jnp.where(bit, level[2 * i + 1], level[2 * i]) for i in range(len(level) // 2)]
    return level[0]


def _bias_from_bucket(bucket, tab_ref, head):
    return _bias_from_bits(_bucket_bits(bucket), tab_ref, head)


def _t5_tiles_kernel(tab_ref, o_ref, *, t, head0, window):
    h = pl.program_id(0)
    d = pl.program_id(1)
    dist = d * t + _iota((t, t), 0) - _iota((t, t), 1)
    val = _bias_from_bucket(_rel_bucket(dist), tab_ref, head0 + h)
    valid = dist >= 0
    if window:
        valid = valid & (dist <= WINDOW)
    o_ref[0, 0] = jnp.where(valid, val, NEG)


def t5_tiles(rel_bias, head0, nh, t, window):
    return pl.pallas_call(
        functools.partial(_t5_tiles_kernel, t=t, head0=head0, window=window),
        grid=(nh, 3),
        in_specs=[pl.BlockSpec(memory_space=pltpu.SMEM)],
        out_specs=pl.BlockSpec((1, 1, t, t), lambda h, d: (h, d, 0, 0)),
        out_shape=jax.ShapeDtypeStruct((nh, 3, t, t), F32),
        compiler_params=_cparams("parallel", "parallel"),
        name="t5_tiles",
    )(rel_bias)


def _pair_tables(nq, back):
    qi, kj, bt, fl = [], [], [], []
    for q in range(nq):
        lo = 0 if back is None else max(q - back, 0)
        for k in range(lo, q + 1):
            qi.append(q)
            kj.append(k)
            bt.append(min(q - k, 2))
            fl.append((1 if k == lo else 0) | (2 if k == q else 0))
    return tuple(jnp.asarray(np.asarray(a, np.int32)) for a in (qi, kj, bt, fl))


def _flash_kernel(qi_t, kj_t, bt_t, fl_t, q_ref, k_ref, v_ref, bias_ref, *rest, nrow, t, rc, pv_group, use_sel, epi, lam_init):
    rest = list(rest)
    sel_ref = rest.pop(0) if use_sel else None
    m_sc, l_sc, acc_sc = rest[-3:]
    o_ref = rest[-4]
    extras = rest[:-4]
    p = pl.program_id(2)
    flags = fl_t[p]
    dh = q_ref.shape[-1]

    @pl.when((flags & 1) != 0)
    def _():
        m_sc[...] = jnp.full_like(m_sc, NEG)
        l_sc[...] = jnp.zeros_like(l_sc)
        acc_sc[...] = jnp.zeros_like(acc_sc)

    k = k_ref[0, 0]
    v = v_ref[0, 0]
    bt = bt_t[p]
    nbias = bias_ref.shape[2]
    if use_sel:
        ns = sel_ref.shape[-1]
        blk = kj_t[p] * (t // SEL_BLOCK) + _iota((ns, t), 1) // SEL_BLOCK
        expand = (_iota((ns, t), 0) == blk).astype(BF16)
        chosen = _dot(sel_ref[0].astype(BF16), expand)
    m_all, l_all, acc_all = m_sc[...], l_sc[...], acc_sc[...]
    m_out, l_out, acc_out = [], [], []
    nchunk = nrow * t // rc
    scores = [_dot_nt(q_ref[0, 0, (c * rc) // t, pl.ds((c * rc) % t, rc), :], k) for c in range(nchunk)]
    for g0 in range(0, nchunk, pv_group):
        prs, alphas = [], []
        for c in range(g0, g0 + pv_group):
            r, off = divmod(c * rc, t)
            rows = slice(c * rc, (c + 1) * rc)
            s = scores[c] + bias_ref[0, bt, r % nbias, pl.ds(off, rc), :]
            if use_sel:
                s = jnp.where(chosen[off:off + rc] > 0.5, s, NEG)
            m_prev = m_all[rows]
            m_new = jnp.maximum(m_prev, jnp.max(s, axis=-1, keepdims=True))
            alpha = jnp.exp(m_prev - m_new)
            pr = jnp.exp(s - m_new)
            if use_sel:
                pr = jnp.where(s > 0.5 * NEG, pr, 0.0)
            l_out.append(alpha * l_all[rows] + jnp.sum(pr, axis=-1, keepdims=True))
            m_out.append(m_new)
            prs.append(pr.astype(BF16))
            alphas.append(alpha)
        grows = slice(g0 * rc, (g0 + pv_group) * rc)
        acc_out.append(jnp.concatenate(alphas, axis=0) * acc_all[grows] + _dot(jnp.concatenate(prs, axis=0), v))
    m_sc[...] = jnp.concatenate(m_out, axis=0)
    l_sc[...] = jnp.concatenate(l_out, axis=0)
    acc_sc[...] = jnp.concatenate(acc_out, axis=0)

    @pl.when((flags & 2) != 0)
    def _():
        o = acc_sc[...] / l_sc[...]
        if epi == "plain":
            o_ref[0, 0] = o.reshape(nrow, t, dh)
        elif epi == "diff":
            lam_ref, gn_ref = extras
            lv = lam_ref[...]
            lam = (jnp.exp(jnp.sum(lv[0:1] * lv[1:2], keepdims=True)) - jnp.exp(jnp.sum(lv[2:3] * lv[3:4], keepdims=True))
                   + lam_init)
            od = o[:t] - lam * o[t:]
            o_ref[0, 0, 0] = _rms_rows(od, gn_ref[...]) * (1.0 - lam_init)
        else:
            gate_ref, oc_ref, os_ref = extras
            g = jax.nn.sigmoid(gate_ref[0, 0])
            o_ref[0, 0] = (g[..., 0:1] * oc_ref[0, 0] + g[..., 1:2] * os_ref[0, 0]
                           + g[..., 2:3] * o.reshape(nrow, t, dh))


def flash(q, k, v, bias, tables, *, t, sel=None, epi="plain", extras=(), extra_specs=(), lam_init=None):
    bsz, hg, nrow, tq, dh = q.shape
    npairs = tables[0].shape[0]
    in_specs = [
        pl.BlockSpec((1, 1, nrow, t, dh), lambda b, h, p, qi, kj, bt, fl: (b, h, 0, qi[p], 0)),
        pl.BlockSpec((1, 1, t, dh), lambda b, h, p, qi, kj, bt, fl: (b, h, kj[p], 0)),
        pl.BlockSpec((1, 1, t, dh), lambda b, h, p, qi, kj, bt, fl: (b, h, kj[p], 0)),
        pl.BlockSpec((1,) + bias.shape[1:], lambda b, h, p, qi, kj, bt, fl: (h, 0, 0, 0, 0)),
    ]
    args = [q, k, v, bias]
    if sel is not None:
        in_specs.append(pl.BlockSpec((1, t, sel.shape[-1]), lambda b, h, p, qi, kj, bt, fl: (b, qi[p], 0)))
        args.append(sel)
    in_specs += list(extra_specs)
    args += list(extras)
    n_out = 1 if epi == "diff" else nrow
    return pl.pallas_call(
        functools.partial(_flash_kernel, nrow=nrow, t=t, rc=min(t, SCORE_CHUNK_ELEMS // t), pv_group=PV_GROUP,use_sel=sel is not None, epi=epi, lam_init=lam_init),
        grid_spec=pltpu.PrefetchScalarGridSpec(
            num_scalar_prefetch=4,
            grid=(bsz, hg, npairs),
            in_specs=in_specs,
            out_specs=pl.BlockSpec((1, 1, n_out, t, dh), lambda b, h, p, qi, kj, bt, fl: (b, h, 0, qi[p], 0)),
            scratch_shapes=[pltpu.VMEM((nrow * t, 1), F32), pltpu.VMEM((nrow * t, 1), F32),
                            pltpu.VMEM((nrow * t, dh), F32)],
        ),
        out_shape=jax.ShapeDtypeStruct((bsz, hg, n_out, tq, dh), F32),
        compiler_params=_cparams("parallel", "parallel", "arbitrary"),
        name="flash_" + epi + ("_sel" if sel is not None else ""),
    )(*tables, *args)


def _compress_kernel(pt_ref, *refs, npp):
    del pt_ref
    wt_ref = refs[npp]
    a_ref, b_ref = refs[npp + 1:]
    wt = wt_ref[...]
    for i in range(npp):
        x3 = refs[i][0].reshape(PAGE_SIZE // CMP_STRIDE, CMP_STRIDE, 2 * NSA_DH)
        a_ref[0, i * 8:(i + 1) * 8, :] = jnp.sum(x3 * wt[None, 0:CMP_STRIDE], axis=1)
        b_ref[0, i * 8:(i + 1) * 8, :] = jnp.sum(x3 * wt[None, CMP_STRIDE:], axis=1)


def compress(pool, pt, wt, npp):
    bsz, n_pages = pt.shape
    g = PAGE_SIZE // CMP_STRIDE
    page = lambda i: pl.BlockSpec((1, PAGE_SIZE, 2 * NSA_DH), lambda b, j, pt_ref: (pt_ref[b, j * npp + i], 0, 0))
    out = pl.BlockSpec((1, npp * g, 2 * NSA_DH), lambda b, j, pt_ref: (b, j, 0))
    shape = jax.ShapeDtypeStruct((bsz, n_pages * g, 2 * NSA_DH), F32)
    return pl.pallas_call(
        functools.partial(_compress_kernel, npp=npp),
        grid_spec=pltpu.PrefetchScalarGridSpec(
            num_scalar_prefetch=1,
            grid=(bsz, n_pages // npp),
            in_specs=[page(i) for i in range(npp)] + [pl.BlockSpec((CMP_BLOCK, 2 * NSA_DH), lambda b, j, pt_ref: (0, 0))],
            out_specs=[out, out],
        ),
        out_shape=[shape, shape],
        compiler_params=_cparams("parallel", "arbitrary"),
        name="nsa_compress",
    )(pt, *([pool] * npp), wt)


def _nsa_cmp_kernel(tab_ref, q_ref, a_ref, b_ref, tail_ref, o_ref, sel_ref, *, t, qpos0, n_cmp, n_sel):
    nc = a_ref.shape[1]
    ns = sel_ref.shape[-1]
    qi = pl.program_id(1)
    rown = _iota((nc, 2 * NSA_DH), 0)
    bsh = jnp.where(rown == nc - 1, tail_ref[0], pltpu.roll(b_ref[0], nc - 1, 0))
    kcv = jnp.where(rown < n_cmp, a_ref[0] + bsh, 0.0)
    vc = kcv[:, NSA_DH:].astype(BF16)
    kc_hi, kc_lo = _split2(kcv[:, :NSA_DH])
    start = _iota((nc, ns), 0) * CMP_STRIDE
    sblk = _iota((nc, ns), 1) * SEL_BLOCK
    ov = ((start < sblk + SEL_BLOCK) & (start + CMP_BLOCK > sblk)).astype(BF16)
    rc = max(8, min(t, SCORE_CHUNK_ELEMS // nc))
    scores = []
    for c in range(t // rc):
        qpos = qpos0 + qi * t + c * rc + _iota((rc, nc), 0)
        n = _iota((rc, nc), 1)
        dist = qpos - (n * CMP_STRIDE + CMP_BLOCK - 1)
        valid = (dist >= 0) & (n < n_cmp)
        bits = _bucket_bits(_rel_bucket(dist))
        psum = jnp.zeros((rc, nc), F32)
        for h in range(NSA_H):
            q_hi, q_lo = _split2(q_ref[0, 0, h, c * rc:(c + 1) * rc, :])
            s = _dot_nt(q_hi, kc_hi) + _dot_nt(q_lo, kc_hi) + _dot_nt(q_hi, kc_lo)
            sh = jnp.where(valid, s + _bias_from_bits(bits, tab_ref, h), NEG)
            m = jnp.max(sh, axis=-1, keepdims=True)
            p = jnp.where(valid, jnp.exp(sh - m), 0.0)
            p = p / jnp.maximum(jnp.sum(p, axis=-1, keepdims=True), 1e-30)
            o_ref[0, 0, h, c * rc:(c + 1) * rc, :] = _dot(p.astype(BF16), vc)
            psum = psum + p
        hi, mid, lo = _split3(psum)
        scores.append(_dot(hi, ov) + _dot(mid, ov) + _dot(lo, ov))
    score = scores[0] if len(scores) == 1 else jnp.concatenate(scores, axis=0)
    j = _iota((t, ns), 1)
    cur = (qpos0 + qi * t + _iota((t, ns), 0)) // SEL_BLOCK
    forced = (j == 0) | (j == cur) | (j == cur - 1)
    sc = jnp.where(j <= cur, score + jnp.where(forced, FORCE_SCORE, 0.0), -1.0)
    sc = jnp.where(j < n_sel, sc, -jnp.inf)
    chosen = jnp.zeros((t, ns), F32)
    for _ in range(min(SEL_TOPK, n_sel)):
        m = jnp.max(sc, axis=-1, keepdims=True)
        idx = jnp.min(jnp.where(sc == m, j, ns), axis=-1, keepdims=True)
        hit = j == idx
        chosen = jnp.where(hit, 1.0, chosen)
        sc = jnp.where(hit, -jnp.inf, sc)
    sel_ref[0] = chosen


def nsa_cmp(q4, a, b, tail, rel_bias_nsa, *, t, qpos0, n_cmp, n_sel):
    bsz, _, _, tq, dh = q4.shape
    nc = a.shape[1]
    ns = -(-n_sel // 128) * 128
    return pl.pallas_call(
        functools.partial(_nsa_cmp_kernel, t=t, qpos0=qpos0, n_cmp=n_cmp, n_sel=n_sel),
        grid=(bsz, tq // t),
        in_specs=[
            pl.BlockSpec(memory_space=pltpu.SMEM),
            pl.BlockSpec((1, 1, NSA_H, t, dh), lambda b, i: (b, 0, 0, i, 0)),
            pl.BlockSpec((1, nc, 2 * dh), lambda b, i: (b, 0, 0)),
            pl.BlockSpec((1, nc, 2 * dh), lambda b, i: (b, 0, 0)),
            pl.BlockSpec((1, 1, 2 * dh), lambda b, i: (b, 0, 0)),
        ],
        out_specs=[pl.BlockSpec((1, 1, NSA_H, t, dh), lambda b, i: (b, 0, 0, i, 0)),
                   pl.BlockSpec((1, t, ns), lambda b, i: (b, i, 0))],
        out_shape=[jax.ShapeDtypeStruct((bsz, 1, NSA_H, tq, dh), F32), jax.ShapeDtypeStruct((bsz, tq, ns), F32)],
        compiler_params=_cparams("parallel", "parallel"),
        name="nsa_cmp",
    )(rel_bias_nsa, q4, a, b, tail)


def _cmp_weight_tile(w_cmp):
    return jnp.repeat(w_cmp.T, NSA_DH, axis=1)


def nsa_prompt(qn, kvn, n_g, w_cmp, Wc):
    bsz, t_len, _ = qn.shape
    t = _pick_tile(t_len, 256)
    q4f = (qn.reshape(bsz, t_len, NSA_H, NSA_DH) * NSA_DH ** -0.5).transpose(0, 2, 1, 3)[:, None]
    q4 = q4f.astype(BF16)
    n_pages = t_len // PAGE_SIZE
    pool = kvn[:, :, 0:2 * NSA_DH].reshape(bsz * n_pages, PAGE_SIZE, 2 * NSA_DH)
    pt = jnp.arange(bsz * n_pages, dtype=I32).reshape(bsz, n_pages)
    a, b = compress(pool, pt, _cmp_weight_tile(w_cmp), _pick_tile(n_pages, 16))
    o_c, chosen = nsa_cmp(q4f, a, b, jnp.zeros((bsz, 1, 2 * NSA_DH), F32), Wc['rel_nsa'], t=t, qpos0=0,
                          n_cmp=t_len // CMP_STRIDE - 1, n_sel=t_len // SEL_BLOCK)
    kv = lambda i: kvn[:, :, i * NSA_DH:(i + 1) * NSA_DH].astype(BF16)[:, None]
    nq = t_len // t
    o_s = flash(q4, kv(2), kv(3), Wc['tiles_nsa'], _pair_tables(nq, None), t=t, sel=chosen)
    gates = n_g.reshape(bsz, t_len, NSA_H, 3).transpose(0, 2, 1, 3)[:, None]
    tok = lambda w: pl.BlockSpec((1, 1, NSA_H, t, w), lambda b, h, p, qi, kj, bt, fl: (b, h, 0, qi[p], 0))
    o = flash(q4, kv(4), kv(5), Wc['tiles_win'], _pair_tables(nq, WINDOW // t), t=t, epi="win",
              extras=(gates, o_c, o_s), extra_specs=(tok(3), tok(NSA_DH), tok(NSA_DH)))
    return o[:, 0].transpose(0, 2, 1, 3).reshape(bsz, t_len, NSA_H * NSA_DH)


def _lam_init(l):
    return 0.8 - 0.6 * math.exp(-0.3 * l)


def diff_prompt(d_q, d_k, d_v, lam_rows, lam_init, gn, Wc):
    bsz, t_len, _ = d_q.shape
    t = _pick_tile(t_len, 512)
    q = d_q.reshape(bsz, t_len, DF_H, 2 * DF_D).transpose(0, 2, 1, 3) * DF_D ** -0.5
    lane = jnp.arange(2 * DF_D) < DF_D
    q2 = jnp.stack([jnp.where(lane, q, 0.0), jnp.where(lane, 0.0, q)], axis=2).astype(BF16)
    k = d_k.reshape(bsz, t_len, DF_H, 2 * DF_D).transpose(0, 2, 1, 3).astype(BF16)
    v = d_v.reshape(bsz, t_len, DF_H, DF_DV).transpose(0, 2, 1, 3).astype(BF16)
    full = lambda s: pl.BlockSpec(s, lambda b, h, p, qi, kj, bt, fl: (0,) * len(s))
    o = flash(q2, k, v, Wc['tiles_diff'], _pair_tables(t_len // t, None), t=t, epi="diff",
              lam_init=lam_init, extras=(lam_rows, gn.reshape(1, DF_DV)),
              extra_specs=(full((4, DF_D)), full((1, DF_DV))))
    return o[:, :, 0].transpose(0, 2, 1, 3).reshape(bsz, t_len, BR_WIDTH)


NEW_PAD = 16


def _paged_attn_kernel(pt_ref, tab_ref, q_ref, *refs, npp, head_cols, kpos0, qpos0, t_new, window, use_sel):
    del pt_ref
    pages = refs[:npp]
    newk_ref, newv_ref = refs[npp], refs[npp + 1]
    sel_ref = refs[npp + 2] if use_sel else None
    o_ref, m_sc, l_sc, acc_sc = refs[-4:]
    j = pl.program_id(1)
    tq = t_new
    ng = len(head_cols)
    nrow = ng * tq
    ks = npp * PAGE_SIZE
    hkv, rh = q_ref.shape[1], q_ref.shape[2]
    hrows = lambda x, h: x[h * rh:(h + 1) * rh]

    @pl.when(j == 0)
    def _():
        m_sc[...] = jnp.full_like(m_sc, NEG)
        l_sc[...] = jnp.zeros_like(l_sc)
        acc_sc[...] = jnp.zeros_like(acc_sc)

    qs = [q_ref[0, h] for h in range(hkv)]

    def update(s, kpos, extra_valid, blocks, pv):
        n = s.shape[-1]
        dist = (qpos0 + _iota((tq, n), 0)) - kpos
        valid = dist >= 0
        if window:
            valid = valid & (dist <= WINDOW)
        if extra_valid is not None:
            valid = valid & extra_valid
        if use_sel:
            nsb = sel_ref.shape[-1]
            expand = (_iota((nsb, n), 0) == blocks).astype(BF16)
            valid = valid & (_dot(sel_ref[0].astype(BF16), expand) > 0.5)
        bits = _bucket_bits(_rel_bucket(dist))
        bias = {c: _bias_from_bits(bits, tab_ref, c) for c in sorted(set(head_cols))}
        s3 = s.reshape(ng, tq, n) + jnp.stack([bias[c] for c in head_cols])
        s = jnp.where(valid[None], s3, NEG).reshape(nrow, n)
        m_prev = m_sc[...]
        m_new = jnp.maximum(m_prev, jnp.max(s, axis=-1, keepdims=True))
        alpha = jnp.exp(m_prev - m_new)
        pr = jnp.where(s > 0.5 * NEG, jnp.exp(s - m_new), 0.0)
        l_sc[...] = alpha * l_sc[...] + jnp.sum(pr, axis=-1, keepdims=True)
        acc_sc[...] = alpha * acc_sc[...] + pv(pr.astype(BF16))
        m_sc[...] = m_new

    cat = lambda xs, axis: xs[0] if len(xs) == 1 else jnp.concatenate(xs, axis=axis)
    kt = [[pages[i][0, 0, 0, h].astype(BF16) for h in range(hkv)] for i in range(npp)]
    vt = [[pages[i][0, 0, 1, h].astype(BF16) for h in range(hkv)] for i in range(npp)]
    s = cat([cat([_dot(qs[h], kt[i][h]) for i in range(npp)], 1) for h in range(hkv)], 0)
    kpos = kpos0 + j * ks + _iota((tq, ks), 1)
    blocks = (kpos0 + j * ks + _iota((1, ks), 1)) // SEL_BLOCK

    def pv_pages(pb):
        outs = []
        for h in range(hkv):
            ph = hrows(pb, h)
            out = _dot_nt(ph[:, 0:PAGE_SIZE], vt[0][h])
            for i in range(1, npp):
                out = out + _dot_nt(ph[:, i * PAGE_SIZE:(i + 1) * PAGE_SIZE], vt[i][h])
            outs.append(out)
        return cat(outs, 0)

    update(s, kpos, None, blocks, pv_pages)

    @pl.when(j == pl.num_programs(1) - 1)
    def _():
        nk = [newk_ref[0, h].astype(BF16) for h in range(hkv)]
        nv = [newv_ref[0, h].astype(BF16) for h in range(hkv)]
        col = _iota((tq, NEW_PAD), 1)
        update(cat([_dot_nt(qs[h], nk[h]) for h in range(hkv)], 0), qpos0 + col, col < t_new,
               (qpos0 + _iota((1, NEW_PAD), 1)) // SEL_BLOCK,
               lambda pb: cat([_dot(hrows(pb, h), nv[h]) for h in range(hkv)], 0))
        o_ref[0] = acc_sc[...] / l_sc[...]


def paged_attn(q, pool, layer, pt, page_index, newk, newv, tab, *, npp, head_cols, kpos0, qpos0, window=False,
               sel=None):
    bsz, hkv, rh, dh = q.shape
    nrow = hkv * rh
    lw = dh
    n_pages = pt.shape[1]
    t_new = nrow // len(head_cols)
    page = lambda i: pl.BlockSpec((1, 1, 2, hkv, dh, PAGE_SIZE),
                                  lambda b, j, pt_ref: (layer,) + page_index(b, j * npp + i, pt_ref))
    new_spec = pl.BlockSpec((1, hkv, NEW_PAD, dh), lambda b, j, pt_ref: (b, 0, 0, 0))
    in_specs = [pl.BlockSpec(memory_space=pltpu.SMEM), pl.BlockSpec((1, hkv, rh, dh), lambda b, j, pt_ref: (b, 0, 0, 0))]
    in_specs += [page(i) for i in range(npp)]
    in_specs += [new_spec, new_spec]
    args = [tab, q] + [pool] * npp + [newk, newv]
    if sel is not None:
        in_specs.append(pl.BlockSpec((1, t_new, sel.shape[-1]), lambda b, j, pt_ref: (b, 0, 0)))
        args.append(sel)
    return pl.pallas_call(
        functools.partial(_paged_attn_kernel, npp=npp, head_cols=tuple(head_cols), kpos0=kpos0, qpos0=qpos0,
                          t_new=t_new, window=window, use_sel=sel is not None),
        grid_spec=pltpu.PrefetchScalarGridSpec(
            num_scalar_prefetch=1,
            grid=(bsz, n_pages // npp),
            in_specs=in_specs,
            out_specs=pl.BlockSpec((1, nrow, lw), lambda b, j, pt_ref: (b, 0, 0)),
            scratch_shapes=[pltpu.VMEM((nrow, 1), F32), pltpu.VMEM((nrow, 1), F32), pltpu.VMEM((nrow, lw), F32)],
        ),
        out_shape=jax.ShapeDtypeStruct((bsz, nrow, lw), F32),
        compiler_params=_cparams("parallel", "arbitrary"),
        name="paged_attn",
    )(pt, *args)


def _nsa_combine_kernel(g_ref, oc_ref, os_ref, ow_ref, o_ref):
    g = jax.nn.sigmoid(g_ref[...])
    o_ref[...] = g[..., 0:1] * oc_ref[...] + g[..., 1:2] * os_ref[...] + g[..., 2:3] * ow_ref[...]


def nsa_combine(gates, o_c, o_s, o_w):
    n, dh = o_c.shape
    full = lambda w: pl.BlockSpec((n, w), lambda i: (0, 0))
    return pl.pallas_call(
        _nsa_combine_kernel, grid=(1,),
        in_specs=[full(3), full(dh), full(dh), full(dh)], out_specs=full(dh),
        out_shape=jax.ShapeDtypeStruct((n, dh), F32), name="nsa_combine",
    )(gates, o_c, o_s, o_w)


def _pad_rows(x, n):
    return jnp.pad(x, ((0, 0), (0, n - x.shape[1]), (0, 0)))


def _compress_t_kernel(pt_ref, *refs, npp):
    del pt_ref
    pages = refs[:npp]
    whi_ref, wlo_ref = refs[npp], refs[npp + 1]
    a_ref, b_ref = refs[npp + 2:]
    for kv in range(2):
        acc = None
        for i in range(npp):
            xh, xl = _split2(pages[i][0, 0, kv, 0])
            y = _dot(xh, whi_ref[kv, i]) + _dot(xl, whi_ref[kv, i]) + _dot(xh, wlo_ref[kv, i])
            acc = y if acc is None else acc + y
        a_ref[0, kv] = acc[:, :128]
        b_ref[0, kv] = acc[:, 128:]


def compress_t(pool, layer, pt, w_cmp):
    npp = 16
    bsz, n_pages = pt.shape
    g = PAGE_SIZE // CMP_STRIDE
    r = np.arange(PAGE_SIZE)
    grp = jnp.asarray((r[:, None] // CMP_STRIDE == np.arange(g)[None, :]).astype(np.float32))
    slot = jnp.eye(npp, dtype=F32)
    halves = []
    for half in range(2):
        wr = w_cmp[:, half * CMP_STRIDE + r % CMP_STRIDE]
        halves.append(jnp.einsum('kr,rg,ip->kirpg', wr, grp, slot).reshape(2, npp, PAGE_SIZE, npp * g))
    wbig = jnp.concatenate(halves, axis=-1)
    whi = wbig.astype(BF16)
    wlo = (wbig - whi.astype(F32)).astype(BF16)
    dh = pool.shape[-2]
    page = lambda i: pl.BlockSpec((1, 1, 2, 1, dh, PAGE_SIZE),
                                  lambda b, j, pt_ref: (layer, pt_ref[b, j * npp + i], 0, 0, 0, 0))
    wspec = pl.BlockSpec((2, npp, PAGE_SIZE, 2 * npp * g), lambda b, j, pt_ref: (0, 0, 0, 0))
    out = pl.BlockSpec((1, 2, dh, npp * g), lambda b, j, pt_ref: (b, 0, 0, j))
    shape = jax.ShapeDtypeStruct((bsz, 2, dh, n_pages * g), F32)
    return pl.pallas_call(
        functools.partial(_compress_t_kernel, npp=npp),
        grid_spec=pltpu.PrefetchScalarGridSpec(
            num_scalar_prefetch=1,
            grid=(bsz, n_pages // npp),
            in_specs=[page(i) for i in range(npp)] + [wspec, wspec],
            out_specs=[out, out],
        ),
        out_shape=[shape, shape],
        compiler_params=_cparams("parallel", "arbitrary"),
        name="nsa_compress_t",
    )(pt, *([pool] * npp), whi, wlo)


def _paged_index(b, page, pt_ref):
    return (pt_ref[b, page], 0, 0, 0, 0)


def _window_index(b, page, pt_ref):
    return (b, 0, 0, 0, page)


def nsa_sample(qn, kvn, n_g, w_cmp, Wc, layer, pool_cmp, pool_sel, page_table, win_t):
    bsz, t_len, _ = qn.shape
    lw = 2 * NSA_DH
    total = PAST_LEN + t_len
    n_grp = -(-total // CMP_STRIDE)
    n_cmp = n_grp - CMP_BLOCK // CMP_STRIDE + 1
    n_sel = -(-total // SEL_BLOCK)
    qf = (qn.reshape(bsz, t_len, NSA_H, NSA_DH) * NSA_DH ** -0.5).transpose(0, 2, 1, 3)
    q1 = qf.astype(BF16).reshape(bsz, 1, NSA_H * t_len, NSA_DH)
    at, bt = compress_t(pool_cmp, layer, page_table, w_cmp)
    rows = lambda x: x.transpose(0, 3, 1, 2).reshape(bsz, x.shape[-1], lw)
    new_page = _pad_rows(kvn[:, :, 0:lw], PAGE_SIZE)
    _, b_new = compress(new_page, jnp.arange(bsz, dtype=I32).reshape(bsz, 1), _cmp_weight_tile(w_cmp), 1)
    o_c, chosen = nsa_cmp(qf[:, None], rows(at), rows(bt), b_new[:, 0:1], Wc['rel_nsa'], t=t_len, qpos0=PAST_LEN,
                          n_cmp=n_cmp, n_sel=n_sel)
    new = lambda i: _pad_rows(kvn[:, :, i * NSA_DH:(i + 1) * NSA_DH], NEW_PAD)[:, None]
    heads = tuple(range(NSA_H))
    o_s = paged_attn(q1, pool_sel, layer, page_table, _paged_index, new(2), new(3), Wc['rel_nsa'], npp=16,
                     head_cols=heads, kpos0=0, qpos0=PAST_LEN, sel=chosen)
    wb = win_t.shape[-1]
    wpages = wb // PAGE_SIZE
    o_w = paged_attn(q1, win_t, layer, jnp.zeros((bsz, wpages), I32), _window_index, new(4), new(5), Wc['rel_nsa'],
                     npp=wpages, head_cols=heads, kpos0=PAST_LEN - wb, qpos0=PAST_LEN, window=True)
    n = bsz * NSA_H * t_len
    gates = n_g.reshape(bsz, t_len, NSA_H, 3).transpose(0, 2, 1, 3).reshape(n, 3)
    o = nsa_combine(gates, o_c.reshape(n, NSA_DH), o_s.reshape(n, NSA_DH), o_w.reshape(n, NSA_DH))
    return o.reshape(bsz, NSA_H, t_len, NSA_DH).transpose(0, 2, 1, 3).reshape(bsz, t_len, NSA_H * NSA_DH)


def _diff_post_kernel(o_ref, lam_ref, gn_ref, out_ref, *, lam_init):
    lv = lam_ref[...]
    lam = (jnp.exp(jnp.sum(lv[0:1] * lv[1:2], keepdims=True)) - jnp.exp(jnp.sum(lv[2:3] * lv[3:4], keepdims=True))
           + lam_init)
    od = o_ref[0] - lam * o_ref[1]
    out_ref[...] = _rms_rows(od, gn_ref[...]) * (1.0 - lam_init)


def diff_post(o2, lam_rows, lam_init, gn):
    _, n, dv = o2.shape
    return pl.pallas_call(
        functools.partial(_diff_post_kernel, lam_init=lam_init), grid=(1,),
        in_specs=[pl.BlockSpec((2, n, dv), lambda i: (0, 0, 0)), pl.BlockSpec((4, DF_D), lambda i: (0, 0)),
                  pl.BlockSpec((1, dv), lambda i: (0, 0))],
        out_specs=pl.BlockSpec((n, dv), lambda i: (0, 0)),
        out_shape=jax.ShapeDtypeStruct((n, dv), F32), name="diff_post",
    )(o2, lam_rows, gn.reshape(1, dv))


def diff_sample(d_q, d_k, d_v, lam_rows, lam_init, gn, Wc, layer, pool, page_table):
    bsz, t_len, _ = d_q.shape
    q = (d_q.reshape(bsz, t_len, DF_H, 2 * DF_D) * DF_D ** -0.5).transpose(0, 2, 1, 3)
    lane = jnp.arange(2 * DF_D) < DF_D
    q2 = jnp.stack([jnp.where(lane, q, 0.0), jnp.where(lane, 0.0, q)], axis=2).astype(BF16)
    q2 = q2.reshape(bsz, DF_H, 2 * t_len, 2 * DF_D)
    new = lambda x: _pad_rows(x, NEW_PAD).reshape(bsz, NEW_PAD, DF_H, DF_DV).transpose(0, 2, 1, 3)
    head_cols = tuple(NSA_H + h for h in range(DF_H) for _ in range(2))
    o = paged_attn(q2, pool, layer, page_table, _paged_index, new(d_k), new(d_v), Wc['rel_all'], npp=16,
                   head_cols=head_cols, kpos0=0, qpos0=PAST_LEN)
    n = bsz * DF_H * t_len
    o2 = o.reshape(bsz, DF_H, 2, t_len, DF_DV).transpose(2, 0, 1, 3, 4).reshape(2, n, DF_DV)
    od = diff_post(o2, lam_rows, lam_init, gn).reshape(bsz, DF_H, t_len, DF_DV)
    return od.transpose(0, 2, 1, 3).reshape(bsz, t_len, BR_WIDTH)


def _jx_masked_softmax(s, mask):
    s = jnp.where(mask, s.astype(F32), -1e30)
    m = jnp.max(s, axis=-1, keepdims=True)
    p = jnp.where(mask, jnp.exp(s - m), 0.0)
    return p / jnp.maximum(jnp.sum(p, axis=-1, keepdims=True), 1e-30)


def _jx_t5_bias(dist, table):
    return jnp.moveaxis(table[_rel_bucket(dist)].astype(F32), -1, -2)


def _jx_nsa_compress(k, w):
    B, T, D = k.shape
    sub = k.reshape(B, T // CMP_STRIDE, CMP_STRIDE, D)
    r = CMP_BLOCK // CMP_STRIDE
    n = T // CMP_STRIDE - r + 1
    out = jnp.einsum('bnsd,s->bnd', sub[:, 0:n], w[0:CMP_STRIDE])
    for j in range(1, r):
        out = out + jnp.einsum('bnsd,s->bnd', sub[:, j:j + n], w[j * CMP_STRIDE:(j + 1) * CMP_STRIDE])
    ends = jnp.arange(n) * CMP_STRIDE + CMP_BLOCK - 1
    return out, ends


def _jx_nsa_sample(q, kv, gates, w_cmp, btab, pool_cmp, pool_sel, page_table, win_buf):
    B, T = q.shape[:2]
    qpos = PAST_LEN + jnp.arange(T)
    total = PAST_LEN + T
    past_cmp = pool_cmp[page_table].reshape(B, PAST_LEN, 2, NSA_DH)
    seq_cmp = jnp.concatenate([past_cmp, kv[:, :, 0:2].astype(past_cmp.dtype)], axis=1)
    seq_cmp = jnp.pad(seq_cmp, ((0, 0), (0, (-total) % CMP_STRIDE), (0, 0), (0, 0)))
    kc, cend = _jx_nsa_compress(seq_cmp[:, :, 0], w_cmp[0])
    vc, _ = _jx_nsa_compress(seq_cmp[:, :, 1], w_cmp[1])
    n_sel = -(-total // SEL_BLOCK)
    start = jnp.arange(kc.shape[1])[:, None] * CMP_STRIDE
    selb = jnp.arange(n_sel)[None, :] * SEL_BLOCK
    ov = ((start < selb + SEL_BLOCK) & (start + CMP_BLOCK > selb)).astype(F32)
    dist = qpos[:, None] - cend[None, :]
    s = jnp.einsum('bqhd,bnd->bqhn', q, kc) * NSA_DH ** -0.5 + _jx_t5_bias(dist, btab)[None]
    p = _jx_masked_softmax(s, (dist >= 0)[None, :, None, :])
    o_c = jnp.einsum('bqhn,bnd->bqhd', p, vc)
    score = jnp.einsum('bqhn,ns->bqs', p, ov)
    cur = (qpos // SEL_BLOCK)[:, None]
    j = jnp.arange(n_sel)[None, :]
    forced = (j == 0) | (j == cur) | (j == cur - 1)
    sc = jnp.where(j <= cur, score + jnp.where(forced, FORCE_SCORE, 0.0), -1.0)
    idx = lax.top_k(sc, min(SEL_TOPK, n_sel))[1]
    nb_past = PAST_LEN // SEL_BLOCK
    nb_new = n_sel - nb_past
    bpp = PAGE_SIZE // SEL_BLOCK
    bidx = jnp.arange(B)[:, None, None]
    jp = jnp.minimum(idx, nb_past - 1)
    phys = page_table[bidx, jp // bpp]
    rows = (jp % bpp)[..., None] * SEL_BLOCK + jnp.arange(SEL_BLOCK)
    g_past = pool_sel[phys[..., None], rows]
    new_sel = jnp.pad(kv[:, :, 2:4], ((0, 0), (0, nb_new * SEL_BLOCK - T), (0, 0), (0, 0)))
    new_sel = new_sel.reshape(B, nb_new, SEL_BLOCK, 2, NSA_DH)
    g_new = new_sel[bidx, jnp.clip(idx - nb_past, 0, nb_new - 1)]
    g = jnp.where((idx < nb_past)[..., None, None, None], g_past, g_new.astype(g_past.dtype))
    Bq, Q, Kk = idx.shape
    kpos = (idx[..., None] * SEL_BLOCK + jnp.arange(SEL_BLOCK)).reshape(Bq, Q, Kk * SEL_BLOCK)
    dist_s = qpos[None, :, None] - kpos
    ks = g[..., 0, :].reshape(Bq, Q, Kk * SEL_BLOCK, NSA_DH)
    vs = g[..., 1, :].reshape(Bq, Q, Kk * SEL_BLOCK, NSA_DH)
    s2 = jnp.einsum('bqhd,bqld->bqhl', q, ks) * NSA_DH ** -0.5 + _jx_t5_bias(dist_s, btab)
    p2 = _jx_masked_softmax(s2, (dist_s >= 0)[:, :, None, :])
    o_s = jnp.einsum('bqhl,bqld->bqhd', p2, vs)
    wb = win_buf.shape[1]
    wseq = jnp.concatenate([win_buf, kv[:, :, 4:6].astype(win_buf.dtype)], axis=1)
    kposw = PAST_LEN - wb + jnp.arange(wb + T)
    dist_w = qpos[:, None] - kposw[None, :]
    mask = (dist_w >= 0) & (dist_w <= WINDOW) & (kposw >= 0)[None, :]
    s3 = jnp.einsum('bqhd,bld->bqhl', q, wseq[:, :, 0]) * NSA_DH ** -0.5 + _jx_t5_bias(dist_w, btab)[None]
    p3 = _jx_masked_softmax(s3, mask[None, :, None, :])
    o_w = jnp.einsum('bqhl,bld->bqhd', p3, wseq[:, :, 1])
    o = gates[..., 0:1] * o_c + gates[..., 1:2] * o_s + gates[..., 2:3] * o_w
    return o.reshape(B, T, NSA_H * NSA_DH), wseq[:, T:]


def _jx_diff_sample(q, k, v, lam, btab, pool, page_table):
    B, T = q.shape[:2]
    past = pool[page_table].reshape(B, PAST_LEN, 2, DF_H, DF_DV)
    segs = ((past[:, :, 0].reshape(B, PAST_LEN, DF_H, 2, DF_D), past[:, :, 1], jnp.arange(PAST_LEN)),
            (k, v, PAST_LEN + jnp.arange(T)))
    qpos = PAST_LEN + jnp.arange(T)
    scores, masks = [], []
    for kk, vv, kpos in segs:
        dist = qpos[:, None] - kpos[None, :]
        s = jnp.einsum('bqhcd,blhcd->bqhcl', q, kk) * DF_D ** -0.5
        scores.append(s + _jx_t5_bias(dist, btab)[None, :, :, None, :])
        masks.append(dist >= 0)
    p = _jx_masked_softmax(jnp.concatenate(scores, axis=-1), jnp.concatenate(masks, axis=-1)[None, :, None, None, :])
    w = p[..., 0, :] - lam * p[..., 1, :]
    o, off = 0.0, 0
    for kk, vv, kpos in segs:
        n = kpos.shape[0]
        o = o + jnp.einsum('bqhl,blhv->bqhv', w[..., off:off + n], vv)
        off += n
    return o


def prepare_weights(W):
    Wc = {}
    w_in = W['w_in']
    Wc['w_main'] = jnp.pad(w_in[:, :, :MAIN_COLS], ((0, 0), (0, 0), (0, MAIN_PAD - MAIN_COLS))).astype(BF16)
    Wc['w_gate'] = w_in[:, :, MAIN_COLS:].reshape(DEPTH, D_MODEL, N_BRANCH, D_MODEL).transpose(0, 2, 1, 3).astype(BF16)
    for name in ('w_br', 'w_out', 'ffn_w1', 'ffn_w3', 'ffn_w2', 'moe_w1', 'moe_w3', 'moe_w2'):
        Wc[name] = W[name].astype(BF16)
    Wc['moe_router'] = jnp.pad(W['moe_router'], ((0, 0), (0, 0), (0, 128 - N_EXPERTS)))
    rel = W['rel_bias']
    Wc['rel_nsa'] = rel[:, :NSA_H]
    Wc['rel_all'] = rel
    Wc['tiles_nsa'] = t5_tiles(rel, 0, NSA_H, 256, False).transpose(1, 0, 2, 3)[None]
    Wc['tiles_win'] = t5_tiles(rel, 0, NSA_H, 256, True).transpose(1, 0, 2, 3)[None]
    Wc['tiles_diff'] = t5_tiles(rel, NSA_H, DF_H, 512, False)[:, :, None]
    return Wc


def token_mix(h, l, W, Wc, past, page_table):
    B, T, _ = h.shape
    m = B * T
    nb_state = 2 if past is None else 8
    proj = matmul(h.reshape(m, D_MODEL), Wc['w_main'][l], MAIN_PAD // 2).reshape(B, T, MAIN_PAD)
    sl = lambda i: proj[:, :, IN_OFFS[i]:IN_OFFS[i + 1]]
    g_q, g_k, g_v, g_a, g_r, n_q, n_kv, n_g, d_q, d_k, d_v, rw = [sl(i) for i in range(12)]
    st = {}
    s0 = jnp.zeros((B, GLA_H * GLA_DK, GLA_H * GLA_DV), F32) if past is None else gla_state_to_bd(past['gla'][l])
    o_a, s_bd = gla_mixer(g_q, g_k, g_v, g_a, g_r, W['gla_wa2'][l], W['gla_ba'][l], W['gla_norm_g'][l], s0, nb_state)
    st['gla'] = gla_state_from_bd(s_bd)
    kvn = n_kv.reshape(B, T, 6, NSA_DH)
    if past is None:
        o_b = nsa_prompt(n_q, n_kv, n_g, W['nsa_cmp_w'][l], Wc)
        st['win'] = kvn[:, T - min(WINDOW, T):, 4:6]
    else:
        o_b = nsa_sample(n_q, n_kv, n_g, W['nsa_cmp_w'][l], Wc, l, past['cmp_t'], past['sel_t'], page_table,
                         past['win_t'])
        st['win'] = jnp.concatenate([past['win'][l], kvn[:, :, 4:6]], axis=1)[:, T:]
    st['cmp'] = kvn[:, :, 0:2]
    st['sel'] = kvn[:, :, 2:4]
    prev = jnp.zeros((B, RW_PROJ), F32) if past is None else past['shift'][l]
    s0 = jnp.zeros((B, RW_H, RW_N, RW_N), F32) if past is None else past['rwkv'][l]
    o_c, st['rwkv'], st['shift'] = rwkv_mixer(
        rw, prev, s0, W['rw_mu'][l], W['rw_w0'][l], W['rw_w2'][l], W['rw_a0'][l], W['rw_a2'][l], W['rw_g2'][l],
        W['rw_kk'][l], W['rw_ka'][l], W['rw_rk'][l], W['rw_norm_g'][l], nb_state)
    lam_init = _lam_init(l)
    if past is None:
        o_d = diff_prompt(d_q, d_k, d_v, W['df_lam'][l], lam_init, W['df_norm_g'][l], Wc)
    else:
        o_d = diff_sample(d_q, d_k, d_v, W['df_lam'][l], lam_init, W['df_norm_g'][l], Wc, l, past['diff_t'],
                          page_table)
    st['diff'] = jnp.stack([d_k.reshape(B, T, DF_H, 2 * DF_D), d_v.reshape(B, T, DF_H, DF_DV)], axis=2)
    return [t.reshape(m, BR_WIDTH) for t in (o_a, o_b, o_c, o_d)], st


def trunk(x, W, Wc, cache, page_table):
    B, T, _ = x.shape
    x2 = x.reshape(B * T, D_MODEL)
    new = {}
    for l in range(DEPTH):
        past = cache
        h = rmsnorm(x2, W['norm1_g'][l], BF16)
        brs, st = token_mix(h.reshape(B, T, D_MODEL), l, W, Wc, past, page_table)
        x2 = merge(h, brs, x2, Wc['w_gate'][l], Wc['w_br'][l], Wc['w_out'][l])
        j = l // 2
        if l % 2 == 0:
            x2 = ffn(x2, W['norm2_g'][l], Wc['ffn_w1'][j], Wc['ffn_w3'][j], Wc['ffn_w2'][j])
        else:
            x2 = moe(x2, W['norm2_g'][l], Wc['moe_router'][j], Wc['moe_w1'][j], Wc['moe_w3'][j], Wc['moe_w2'][j])
        for name, arr in st.items():
            new.setdefault(name, []).append(arr)
    y = rmsnorm(x2, W['final_norm_g'], F32).reshape(B, T, D_MODEL)
    return y, {name: jnp.stack(arrs) for name, arrs in new.items()}


def kernel(x_prompt, x_sample, cache_nsa_cmp, cache_nsa_sel, cache_diff, state_nsa_win, state_gla, state_rwkv, state_rwkv_shift, page_table, norm1_g, norm2_g, final_norm_g, w_in, gla_wa2, gla_ba, gla_norm_g, nsa_cmp_w, rw_mu, rw_w0, rw_w2, rw_a0, rw_a2, rw_g2, rw_kk, rw_ka, rw_rk, rw_norm_g, df_lam, df_norm_g, w_br, w_out, rel_bias, ffn_w1, ffn_w3, ffn_w2, moe_router, moe_w1, moe_w3, moe_w2):
    W = dict(norm1_g=norm1_g, norm2_g=norm2_g, final_norm_g=final_norm_g, w_in=w_in, gla_wa2=gla_wa2,
             gla_ba=gla_ba, gla_norm_g=gla_norm_g, nsa_cmp_w=nsa_cmp_w, rw_mu=rw_mu, rw_w0=rw_w0, rw_w2=rw_w2,
             rw_a0=rw_a0, rw_a2=rw_a2, rw_g2=rw_g2, rw_kk=rw_kk, rw_ka=rw_ka, rw_rk=rw_rk, rw_norm_g=rw_norm_g,
             df_lam=df_lam, df_norm_g=df_norm_g, w_br=w_br, w_out=w_out, rel_bias=rel_bias, ffn_w1=ffn_w1,
             ffn_w3=ffn_w3, ffn_w2=ffn_w2, moe_router=moe_router, moe_w1=moe_w1, moe_w3=moe_w3, moe_w2=moe_w2)
    rows_last = lambda a: jnp.moveaxis(a, 2, -1)
    cache = dict(cmp_t=rows_last(cache_nsa_cmp)[:, :, :, None], sel_t=rows_last(cache_nsa_sel)[:, :, :, None],
                 diff_t=rows_last(cache_diff), win_t=rows_last(state_nsa_win)[:, :, :, None], win=state_nsa_win,
                 gla=state_gla, rwkv=state_rwkv, shift=state_rwkv_shift)
    Wc = prepare_weights(W)
    y_prompt, sp = trunk(x_prompt, W, Wc, None, None)
    y_sample, ss = trunk(x_sample, W, Wc, cache, page_table)
    return (y_prompt, y_sample,
            sp['cmp'], sp['sel'], sp['diff'], sp['win'], sp['gla'], sp['rwkv'], sp['shift'],
            ss['cmp'], ss['sel'], ss['diff'], ss['win'], ss['gla'], ss['rwkv'], ss['shift'])
```

```python
import functools
import math

import numpy as np
import jax
import jax.numpy as jnp
from jax import lax
from jax.experimental import pallas as pl
from jax.experimental.pallas import tpu as pltpu

F32 = jnp.float32
BF16 = jnp.bfloat16
I32 = jnp.int32

D_MODEL = 1024
DEPTH = 2
PAST_LEN = 16384
PAGE_SIZE = 128
N_BRANCH = 4
BR_WIDTH = 256
GLA_H, GLA_DK, GLA_DV, GLA_RANK = 4, 32, 64, 16
GLA_TAU = 16.0
GLA_CHUNK = 64
NSA_H, NSA_DH = 4, 64
CMP_BLOCK, CMP_STRIDE, SEL_BLOCK, SEL_TOPK, WINDOW = 32, 16, 64, 16, 512
FORCE_SCORE = 1.0e4
RW_H, RW_N = 4, 64
RW_PROJ = 1024
DF_H, DF_D, DF_DV = 4, 32, 64
REL_BUCKETS, REL_MAX_DIST = 32, 128
N_EXPERTS, TOP_K = 8, 2
Q_BLOCK = 128
EPS = 1e-6
NEG = -1e30

IN_WIDTHS = (128, 128, 256, 16, 256, 256, 384, 12, 256, 256, 256, RW_PROJ, N_BRANCH * D_MODEL)
IN_OFFS = tuple(int(s) for s in np.cumsum((0,) + IN_WIDTHS))
MAIN_COLS = IN_OFFS[12]
MAIN_PAD = 3328

VMEM_LIMIT_BYTES = 56 * 1024 * 1024
CMP_CHUNK_ROWS = 256
FLASH_COLS = 512
NSA_FLASH_TILE = 512
DIFF_FLASH_TILE = 512
RW_LOOKAHEAD = 2


def _cparams(*sem):
    return pltpu.CompilerParams(dimension_semantics=sem, vmem_limit_bytes=VMEM_LIMIT_BYTES)


def _dot(a, b):
    return jnp.dot(a, b, preferred_element_type=F32)


def _dot_nt(a, b):
    return lax.dot_general(a, b, (((1,), (1,)), ((), ())), preferred_element_type=F32)


def _dot_tn(a, b):
    return lax.dot_general(a, b, (((0,), (0,)), ((), ())), preferred_element_type=F32)


def _split2(x):
    hi = x.astype(BF16)
    lo = (x - hi.astype(F32)).astype(BF16)
    return hi, lo


def _split3(x):
    hi = x.astype(BF16)
    r = x - hi.astype(F32)
    mid = r.astype(BF16)
    lo = (r - mid.astype(F32)).astype(BF16)
    return hi, mid, lo


def _dot_x2(x, e):
    hi, lo = _split2(x)
    return _dot(hi, e) + _dot(lo, e)


def _dot_e3(e, x):
    hi, mid, lo = _split3(x)
    return _dot(e, hi) + _dot(e, mid) + _dot(e, lo)


def _dot_3x(a, b, dot=_dot):
    ah, al = _split2(a)
    bh, bl = _split2(b)
    return dot(ah, bh) + dot(al, bh) + dot(ah, bl)


def _iota(shape, dim):
    return lax.broadcasted_iota(I32, shape, dim)


def _block_ones(n, seg):
    r = _iota((n, n), 0) // seg
    c = _iota((n, n), 1) // seg
    return (r == c).astype(BF16)


def _rms_rows(x, g):
    ms = jnp.mean(x * x, axis=-1, keepdims=True)
    return x * lax.rsqrt(ms + EPS) * g


def _log_sigmoid(x):
    return -(jnp.maximum(-x, 0.0) + jnp.log1p(jnp.exp(-jnp.abs(x))))


def _pick_tile(n, pref):
    t = min(n, pref)
    while n % t:
        t //= 2
    return t


def _rmsnorm_kernel(x_ref, g_ref, o_ref):
    o_ref[...] = _rms_rows(x_ref[...], g_ref[...]).astype(o_ref.dtype)


def rmsnorm(x, g, out_dtype):
    m, d = x.shape
    tm = _pick_tile(m, 512)
    return pl.pallas_call(
        _rmsnorm_kernel,
        grid=(m // tm,),
        in_specs=[pl.BlockSpec((tm, d), lambda i: (i, 0)), pl.BlockSpec((1, d), lambda i: (0, 0))],
        out_specs=pl.BlockSpec((tm, d), lambda i: (i, 0)),
        out_shape=jax.ShapeDtypeStruct((m, d), out_dtype),
        compiler_params=_cparams("parallel"),
        name="rmsnorm",
    )(x, g.reshape(1, d))


def _mm_kernel(a_ref, b_ref, o_ref):
    o_ref[...] = _dot(a_ref[...], b_ref[...])


def matmul(a, b, tn):
    m, k = a.shape
    n = b.shape[1]
    tm = _pick_tile(m, 512)
    return pl.pallas_call(
        _mm_kernel,
        grid=(m // tm, n // tn),
        in_specs=[pl.BlockSpec((tm, k), lambda i, j: (i, 0)), pl.BlockSpec((k, tn), lambda i, j: (0, j))],
        out_specs=pl.BlockSpec((tm, tn), lambda i, j: (i, j)),
        out_shape=jax.ShapeDtypeStruct((m, n), F32),
        compiler_params=_cparams("parallel", "arbitrary"),
        name="in_proj",
    )(a, b)


def _merge_kernel(h_ref, a_ref, b_ref, c_ref, d_ref, x_ref, wg_ref, wbr_ref, wout_ref, o_ref):
    h = h_ref[...]
    acc = None
    for n, br_ref in enumerate((a_ref, b_ref, c_ref, d_ref)):
        gate = jax.nn.sigmoid(_dot(h, wg_ref[n]))
        up = _dot(br_ref[...].astype(BF16), wbr_ref[n])
        acc = gate * up if acc is None else acc + gate * up
    o_ref[...] = x_ref[...] + _dot(acc.astype(BF16), wout_ref[...])


def merge(h, brs, x, wg, wbr, wout):
    m, d = x.shape
    tm = _pick_tile(m, 256)
    row = lambda i: (i, 0)
    return pl.pallas_call(
        _merge_kernel,
        grid=(m // tm,),
        in_specs=[pl.BlockSpec((tm, d), row)] + [pl.BlockSpec((tm, BR_WIDTH), row)] * 4 + [
            pl.BlockSpec((tm, d), row),
            pl.BlockSpec((N_BRANCH, d, d), lambda i: (0, 0, 0)),
            pl.BlockSpec((N_BRANCH, BR_WIDTH, d), lambda i: (0, 0, 0)),
            pl.BlockSpec((d, d), lambda i: (0, 0)),
        ],
        out_specs=pl.BlockSpec((tm, d), row),
        out_shape=jax.ShapeDtypeStruct((m, d), F32),
        compiler_params=_cparams("parallel"),
        name="merge",
    )(h, *brs, x, wg, wbr, wout)


def _ffn_kernel(x_ref, g_ref, w1_ref, w3_ref, w2_ref, o_ref, h_sc, acc_sc):
    j = pl.program_id(1)

    @pl.when(j == 0)
    def _():
        h_sc[...] = _rms_rows(x_ref[...], g_ref[...]).astype(BF16)
        acc_sc[...] = jnp.zeros_like(acc_sc)

    h = h_sc[...]
    a = _dot(h, w1_ref[...])
    b = _dot(h, w3_ref[...])
    t = (a * jax.nn.sigmoid(a)) * b
    acc_sc[...] += _dot(t.astype(BF16), w2_ref[...])

    @pl.when(j == pl.num_programs(1) - 1)
    def _():
        o_ref[...] = x_ref[...] + acc_sc[...]


def ffn(x, g, w1, w3, w2):
    m, d = x.shape
    ff = w1.shape[1]
    tm = _pick_tile(m, 512)
    tf = 256
    return pl.pallas_call(
        _ffn_kernel,
        grid=(m // tm, ff // tf),
        in_specs=[
            pl.BlockSpec((tm, d), lambda i, j: (i, 0)),
            pl.BlockSpec((1, d), lambda i, j: (0, 0)),
            pl.BlockSpec((d, tf), lambda i, j: (0, j)),
            pl.BlockSpec((d, tf), lambda i, j: (0, j)),
            pl.BlockSpec((tf, d), lambda i, j: (j, 0)),
        ],
        out_specs=pl.BlockSpec((tm, d), lambda i, j: (i, 0)),
        out_shape=jax.ShapeDtypeStruct((m, d), F32),
        scratch_shapes=[pltpu.VMEM((tm, d), BF16), pltpu.VMEM((tm, d), F32)],
        compiler_params=_cparams("parallel", "arbitrary"),
        name="ffn",
    )(x, g.reshape(1, d), w1, w3, w2)


def _moe_kernel(x_ref, g_ref, wr_ref, w1_ref, w3_ref, w2_ref, o_ref, h_sc, acc_sc, comb_sc):
    e = pl.program_id(1)
    j = pl.program_id(2)
    first = (e == 0) & (j == 0)
    last = (e == pl.num_programs(1) - 1) & (j == pl.num_programs(2) - 1)

    @pl.when(first)
    def _():
        hf = _rms_rows(x_ref[...], g_ref[...])
        h_sc[...] = hf.astype(BF16)
        acc_sc[...] = jnp.zeros_like(acc_sc)
        logits = _dot_3x(hf, wr_ref[...])
        lane = _iota(logits.shape, 1)
        lg = jnp.where(lane < N_EXPERTS, logits, -jnp.inf)
        m1 = jnp.max(lg, axis=-1, keepdims=True)
        i1 = jnp.min(jnp.where(lg == m1, lane, 128), axis=-1, keepdims=True)
        lg2 = jnp.where(lane == i1, -jnp.inf, lg)
        m2 = jnp.max(lg2, axis=-1, keepdims=True)
        i2 = jnp.min(jnp.where(lg2 == m2, lane, 128), axis=-1, keepdims=True)
        e2 = jnp.exp(m2 - m1)
        den = 1.0 + e2
        comb_sc[...] = jnp.where(lane == i1, 1.0 / den, 0.0) + jnp.where(lane == i2, e2 / den, 0.0)

    h = h_sc[...]
    a = _dot(h, w1_ref[0])
    b = _dot(h, w3_ref[0])
    t = (a * jax.nn.sigmoid(a)) * b
    comb = comb_sc[...]
    c = jnp.sum(jnp.where(_iota(comb.shape, 1) == e, comb, 0.0), axis=-1, keepdims=True)
    acc_sc[...] += c * _dot(t.astype(BF16), w2_ref[0])

    @pl.when(last)
    def _():
        o_ref[...] = x_ref[...] + acc_sc[...]


def moe(x, g, wr_pad, w1, w3, w2):
    m, d = x.shape
    ne, _, ff = w1.shape
    tm = _pick_tile(m, 512)
    tf = 256
    return pl.pallas_call(
        _moe_kernel,
        grid=(m // tm, ne, ff // tf),
        in_specs=[
            pl.BlockSpec((tm, d), lambda i, e, j: (i, 0)),
            pl.BlockSpec((1, d), lambda i, e, j: (0, 0)),
            pl.BlockSpec((d, 128), lambda i, e, j: (0, 0)),
            pl.BlockSpec((1, d, tf), lambda i, e, j: (e, 0, j)),
            pl.BlockSpec((1, d, tf), lambda i, e, j: (e, 0, j)),
            pl.BlockSpec((1, tf, d), lambda i, e, j: (e, j, 0)),
        ],
        out_specs=pl.BlockSpec((tm, d), lambda i, e, j: (i, 0)),
        out_shape=jax.ShapeDtypeStruct((m, d), F32),
        scratch_shapes=[pltpu.VMEM((tm, d), BF16), pltpu.VMEM((tm, d), F32), pltpu.VMEM((tm, 128), F32)],
        compiler_params=_cparams("parallel", "arbitrary", "arbitrary"),
        name="moe",
    )(x, g.reshape(1, d), wr_pad, w1, w3, w2)


def _gla_kernel(q_ref, k_ref, v_ref, a_ref, r_ref, wa2_ref, wa2t_ref, ba_ref, bacol_ref, gn_ref, s0_ref,
                o_ref, sout_ref, s_sc, b_sc, k_sc, v_sc):
    nb, c, _ = q_ref.shape
    ci = pl.program_id(1)

    @pl.when(ci == 0)
    def _():
        s_sc[...] = s0_ref[...]

    hk = GLA_H * GLA_DK
    hv = GLA_H * GLA_DV
    tri = (_iota((c, c), 0) >= _iota((c, c), 1)).astype(BF16)
    expand = (_iota((hk, hv), 0) // GLA_DK == _iota((hk, hv), 1) // GLA_DV)
    expand_bf = expand.astype(BF16)
    ones_v = _block_ones(hv, GLA_DV)
    wa2 = wa2_ref[...].astype(BF16)
    wa2t = wa2t_ref[...].astype(BF16)
    rowi = _iota((nb, c, hk), 1)

    qs, bs, os1 = [], [], []
    for n in range(nb):
        a_in = a_ref[n].astype(BF16)
        g = _log_sigmoid(_dot(a_in, wa2) + ba_ref[...]) / GLA_TAU
        b = _dot_e3(tri, g)
        gt = _log_sigmoid(_dot_nt(wa2t, a_in) + bacol_ref[...]) / GLA_TAU
        bl_col = jnp.sum(gt, axis=1, keepdims=True)
        q = q_ref[n] * (GLA_DK ** -0.5)
        k = k_ref[n]
        v = v_ref[n]
        s_old = s_sc[n]
        os1.append(_dot_3x(q * jnp.exp(b), s_old))
        kd = k * jnp.exp(b[c - 1:c, :] - b)
        upd = _dot_3x(kd, v, dot=_dot_tn)
        s_sc[n] = s_old * jnp.exp(bl_col) + jnp.where(expand, upd, 0.0)
        qs.append(q)
        b_sc[n] = b
        k_sc[n] = k
        v_sc[n] = v
        bs.append(b)
    q3 = jnp.stack(qs)
    b3 = jnp.stack(bs)

    def body(s, o2):
        b_s = b_sc[:, pl.ds(s, 1), :]
        k_s = k_sc[:, pl.ds(s, 1), :]
        v_s = v_sc[:, pl.ds(s, 1), :]
        dec = jnp.exp(jnp.where(rowi >= s, b3 - b_s, -jnp.inf))
        contrib = (q3 * k_s * dec).reshape(nb * c, hk)
        att = _dot_x2(contrib, expand_bf).reshape(nb, c, hv)
        return o2 + att * v_s

    o2 = lax.fori_loop(0, c, body, jnp.zeros((nb, c, hv), F32), unroll=4)
    for n in range(nb):
        o = os1[n] + o2[n]
        ms = _dot_x2(o * o, ones_v) * (1.0 / GLA_DV)
        o = o * lax.rsqrt(ms + EPS) * gn_ref[...]
        r = r_ref[n]
        o_ref[n] = o * (r * jax.nn.sigmoid(r))

    @pl.when(ci == pl.num_programs(1) - 1)
    def _():
        sout_ref[...] = s_sc[...]


def gla_mixer(q, k, v, a_in, r, wa2, ba, gn, s0_bd, nb):
    bsz, t, hk = q.shape
    hv = v.shape[-1]
    c = GLA_CHUNK if t % GLA_CHUNK == 0 else t
    tok = lambda w: pl.BlockSpec((nb, c, w), lambda b, i: (b, i, 0))
    full = lambda s: pl.BlockSpec(s, lambda b, i: (0,) * len(s))
    st = pl.BlockSpec((nb, hk, hv), lambda b, i: (b, 0, 0))
    return pl.pallas_call(
        _gla_kernel,
        grid=(bsz // nb, t // c),
        in_specs=[tok(hk), tok(hk), tok(hv), tok(GLA_RANK), tok(hv),
                  full((GLA_RANK, hk)), full((hk, GLA_RANK)), full((1, hk)), full((hk, 1)), full((1, hv)), st],
        out_specs=[tok(hv), st],
        out_shape=[jax.ShapeDtypeStruct((bsz, t, hv), F32), jax.ShapeDtypeStruct((bsz, hk, hv), F32)],
        scratch_shapes=[pltpu.VMEM((nb, hk, hv), F32), pltpu.VMEM((nb, c, hk), F32),
                        pltpu.VMEM((nb, c, hk), F32), pltpu.VMEM((nb, c, hv), F32)],
        compiler_params=_cparams("parallel", "arbitrary"),
        name="gla",
    )(q, k, v, a_in, r, wa2, wa2.T, ba.reshape(1, hk), ba.reshape(hk, 1), jnp.tile(gn, GLA_H).reshape(1, hv), s0_bd)


def gla_state_to_bd(s):
    b = s.shape[0]
    eye = jnp.eye(GLA_H, dtype=s.dtype)
    return jnp.einsum('bhkv,hg->bhkgv', s, eye).reshape(b, GLA_H * GLA_DK, GLA_H * GLA_DV)


def gla_state_from_bd(sbd):
    b = sbd.shape[0]
    s5 = sbd.reshape(b, GLA_H, GLA_DK, GLA_H, GLA_DV)
    return jnp.stack([s5[:, h, :, h, :] for h in range(GLA_H)], axis=1)


def _rwkv_prep_kernel(p_ref, first_ref, mu_ref, w0_ref, w2_ref, a0_ref, a2_ref, g2_ref, kk_ref, ka_ref, rk_ref,
                      r_o, w_o, k_o, v_o, kk_o, kka_o, g_o, bonus_o):
    p = p_ref[0]
    prev = jnp.where(_iota(p.shape, 0) == 0, first_ref[0, 0], pltpu.roll(p, 1, 0))
    xm = p + (prev - p) * mu_ref[...]
    n = BR_WIDTH
    r, k, v = xm[:, 0:n], xm[:, n:2 * n], xm[:, 2 * n:3 * n]
    xwa = xm[:, 3 * n:3 * n + 128]
    xg = xm[:, 3 * n + 128:]
    decay = jnp.exp(-math.exp(-0.5) * jax.nn.sigmoid(w0_ref[...] + _dot(jnp.tanh(xwa).astype(BF16), w2_ref[...])))
    a = jax.nn.sigmoid(a0_ref[...] + _dot(xwa.astype(BF16), a2_ref[...]))
    g = _dot(jax.nn.sigmoid(xg).astype(BF16), g2_ref[...])
    ones = _block_ones(n, RW_N)
    kk = k * kk_ref[...]
    kk = kk * lax.rsqrt(jnp.maximum(_dot_x2(kk * kk, ones), 1e-12))
    k2 = k * (1.0 + (a - 1.0) * ka_ref[...])
    r_o[0] = r
    w_o[0] = decay
    k_o[0] = k2
    v_o[0] = v
    kk_o[0] = kk
    kka_o[0] = kk * a
    g_o[0] = g
    bonus_o[0] = _dot_x2(r * k2 * rk_ref[...], ones) * v


def rwkv_prep(p, prev, mu, w0, w2, a0, a2, g2, k_k, k_a, r_k):
    bsz, t, d = p.shape
    tm = _pick_tile(t, 512)
    nt = t // tm
    first = jnp.concatenate([prev[:, None, :], p[:, tm - 1:t - 1:tm, :]], axis=1).reshape(bsz, nt, 1, d)
    n = BR_WIDTH
    w2p = jnp.concatenate([w2, jnp.zeros_like(w2)], axis=0).astype(BF16)
    a2p = jnp.concatenate([jnp.zeros_like(a2), a2], axis=0).astype(BF16)
    row = lambda v: v.reshape(1, -1)
    full = lambda s: pl.BlockSpec(s, lambda b, i: (0,) * len(s))
    tok = pl.BlockSpec((1, tm, n), lambda b, i: (b, i, 0))
    return pl.pallas_call(
        _rwkv_prep_kernel,
        grid=(bsz, nt),
        in_specs=[pl.BlockSpec((1, tm, d), lambda b, i: (b, i, 0)),
                  pl.BlockSpec((1, 1, 1, d), lambda b, i: (b, i, 0, 0)),
                  full((1, d)), full((1, n)), full((128, n)), full((1, n)), full((128, n)), full((128, n)),
                  full((1, n)), full((1, n)), full((1, n))],
        out_specs=[tok] * 8,
        out_shape=[jax.ShapeDtypeStruct((bsz, t, n), F32)] * 8,
        compiler_params=_cparams("parallel", "parallel"),
        name="rwkv_prep",
    )(p, first, row(mu), row(w0), w2p, row(a0), a2p, g2.astype(BF16), row(k_k), row(k_a), row(r_k))


def _rwkv_scan_kernel(r_ref, w_ref, k_ref, v_ref, kk_ref, kka_ref, s0_ref, o_ref, sout_ref, s_sc):
    _, tc, nb, n = r_ref.shape
    ti = pl.program_id(1)

    @pl.when(ti == 0)
    def _():
        s_sc[...] = s0_ref[0]

    ones = _block_ones(n, RW_N)
    diag = (_iota((RW_N, n), 0) == (_iota((RW_N, n), 1) % RW_N)).astype(F32)

    nj = RW_LOOKAHEAD

    def seg_many(xs):
        y = _dot_x2(jnp.concatenate([x.reshape(nb * RW_N, n) for x in xs], axis=0), ones)
        return [y[j * nb * RW_N:(j + 1) * nb * RW_N].reshape(nb, RW_N, n) for j in range(len(xs))]

    def body(bi, s):
        t0 = bi * nj
        row = lambda ref, j: ref[0, t0 + j]
        w, kk, kka = [row(w_ref, j) for j in range(nj)], [row(kk_ref, j) for j in range(nj)], [row(kka_ref, j) for j in range(nj)]
        k, r, v = [row(k_ref, j) for j in range(nj)], [row(r_ref, j) for j in range(nj)], [row(v_ref, j) for j in range(nj)]
        lift = lambda x: x[:, None, :]
        decay = [jnp.ones_like(w[0])]
        for j in range(1, nj):
            decay.append(decay[-1] * w[j - 1])
        seg_rows = lambda x: _dot_x2(x, ones)
        c, d = {}, {}
        for j in range(1, nj):
            between = jnp.ones_like(w[0])
            for i in range(j - 1, -1, -1):
                c[i, j] = lift(seg_rows(kka[i] * between * kk[j]))
                d[i, j] = lift(seg_rows(k[i] * between * kk[j]))
                between = between * w[i]
        u = seg_many([s * lift(decay[j] * kk[j]) for j in range(nj)])
        vcol = seg_many([lift(v[j]) * diag for j in range(nj)])
        sa, states = [], []
        for j in range(nj):
            x = u[j]
            for i in range(j):
                x = x - sa[i] * c[i, j] + vcol[i] * d[i, j]
            sa.append(x)
            s = s * lift(w[j]) - x * lift(kka[j]) + vcol[j] * lift(k[j])
            states.append(s)
        ocol = seg_many([states[j] * lift(r[j]) for j in range(nj)])
        for j in range(nj):
            o_ref[0, t0 + j] = jnp.sum(ocol[j] * diag, axis=1)
        return s

    s = lax.fori_loop(0, tc // nj, body, s_sc[...], unroll=2)
    s_sc[...] = s

    @pl.when(ti == pl.num_programs(1) - 1)
    def _():
        sout_ref[0] = s


def rwkv_scan(r, w, k, v, kk, kka, s0, nb):
    bsz, t, n = r.shape
    bg = bsz // nb
    tc = _pick_tile(t, 256)
    tm = lambda x: x.reshape(bg, nb, t, n).transpose(0, 2, 1, 3)
    tok = pl.BlockSpec((1, tc, nb, n), lambda b, i: (b, i, 0, 0))
    st = pl.BlockSpec((1, nb, RW_N, n), lambda b, i: (b, 0, 0, 0))
    o, s = pl.pallas_call(
        _rwkv_scan_kernel,
        grid=(bg, t // tc),
        in_specs=[tok] * 6 + [st],
        out_specs=[tok, st],
        out_shape=[jax.ShapeDtypeStruct((bg, t, nb, n), F32), jax.ShapeDtypeStruct((bg, nb, RW_N, n), F32)],
        scratch_shapes=[pltpu.VMEM((nb, RW_N, n), F32)],
        compiler_params=_cparams("parallel", "arbitrary"),
        name="rwkv_scan",
    )(tm(r), tm(w), tm(k), tm(v), tm(kk), tm(kka), s0.reshape(bg, nb, RW_N, n))
    return o.transpose(0, 2, 1, 3).reshape(bsz, t, n), s.reshape(bsz, RW_N, n)


def _rwkv_post_kernel(o_ref, bonus_ref, g_ref, gn_ref, out_ref):
    o = o_ref[...]
    ms = _dot_x2(o * o, _block_ones(BR_WIDTH, RW_N)) * (1.0 / RW_N)
    out_ref[...] = (o * lax.rsqrt(ms + EPS) * gn_ref[...] + bonus_ref[...]) * g_ref[...]


def rwkv_post(o, bonus, g, gn):
    m, n = o.shape
    tm = _pick_tile(m, 1024)
    row = pl.BlockSpec((tm, n), lambda i: (i, 0))
    return pl.pallas_call(
        _rwkv_post_kernel,
        grid=(m // tm,),
        in_specs=[row, row, row, pl.BlockSpec((1, n), lambda i: (0, 0))],
        out_specs=row,
        out_shape=jax.ShapeDtypeStruct((m, n), F32),
        compiler_params=_cparams("parallel"),
        name="rwkv_post",
    )(o, bonus, g, jnp.tile(gn, RW_H).reshape(1, n))


def rwkv_mixer(p, prev, s0, mu, w0, w2, a0, a2, g2, k_k, k_a, r_k, gn, nb):
    bsz, t, _ = p.shape
    r, w, k, v, kk, kka, g, bonus = rwkv_prep(p, prev, mu, w0, w2, a0, a2, g2, k_k, k_a, r_k.reshape(-1))
    s0l = s0.transpose(0, 2, 1, 3).reshape(bsz, RW_N, BR_WIDTH)
    o, s = rwkv_scan(r, w, k, v, kk, kka, s0l, nb)
    out = rwkv_post(o.reshape(bsz * t, BR_WIDTH), bonus.reshape(bsz * t, BR_WIDTH), g.reshape(bsz * t, BR_WIDTH), gn)
    s_new = s.reshape(bsz, RW_N, RW_H, RW_N).transpose(0, 2, 1, 3)
    return out.reshape(bsz, t, BR_WIDTH), s_new, p[:, -1]


def _rel_bucket(dist):
    n = jnp.maximum(dist, 0)
    exact = REL_BUCKETS // 2
    nf = jnp.maximum(n, 1).astype(F32)
    large = exact + (jnp.log(nf / exact) / math.log(REL_MAX_DIST / exact) * (REL_BUCKETS - exact)).astype(I32)
    return jnp.where(n < exact, n, jnp.minimum(large, REL_BUCKETS - 1))


def _bucket_bits(bucket):
    return [((bucket >> i) & 1) == 1 for i in range(REL_BUCKETS.bit_length() - 1)]


def _bias_from_bits(bits, tab_ref, head):
    level = [tab_ref[b, head] for b in range(REL_BUCKETS)]
    for bit in bits:
        level = [jnp.where(bit, level[2 * i + 1], level[2 * i]) for i in range(len(level) // 2)]
    return level[0]


def _bias_from_bucket(bucket, tab_ref, head):
    return _bias_from_bits(_bucket_bits(bucket), tab_ref, head)


def _t5_tiles_kernel(tab_ref, o_ref, *, t, head0, window):
    h = pl.program_id(0)
    d = pl.program_id(1)
    dist = d * t + _iota((t, t), 1) - _iota((t, t), 0)
    val = _bias_from_bucket(_rel_bucket(dist), tab_ref, head0 + h)
    valid = dist >= 0
    if window:
        valid = valid & (dist <= WINDOW)
    o_ref[0, 0] = jnp.where(valid, val, NEG)


def t5_tiles(rel_bias, head0, nh, t, window):
    return pl.pallas_call(
        functools.partial(_t5_tiles_kernel, t=t, head0=head0, window=window),
        grid=(nh, 3),
        in_specs=[pl.BlockSpec(memory_space=pltpu.SMEM)],
        out_specs=pl.BlockSpec((1, 1, t, t), lambda h, d: (h, d, 0, 0)),
        out_shape=jax.ShapeDtypeStruct((nh, 3, t, t), F32),
        compiler_params=_cparams("parallel", "parallel"),
        name="t5_tiles",
    )(rel_bias)


def _pair_tables(nq, back):
    qi, kj, bt, fl = [], [], [], []
    for q in range(nq):
        lo = 0 if back is None else max(q - back, 0)
        for k in range(lo, q + 1):
            qi.append(q)
            kj.append(k)
            bt.append(min(q - k, 2))
            fl.append((1 if k == lo else 0) | (2 if k == q else 0))
    return tuple(jnp.asarray(np.asarray(a, np.int32)) for a in (qi, kj, bt, fl))


def _flash_kernel(qi_t, kj_t, bt_t, fl_t, q_ref, k_ref, v_ref, bias_ref, *rest, nrow, t, cw, use_sel, epi, lam_init):
    rest = list(rest)
    sel_ref = rest.pop(0) if use_sel else None
    m_sc, l_sc, acc_sc = rest[-3:]
    o_ref = rest[-4]
    extras = rest[:-4]
    p = pl.program_id(2)
    flags = fl_t[p]
    ncol = nrow * t

    @pl.when((flags & 1) != 0)
    def _():
        m_sc[...] = jnp.full_like(m_sc, NEG)
        l_sc[...] = jnp.zeros_like(l_sc)
        acc_sc[...] = jnp.zeros_like(acc_sc)

    k = k_ref[0, 0]
    vt = v_ref[0, 0]
    bt = bt_t[p]
    nbias = bias_ref.shape[2]
    if use_sel:
        ns = sel_ref.shape[1]
        blk = kj_t[p] * (t // SEL_BLOCK) + _iota((t, ns), 0) // SEL_BLOCK
        expand = (_iota((t, ns), 1) == blk).astype(BF16)
        chosen = _dot(expand, sel_ref[0].astype(BF16))
    m_all, l_all, acc_all = m_sc[...], l_sc[...], acc_sc[...]
    m_out, l_out, acc_out = [], [], []
    scores = [_dot(k, q_ref[0, 0, 0, :, c * cw:(c + 1) * cw]) for c in range(ncol // cw)]
    for c in range(ncol // cw):
        r, off = divmod(c * cw, t)
        cols = slice(c * cw, (c + 1) * cw)
        s = scores[c] + bias_ref[0, bt, r % nbias, :, off:off + cw]
        if use_sel:
            s = jnp.where(chosen[:, off:off + cw] > 0.5, s, NEG)
        m_prev = m_all[:, cols]
        m_new = jnp.maximum(m_prev, jnp.max(s, axis=0, keepdims=True))
        alpha = jnp.exp(m_prev - m_new)
        pr = jnp.exp(s - m_new)
        if use_sel:
            pr = jnp.where(s > 0.5 * NEG, pr, 0.0)
        l_out.append(alpha * l_all[:, cols] + jnp.sum(pr, axis=0, keepdims=True))
        acc_out.append(alpha * acc_all[:, cols] + _dot(vt, pr.astype(BF16)))
        m_out.append(m_new)
    m_sc[...] = jnp.concatenate(m_out, axis=1)
    l_sc[...] = jnp.concatenate(l_out, axis=1)
    acc_sc[...] = jnp.concatenate(acc_out, axis=1)

    @pl.when((flags & 2) != 0)
    def _():
        o = acc_sc[...] / l_sc[...]
        if epi == "plain":
            o_ref[0, 0, 0] = o
        elif epi == "diff":
            lam_ref, gn_ref = extras
            lv = lam_ref[...]
            lam = (jnp.exp(jnp.sum(lv[0:1] * lv[1:2], keepdims=True)) - jnp.exp(jnp.sum(lv[2:3] * lv[3:4], keepdims=True))
                   + lam_init)
            od = o[:, :t] - lam * o[:, t:]
            ms = jnp.mean(od * od, axis=0, keepdims=True)
            o_ref[0, 0, 0] = od * lax.rsqrt(ms + EPS) * gn_ref[...] * (1.0 - lam_init)
        else:
            gate_ref, oc_ref, os_ref = extras
            g = jax.nn.sigmoid(gate_ref[0, 0, 0])
            o_ref[0, 0, 0] = g[0:1] * oc_ref[0, 0, 0] + g[1:2] * os_ref[0, 0, 0] + g[2:3] * o


def flash(qt, k, vt, bias, tables, *, t, sel=None, epi="plain", extras=(), extra_specs=(), lam_init=None):
    bsz, hg, nq, dh, ncol = qt.shape
    nrow = ncol // t
    npairs = tables[0].shape[0]
    in_specs = [
        pl.BlockSpec((1, 1, 1, dh, ncol), lambda b, h, p, qi, kj, bt, fl: (b, h, qi[p], 0, 0)),
        pl.BlockSpec((1, 1, t, dh), lambda b, h, p, qi, kj, bt, fl: (b, h, kj[p], 0)),
        pl.BlockSpec((1, 1, dh, t), lambda b, h, p, qi, kj, bt, fl: (b, h, 0, kj[p])),
        pl.BlockSpec((1,) + bias.shape[1:], lambda b, h, p, qi, kj, bt, fl: (h, 0, 0, 0, 0)),
    ]
    args = [qt, k, vt, bias]
    if sel is not None:
        in_specs.append(pl.BlockSpec((1, sel.shape[1], t), lambda b, h, p, qi, kj, bt, fl: (b, 0, qi[p])))
        args.append(sel)
    in_specs += list(extra_specs)
    args += list(extras)
    n_out = t if epi == "diff" else ncol
    return pl.pallas_call(
        functools.partial(_flash_kernel, nrow=nrow, t=t, cw=min(t, FLASH_COLS), use_sel=sel is not None, epi=epi,
                          lam_init=lam_init),
        grid_spec=pltpu.PrefetchScalarGridSpec(
            num_scalar_prefetch=4,
            grid=(bsz, hg, npairs),
            in_specs=in_specs,
            out_specs=pl.BlockSpec((1, 1, 1, dh, n_out), lambda b, h, p, qi, kj, bt, fl: (b, h, qi[p], 0, 0)),
            scratch_shapes=[pltpu.VMEM((1, ncol), F32), pltpu.VMEM((1, ncol), F32), pltpu.VMEM((dh, ncol), F32)],
        ),
        out_shape=jax.ShapeDtypeStruct((bsz, hg, nq, dh, n_out), F32),
        compiler_params=_cparams("parallel", "parallel", "arbitrary"),
        name="flash_" + epi + ("_sel" if sel is not None else ""),
    )(*tables, *args)


def _to_tiles(x, t):
    b, g, r, tl, d = x.shape
    return x.reshape(b, g, r, tl // t, t, d).transpose(0, 1, 3, 5, 2, 4).reshape(b, g, tl // t, d, r * t)


def _from_tiles(x, r):
    b, g, nq, d, rt = x.shape
    t = rt // r
    return x.reshape(b, g, nq, d, r, t).transpose(0, 1, 4, 2, 5, 3).reshape(b, g, r, nq * t, d)


def _compress_kernel(pt_ref, *refs, npp):
    del pt_ref
    wt_ref = refs[npp]
    a_ref, b_ref = refs[npp + 1:]
    wt = wt_ref[...]
    for i in range(npp):
        x3 = refs[i][0].reshape(PAGE_SIZE // CMP_STRIDE, CMP_STRIDE, 2 * NSA_DH)
        a_ref[0, i * 8:(i + 1) * 8, :] = jnp.sum(x3 * wt[None, 0:CMP_STRIDE], axis=1)
        b_ref[0, i * 8:(i + 1) * 8, :] = jnp.sum(x3 * wt[None, CMP_STRIDE:], axis=1)


def compress(pool, pt, wt, npp):
    bsz, n_pages = pt.shape
    g = PAGE_SIZE // CMP_STRIDE
    page = lambda i: pl.BlockSpec((1, PAGE_SIZE, 2 * NSA_DH), lambda b, j, pt_ref: (pt_ref[b, j * npp + i], 0, 0))
    out = pl.BlockSpec((1, npp * g, 2 * NSA_DH), lambda b, j, pt_ref: (b, j, 0))
    shape = jax.ShapeDtypeStruct((bsz, n_pages * g, 2 * NSA_DH), F32)
    return pl.pallas_call(
        functools.partial(_compress_kernel, npp=npp),
        grid_spec=pltpu.PrefetchScalarGridSpec(
            num_scalar_prefetch=1,
            grid=(bsz, n_pages // npp),
            in_specs=[page(i) for i in range(npp)] + [pl.BlockSpec((CMP_BLOCK, 2 * NSA_DH), lambda b, j, pt_ref: (0, 0))],
            out_specs=[out, out],
        ),
        out_shape=[shape, shape],
        compiler_params=_cparams("parallel", "arbitrary"),
        name="nsa_compress",
    )(pt, *([pool] * npp), wt)


def _nsa_cmp_kernel(tab_ref, q_ref, a_ref, b_ref, tail_ref, o_ref, sel_ref, *, t, qpos0, n_cmp, n_sel):
    nc = a_ref.shape[1]
    ns = sel_ref.shape[-1]
    qi = pl.program_id(1)
    rown = _iota((nc, 2 * NSA_DH), 0)
    bsh = jnp.where(rown == nc - 1, tail_ref[0], pltpu.roll(b_ref[0], nc - 1, 0))
    kcv = jnp.where(rown < n_cmp, a_ref[0] + bsh, 0.0)
    vc = kcv[:, NSA_DH:].astype(BF16)
    kc_hi, kc_lo = _split2(kcv[:, :NSA_DH])
    start = _iota((nc, ns), 0) * CMP_STRIDE
    sblk = _iota((nc, ns), 1) * SEL_BLOCK
    ov = ((start < sblk + SEL_BLOCK) & (start + CMP_BLOCK > sblk)).astype(BF16)
    rc = max(8, min(t, CMP_CHUNK_ROWS))
    scores = []
    for c in range(t // rc):
        qpos = qpos0 + qi * t + c * rc + _iota((rc, nc), 0)
        n = _iota((rc, nc), 1)
        dist = qpos - (n * CMP_STRIDE + CMP_BLOCK - 1)
        valid = (dist >= 0) & (n < n_cmp)
        bits = _bucket_bits(_rel_bucket(dist))
        psum = jnp.zeros((rc, nc), F32)
        for h in range(NSA_H):
            q_hi, q_lo = _split2(q_ref[0, 0, h, c * rc:(c + 1) * rc, :])
            s = _dot_nt(q_hi, kc_hi) + _dot_nt(q_lo, kc_hi) + _dot_nt(q_hi, kc_lo)
            sh = jnp.where(valid, s + _bias_from_bits(bits, tab_ref, h), NEG)
            m = jnp.max(sh, axis=-1, keepdims=True)
            p = jnp.where(valid, jnp.exp(sh - m), 0.0)
            p = p / jnp.maximum(jnp.sum(p, axis=-1, keepdims=True), 1e-30)
            o_ref[0, 0, h, c * rc:(c + 1) * rc, :] = _dot(p.astype(BF16), vc)
            psum = psum + p
        hi, mid, lo = _split3(psum)
        scores.append(_dot(hi, ov) + _dot(mid, ov) + _dot(lo, ov))
    score = scores[0] if len(scores) == 1 else jnp.concatenate(scores, axis=0)
    j = _iota((t, ns), 1)
    cur = (qpos0 + qi * t + _iota((t, ns), 0)) // SEL_BLOCK
    forced = (j == 0) | (j == cur) | (j == cur - 1)
    sc = jnp.where(j <= cur, score + jnp.where(forced, FORCE_SCORE, 0.0), -1.0)
    sc = jnp.where(j < n_sel, sc, -jnp.inf)
    chosen = jnp.zeros((t, ns), F32)
    for _ in range(min(SEL_TOPK, n_sel)):
        m = jnp.max(sc, axis=-1, keepdims=True)
        idx = jnp.min(jnp.where(sc == m, j, ns), axis=-1, keepdims=True)
        hit = j == idx
        chosen = jnp.where(hit, 1.0, chosen)
        sc = jnp.where(hit, -jnp.inf, sc)
    sel_ref[0] = chosen


def nsa_cmp(q4, a, b, tail, rel_bias_nsa, *, t, qpos0, n_cmp, n_sel):
    bsz, _, _, tq, dh = q4.shape
    nc = a.shape[1]
    ns = -(-n_sel // 128) * 128
    return pl.pallas_call(
        functools.partial(_nsa_cmp_kernel, t=t, qpos0=qpos0, n_cmp=n_cmp, n_sel=n_sel),
        grid=(bsz, tq // t),
        in_specs=[
            pl.BlockSpec(memory_space=pltpu.SMEM),
            pl.BlockSpec((1, 1, NSA_H, t, dh), lambda b, i: (b, 0, 0, i, 0)),
            pl.BlockSpec((1, nc, 2 * dh), lambda b, i: (b, 0, 0)),
            pl.BlockSpec((1, nc, 2 * dh), lambda b, i: (b, 0, 0)),
            pl.BlockSpec((1, 1, 2 * dh), lambda b, i: (b, 0, 0)),
        ],
        out_specs=[pl.BlockSpec((1, 1, NSA_H, t, dh), lambda b, i: (b, 0, 0, i, 0)),
                   pl.BlockSpec((1, t, ns), lambda b, i: (b, i, 0))],
        out_shape=[jax.ShapeDtypeStruct((bsz, 1, NSA_H, tq, dh), F32), jax.ShapeDtypeStruct((bsz, tq, ns), F32)],
        compiler_params=_cparams("parallel", "parallel"),
        name="nsa_cmp",
    )(rel_bias_nsa, q4, a, b, tail)


def _cmp_weight_tile(w_cmp):
    return jnp.repeat(w_cmp.T, NSA_DH, axis=1)


def nsa_prompt(qn, kvn, n_g, w_cmp, Wc):
    bsz, t_len, _ = qn.shape
    t = _pick_tile(t_len, 256)
    q4f = (qn.reshape(bsz, t_len, NSA_H, NSA_DH) * NSA_DH ** -0.5).transpose(0, 2, 1, 3)[:, None]
    q4 = q4f.astype(BF16)
    n_pages = t_len // PAGE_SIZE
    pool = kvn[:, :, 0:2 * NSA_DH].reshape(bsz * n_pages, PAGE_SIZE, 2 * NSA_DH)
    pt = jnp.arange(bsz * n_pages, dtype=I32).reshape(bsz, n_pages)
    a, b = compress(pool, pt, _cmp_weight_tile(w_cmp), _pick_tile(n_pages, 16))
    o_c, chosen = nsa_cmp(q4f, a, b, jnp.zeros((bsz, 1, 2 * NSA_DH), F32), Wc['rel_nsa'], t=t, qpos0=0,
                          n_cmp=t_len // CMP_STRIDE - 1, n_sel=t_len // SEL_BLOCK)
    t = _pick_tile(t_len, NSA_FLASH_TILE)
    kv = lambda i: kvn[:, :, i * NSA_DH:(i + 1) * NSA_DH].astype(BF16)[:, None]
    kvt = lambda i: kv(i).transpose(0, 1, 3, 2)
    nq = t_len // t
    qt = _to_tiles(q4, t)
    o_s = flash(qt, kv(2), kvt(3), Wc['tiles_nsa'], _pair_tables(nq, None), t=t, sel=chosen.transpose(0, 2, 1))
    gates = n_g.reshape(bsz, nq, t, NSA_H, 3).transpose(0, 1, 4, 3, 2).reshape(bsz, 1, nq, 3, NSA_H * t)
    tile = lambda w: pl.BlockSpec((1, 1, 1, w, NSA_H * t), lambda b, h, p, qi, kj, bt, fl: (b, h, qi[p], 0, 0))
    o = flash(qt, kv(4), kvt(5), Wc['tiles_win'], _pair_tables(nq, WINDOW // t), t=t, epi="win",
              extras=(gates, _to_tiles(o_c, t), o_s), extra_specs=(tile(3), tile(NSA_DH), tile(NSA_DH)))
    return _from_tiles(o, NSA_H)[:, 0].transpose(0, 2, 1, 3).reshape(bsz, t_len, NSA_H * NSA_DH)


def _lam_init(l):
    return 0.8 - 0.6 * math.exp(-0.3 * l)


def diff_prompt(d_q, d_k, d_v, lam_rows, lam_init, gn, Wc):
    bsz, t_len, _ = d_q.shape
    t = _pick_tile(t_len, DIFF_FLASH_TILE)
    q = d_q.reshape(bsz, t_len, DF_H, 2 * DF_D).transpose(0, 2, 1, 3) * DF_D ** -0.5
    lane = jnp.arange(2 * DF_D) < DF_D
    q2 = jnp.stack([jnp.where(lane, q, 0.0), jnp.where(lane, 0.0, q)], axis=2).astype(BF16)
    k = d_k.reshape(bsz, t_len, DF_H, 2 * DF_D).transpose(0, 2, 1, 3).astype(BF16)
    vt = d_v.reshape(bsz, t_len, DF_H, DF_DV).transpose(0, 2, 3, 1).astype(BF16)
    full = lambda s: pl.BlockSpec(s, lambda b, h, p, qi, kj, bt, fl: (0,) * len(s))
    o = flash(_to_tiles(q2, t), k, vt, Wc['tiles_diff'], _pair_tables(t_len // t, None), t=t, epi="diff",
              lam_init=lam_init, extras=(lam_rows, gn.reshape(DF_DV, 1)),
              extra_specs=(full((4, DF_D)), full((DF_DV, 1))))
    return _from_tiles(o, 1)[:, :, 0].transpose(0, 2, 1, 3).reshape(bsz, t_len, BR_WIDTH)


NEW_PAD = 16


def _paged_attn_kernel(pt_ref, tab_ref, q_ref, *refs, npp, head_cols, kpos0, qpos0, t_new, window, use_sel):
    del pt_ref
    pages = refs[:npp]
    newk_ref, newv_ref = refs[npp], refs[npp + 1]
    sel_ref = refs[npp + 2] if use_sel else None
    o_ref, m_sc, l_sc, acc_sc = refs[-4:]
    j = pl.program_id(1)
    tq = t_new
    ng = len(head_cols)
    nrow = ng * tq
    ks = npp * PAGE_SIZE
    hkv, rh = q_ref.shape[1], q_ref.shape[2]
    hrows = lambda x, h: x[h * rh:(h + 1) * rh]

    @pl.when(j == 0)
    def _():
        m_sc[...] = jnp.full_like(m_sc, NEG)
        l_sc[...] = jnp.zeros_like(l_sc)
        acc_sc[...] = jnp.zeros_like(acc_sc)

    qs = [q_ref[0, h] for h in range(hkv)]

    def update(s, kpos, extra_valid, blocks, pv):
        n = s.shape[-1]
        dist = (qpos0 + _iota((tq, n), 0)) - kpos
        valid = dist >= 0
        if window:
            valid = valid & (dist <= WINDOW)
        if extra_valid is not None:
            valid = valid & extra_valid
        if use_sel:
            nsb = sel_ref.shape[-1]
            expand = (_iota((nsb, n), 0) == blocks).astype(BF16)
            valid = valid & (_dot(sel_ref[0].astype(BF16), expand) > 0.5)
        bits = _bucket_bits(_rel_bucket(dist))
        bias = {c: _bias_from_bits(bits, tab_ref, c) for c in sorted(set(head_cols))}
        s3 = s.reshape(ng, tq, n) + jnp.stack([bias[c] for c in head_cols])
        s = jnp.where(valid[None], s3, NEG).reshape(nrow, n)
        m_prev = m_sc[...]
        m_new = jnp.maximum(m_prev, jnp.max(s, axis=-1, keepdims=True))
        alpha = jnp.exp(m_prev - m_new)
        pr = jnp.where(s > 0.5 * NEG, jnp.exp(s - m_new), 0.0)
        l_sc[...] = alpha * l_sc[...] + jnp.sum(pr, axis=-1, keepdims=True)
        acc_sc[...] = alpha * acc_sc[...] + pv(pr.astype(BF16))
        m_sc[...] = m_new

    cat = lambda xs, axis: xs[0] if len(xs) == 1 else jnp.concatenate(xs, axis=axis)
    kt = [[pages[i][0, 0, 0, h].astype(BF16) for h in range(hkv)] for i in range(npp)]
    vt = [[pages[i][0, 0, 1, h].astype(BF16) for h in range(hkv)] for i in range(npp)]
    s = cat([cat([_dot(qs[h], kt[i][h]) for i in range(npp)], 1) for h in range(hkv)], 0)
    kpos = kpos0 + j * ks + _iota((tq, ks), 1)
    blocks = (kpos0 + j * ks + _iota((1, ks), 1)) // SEL_BLOCK

    def pv_pages(pb):
        outs = []
        for h in range(hkv):
            ph = hrows(pb, h)
            out = _dot_nt(ph[:, 0:PAGE_SIZE], vt[0][h])
            for i in range(1, npp):
                out = out + _dot_nt(ph[:, i * PAGE_SIZE:(i + 1) * PAGE_SIZE], vt[i][h])
            outs.append(out)
        return cat(outs, 0)

    update(s, kpos, None, blocks, pv_pages)

    @pl.when(j == pl.num_programs(1) - 1)
    def _():
        nk = [newk_ref[0, h].astype(BF16) for h in range(hkv)]
        nv = [newv_ref[0, h].astype(BF16) for h in range(hkv)]
        col = _iota((tq, NEW_PAD), 1)
        update(cat([_dot_nt(qs[h], nk[h]) for h in range(hkv)], 0), qpos0 + col, col < t_new,
               (qpos0 + _iota((1, NEW_PAD), 1)) // SEL_BLOCK,
               lambda pb: cat([_dot(hrows(pb, h), nv[h]) for h in range(hkv)], 0))
        o_ref[0] = acc_sc[...] / l_sc[...]


def paged_attn(q, pool, layer, pt, page_index, newk, newv, tab, *, npp, head_cols, kpos0, qpos0, window=False,
               sel=None):
    bsz, hkv, rh, dh = q.shape
    nrow = hkv * rh
    lw = dh
    n_pages = pt.shape[1]
    t_new = nrow // len(head_cols)
    page = lambda i: pl.BlockSpec((1, 1, 2, hkv, dh, PAGE_SIZE),
                                  lambda b, j, pt_ref: (layer,) + page_index(b, j * npp + i, pt_ref))
    new_spec = pl.BlockSpec((1, hkv, NEW_PAD, dh), lambda b, j, pt_ref: (b, 0, 0, 0))
    in_specs = [pl.BlockSpec(memory_space=pltpu.SMEM), pl.BlockSpec((1, hkv, rh, dh), lambda b, j, pt_ref: (b, 0, 0, 0))]
    in_specs += [page(i) for i in range(npp)]
    in_specs += [new_spec, new_spec]
    args = [tab, q] + [pool] * npp + [newk, newv]
    if sel is not None:
        in_specs.append(pl.BlockSpec((1, t_new, sel.shape[-1]), lambda b, j, pt_ref: (b, 0, 0)))
        args.append(sel)
    return pl.pallas_call(
        functools.partial(_paged_attn_kernel, npp=npp, head_cols=tuple(head_cols), kpos0=kpos0, qpos0=qpos0,
                          t_new=t_new, window=window, use_sel=sel is not None),
        grid_spec=pltpu.PrefetchScalarGridSpec(
            num_scalar_prefetch=1,
            grid=(bsz, n_pages // npp),
            in_specs=in_specs,
            out_specs=pl.BlockSpec((1, nrow, lw), lambda b, j, pt_ref: (b, 0, 0)),
            scratch_shapes=[pltpu.VMEM((nrow, 1), F32), pltpu.VMEM((nrow, 1), F32), pltpu.VMEM((nrow, lw), F32)],
        ),
        out_shape=jax.ShapeDtypeStruct((bsz, nrow, lw), F32),
        compiler_params=_cparams("parallel", "arbitrary"),
        name="paged_attn",
    )(pt, *args)


def _nsa_combine_kernel(g_ref, oc_ref, os_ref, ow_ref, o_ref):
    g = jax.nn.sigmoid(g_ref[...])
    o_ref[...] = g[..., 0:1] * oc_ref[...] + g[..., 1:2] * os_ref[...] + g[..., 2:3] * ow_ref[...]


def nsa_combine(gates, o_c, o_s, o_w):
    n, dh = o_c.shape
    full = lambda w: pl.BlockSpec((n, w), lambda i: (0, 0))
    return pl.pallas_call(
        _nsa_combine_kernel, grid=(1,),
        in_specs=[full(3), full(dh), full(dh), full(dh)], out_specs=full(dh),
        out_shape=jax.ShapeDtypeStruct((n, dh), F32), name="nsa_combine",
    )(gates, o_c, o_s, o_w)


def _pad_rows(x, n):
    return jnp.pad(x, ((0, 0), (0, n - x.shape[1]), (0, 0)))


def _compress_t_kernel(pt_ref, *refs, npp):
    del pt_ref
    pages = refs[:npp]
    whi_ref, wlo_ref = refs[npp], refs[npp + 1]
    a_ref, b_ref = refs[npp + 2:]
    for kv in range(2):
        acc = None
        for i in range(npp):
            xh, xl = _split2(pages[i][0, 0, kv, 0])
            y = _dot(xh, whi_ref[kv, i]) + _dot(xl, whi_ref[kv, i]) + _dot(xh, wlo_ref[kv, i])
            acc = y if acc is None else acc + y
        a_ref[0, kv] = acc[:, :128]
        b_ref[0, kv] = acc[:, 128:]


def compress_t(pool, layer, pt, w_cmp):
    npp = 16
    bsz, n_pages = pt.shape
    g = PAGE_SIZE // CMP_STRIDE
    r = np.arange(PAGE_SIZE)
    grp = jnp.asarray((r[:, None] // CMP_STRIDE == np.arange(g)[None, :]).astype(np.float32))
    slot = jnp.eye(npp, dtype=F32)
    halves = []
    for half in range(2):
        wr = w_cmp[:, half * CMP_STRIDE + r % CMP_STRIDE]
        halves.append(jnp.einsum('kr,rg,ip->kirpg', wr, grp, slot).reshape(2, npp, PAGE_SIZE, npp * g))
    wbig = jnp.concatenate(halves, axis=-1)
    whi = wbig.astype(BF16)
    wlo = (wbig - whi.astype(F32)).astype(BF16)
    dh = pool.shape[-2]
    page = lambda i: pl.BlockSpec((1, 1, 2, 1, dh, PAGE_SIZE),
                                  lambda b, j, pt_ref: (layer, pt_ref[b, j * npp + i], 0, 0, 0, 0))
    wspec = pl.BlockSpec((2, npp, PAGE_SIZE, 2 * npp * g), lambda b, j, pt_ref: (0, 0, 0, 0))
    out = pl.BlockSpec((1, 2, dh, npp * g), lambda b, j, pt_ref: (b, 0, 0, j))
    shape = jax.ShapeDtypeStruct((bsz, 2, dh, n_pages * g), F32)
    return pl.pallas_call(
        functools.partial(_compress_t_kernel, npp=npp),
        grid_spec=pltpu.PrefetchScalarGridSpec(
            num_scalar_prefetch=1,
            grid=(bsz, n_pages // npp),
            in_specs=[page(i) for i in range(npp)] + [wspec, wspec],
            out_specs=[out, out],
        ),
        out_shape=[shape, shape],
        compiler_params=_cparams("parallel", "arbitrary"),
        name="nsa_compress_t",
    )(pt, *([pool] * npp), whi, wlo)


def _paged_index(b, page, pt_ref):
    return (pt_ref[b, page], 0, 0, 0, 0)


def _window_index(b, page, pt_ref):
    return (b, 0, 0, 0, page)


def nsa_sample(qn, kvn, n_g, w_cmp, Wc, layer, pool_cmp, pool_sel, page_table, win_t):
    bsz, t_len, _ = qn.shape
    lw = 2 * NSA_DH
    total = PAST_LEN + t_len
    n_grp = -(-total // CMP_STRIDE)
    n_cmp = n_grp - CMP_BLOCK // CMP_STRIDE + 1
    n_sel = -(-total // SEL_BLOCK)
    qf = (qn.reshape(bsz, t_len, NSA_H, NSA_DH) * NSA_DH ** -0.5).transpose(0, 2, 1, 3)
    q1 = qf.astype(BF16).reshape(bsz, 1, NSA_H * t_len, NSA_DH)
    at, bt = compress_t(pool_cmp, layer, page_table, w_cmp)
    rows = lambda x: x.transpose(0, 3, 1, 2).reshape(bsz, x.shape[-1], lw)
    new_page = _pad_rows(kvn[:, :, 0:lw], PAGE_SIZE)
    _, b_new = compress(new_page, jnp.arange(bsz, dtype=I32).reshape(bsz, 1), _cmp_weight_tile(w_cmp), 1)
    o_c, chosen = nsa_cmp(qf[:, None], rows(at), rows(bt), b_new[:, 0:1], Wc['rel_nsa'], t=t_len, qpos0=PAST_LEN,
                          n_cmp=n_cmp, n_sel=n_sel)
    new = lambda i: _pad_rows(kvn[:, :, i * NSA_DH:(i + 1) * NSA_DH], NEW_PAD)[:, None]
    heads = tuple(range(NSA_H))
    o_s = paged_attn(q1, pool_sel, layer, page_table, _paged_index, new(2), new(3), Wc['rel_nsa'], npp=16,
                     head_cols=heads, kpos0=0, qpos0=PAST_LEN, sel=chosen)
    wb = win_t.shape[-1]
    wpages = wb // PAGE_SIZE
    o_w = paged_attn(q1, win_t, layer, jnp.zeros((bsz, wpages), I32), _window_index, new(4), new(5), Wc['rel_nsa'],
                     npp=wpages, head_cols=heads, kpos0=PAST_LEN - wb, qpos0=PAST_LEN, window=True)
    n = bsz * NSA_H * t_len
    gates = n_g.reshape(bsz, t_len, NSA_H, 3).transpose(0, 2, 1, 3).reshape(n, 3)
    o = nsa_combine(gates, o_c.reshape(n, NSA_DH), o_s.reshape(n, NSA_DH), o_w.reshape(n, NSA_DH))
    return o.reshape(bsz, NSA_H, t_len, NSA_DH).transpose(0, 2, 1, 3).reshape(bsz, t_len, NSA_H * NSA_DH)


def _diff_post_kernel(o_ref, lam_ref, gn_ref, out_ref, *, lam_init):
    lv = lam_ref[...]
    lam = (jnp.exp(jnp.sum(lv[0:1] * lv[1:2], keepdims=True)) - jnp.exp(jnp.sum(lv[2:3] * lv[3:4], keepdims=True))
           + lam_init)
    od = o_ref[0] - lam * o_ref[1]
    out_ref[...] = _rms_rows(od, gn_ref[...]) * (1.0 - lam_init)


def diff_post(o2, lam_rows, lam_init, gn):
    _, n, dv = o2.shape
    return pl.pallas_call(
        functools.partial(_diff_post_kernel, lam_init=lam_init), grid=(1,),
        in_specs=[pl.BlockSpec((2, n, dv), lambda i: (0, 0, 0)), pl.BlockSpec((4, DF_D), lambda i: (0, 0)),
                  pl.BlockSpec((1, dv), lambda i: (0, 0))],
        out_specs=pl.BlockSpec((n, dv), lambda i: (0, 0)),
        out_shape=jax.ShapeDtypeStruct((n, dv), F32), name="diff_post",
    )(o2, lam_rows, gn.reshape(1, dv))


def diff_sample(d_q, d_k, d_v, lam_rows, lam_init, gn, Wc, layer, pool, page_table):
    bsz, t_len, _ = d_q.shape
    q = (d_q.reshape(bsz, t_len, DF_H, 2 * DF_D) * DF_D ** -0.5).transpose(0, 2, 1, 3)
    lane = jnp.arange(2 * DF_D) < DF_D
    q2 = jnp.stack([jnp.where(lane, q, 0.0), jnp.where(lane, 0.0, q)], axis=2).astype(BF16)
    q2 = q2.reshape(bsz, DF_H, 2 * t_len, 2 * DF_D)
    new = lambda x: _pad_rows(x, NEW_PAD).reshape(bsz, NEW_PAD, DF_H, DF_DV).transpose(0, 2, 1, 3)
    head_cols = tuple(NSA_H + h for h in range(DF_H) for _ in range(2))
    o = paged_attn(q2, pool, layer, page_table, _paged_index, new(d_k), new(d_v), Wc['rel_all'], npp=16,
                   head_cols=head_cols, kpos0=0, qpos0=PAST_LEN)
    n = bsz * DF_H * t_len
    o2 = o.reshape(bsz, DF_H, 2, t_len, DF_DV).transpose(2, 0, 1, 3, 4).reshape(2, n, DF_DV)
    od = diff_post(o2, lam_rows, lam_init, gn).reshape(bsz, DF_H, t_len, DF_DV)
    return od.transpose(0, 2, 1, 3).reshape(bsz, t_len, BR_WIDTH)


def _jx_masked_softmax(s, mask):
    s = jnp.where(mask, s.astype(F32), -1e30)
    m = jnp.max(s, axis=-1, keepdims=True)
    p = jnp.where(mask, jnp.exp(s - m), 0.0)
    return p / jnp.maximum(jnp.sum(p, axis=-1, keepdims=True), 1e-30)


def _jx_t5_bias(dist, table):
    return jnp.moveaxis(table[_rel_bucket(dist)].astype(F32), -1, -2)


def _jx_nsa_compress(k, w):
    B, T, D = k.shape
    sub = k.reshape(B, T // CMP_STRIDE, CMP_STRIDE, D)
    r = CMP_BLOCK // CMP_STRIDE
    n = T // CMP_STRIDE - r + 1
    out = jnp.einsum('bnsd,s->bnd', sub[:, 0:n], w[0:CMP_STRIDE])
    for j in range(1, r):
        out = out + jnp.einsum('bnsd,s->bnd', sub[:, j:j + n], w[j * CMP_STRIDE:(j + 1) * CMP_STRIDE])
    ends = jnp.arange(n) * CMP_STRIDE + CMP_BLOCK - 1
    return out, ends


def _jx_nsa_sample(q, kv, gates, w_cmp, btab, pool_cmp, pool_sel, page_table, win_buf):
    B, T = q.shape[:2]
    qpos = PAST_LEN + jnp.arange(T)
    total = PAST_LEN + T
    past_cmp = pool_cmp[page_table].reshape(B, PAST_LEN, 2, NSA_DH)
    seq_cmp = jnp.concatenate([past_cmp, kv[:, :, 0:2].astype(past_cmp.dtype)], axis=1)
    seq_cmp = jnp.pad(seq_cmp, ((0, 0), (0, (-total) % CMP_STRIDE), (0, 0), (0, 0)))
    kc, cend = _jx_nsa_compress(seq_cmp[:, :, 0], w_cmp[0])
    vc, _ = _jx_nsa_compress(seq_cmp[:, :, 1], w_cmp[1])
    n_sel = -(-total // SEL_BLOCK)
    start = jnp.arange(kc.shape[1])[:, None] * CMP_STRIDE
    selb = jnp.arange(n_sel)[None, :] * SEL_BLOCK
    ov = ((start < selb + SEL_BLOCK) & (start + CMP_BLOCK > selb)).astype(F32)
    dist = qpos[:, None] - cend[None, :]
    s = jnp.einsum('bqhd,bnd->bqhn', q, kc) * NSA_DH ** -0.5 + _jx_t5_bias(dist, btab)[None]
    p = _jx_masked_softmax(s, (dist >= 0)[None, :, None, :])
    o_c = jnp.einsum('bqhn,bnd->bqhd', p, vc)
    score = jnp.einsum('bqhn,ns->bqs', p, ov)
    cur = (qpos // SEL_BLOCK)[:, None]
    j = jnp.arange(n_sel)[None, :]
    forced = (j == 0) | (j == cur) | (j == cur - 1)
    sc = jnp.where(j <= cur, score + jnp.where(forced, FORCE_SCORE, 0.0), -1.0)
    idx = lax.top_k(sc, min(SEL_TOPK, n_sel))[1]
    nb_past = PAST_LEN // SEL_BLOCK
    nb_new = n_sel - nb_past
    bpp = PAGE_SIZE // SEL_BLOCK
    bidx = jnp.arange(B)[:, None, None]
    jp = jnp.minimum(idx, nb_past - 1)
    phys = page_table[bidx, jp // bpp]
    rows = (jp % bpp)[..., None] * SEL_BLOCK + jnp.arange(SEL_BLOCK)
    g_past = pool_sel[phys[..., None], rows]
    new_sel = jnp.pad(kv[:, :, 2:4], ((0, 0), (0, nb_new * SEL_BLOCK - T), (0, 0), (0, 0)))
    new_sel = new_sel.reshape(B, nb_new, SEL_BLOCK, 2, NSA_DH)
    g_new = new_sel[bidx, jnp.clip(idx - nb_past, 0, nb_new - 1)]
    g = jnp.where((idx < nb_past)[..., None, None, None], g_past, g_new.astype(g_past.dtype))
    Bq, Q, Kk = idx.shape
    kpos = (idx[..., None] * SEL_BLOCK + jnp.arange(SEL_BLOCK)).reshape(Bq, Q, Kk * SEL_BLOCK)
    dist_s = qpos[None, :, None] - kpos
    ks = g[..., 0, :].reshape(Bq, Q, Kk * SEL_BLOCK, NSA_DH)
    vs = g[..., 1, :].reshape(Bq, Q, Kk * SEL_BLOCK, NSA_DH)
    s2 = jnp.einsum('bqhd,bqld->bqhl', q, ks) * NSA_DH ** -0.5 + _jx_t5_bias(dist_s, btab)
    p2 = _jx_masked_softmax(s2, (dist_s >= 0)[:, :, None, :])
    o_s = jnp.einsum('bqhl,bqld->bqhd', p2, vs)
    wb = win_buf.shape[1]
    wseq = jnp.concatenate([win_buf, kv[:, :, 4:6].astype(win_buf.dtype)], axis=1)
    kposw = PAST_LEN - wb + jnp.arange(wb + T)
    dist_w = qpos[:, None] - kposw[None, :]
    mask = (dist_w >= 0) & (dist_w <= WINDOW) & (kposw >= 0)[None, :]
    s3 = jnp.einsum('bqhd,bld->bqhl', q, wseq[:, :, 0]) * NSA_DH ** -0.5 + _jx_t5_bias(dist_w, btab)[None]
    p3 = _jx_masked_softmax(s3, mask[None, :, None, :])
    o_w = jnp.einsum('bqhl,bld->bqhd', p3, wseq[:, :, 1])
    o = gates[..., 0:1] * o_c + gates[..., 1:2] * o_s + gates[..., 2:3] * o_w
    return o.reshape(B, T, NSA_H * NSA_DH), wseq[:, T:]


def _jx_diff_sample(q, k, v, lam, btab, pool, page_table):
    B, T = q.shape[:2]
    past = pool[page_table].reshape(B, PAST_LEN, 2, DF_H, DF_DV)
    segs = ((past[:, :, 0].reshape(B, PAST_LEN, DF_H, 2, DF_D), past[:, :, 1], jnp.arange(PAST_LEN)),
            (k, v, PAST_LEN + jnp.arange(T)))
    qpos = PAST_LEN + jnp.arange(T)
    scores, masks = [], []
    for kk, vv, kpos in segs:
        dist = qpos[:, None] - kpos[None, :]
        s = jnp.einsum('bqhcd,blhcd->bqhcl', q, kk) * DF_D ** -0.5
        scores.append(s + _jx_t5_bias(dist, btab)[None, :, :, None, :])
        masks.append(dist >= 0)
    p = _jx_masked_softmax(jnp.concatenate(scores, axis=-1), jnp.concatenate(masks, axis=-1)[None, :, None, None, :])
    w = p[..., 0, :] - lam * p[..., 1, :]
    o, off = 0.0, 0
    for kk, vv, kpos in segs:
        n = kpos.shape[0]
        o = o + jnp.einsum('bqhl,blhv->bqhv', w[..., off:off + n], vv)
        off += n
    return o


def prepare_weights(W):
    Wc = {}
    w_in = W['w_in']
    Wc['w_main'] = jnp.pad(w_in[:, :, :MAIN_COLS], ((0, 0), (0, 0), (0, MAIN_PAD - MAIN_COLS))).astype(BF16)
    Wc['w_gate'] = w_in[:, :, MAIN_COLS:].reshape(DEPTH, D_MODEL, N_BRANCH, D_MODEL).transpose(0, 2, 1, 3).astype(BF16)
    for name in ('w_br', 'w_out', 'ffn_w1', 'ffn_w3', 'ffn_w2', 'moe_w1', 'moe_w3', 'moe_w2'):
        Wc[name] = W[name].astype(BF16)
    Wc['moe_router'] = jnp.pad(W['moe_router'], ((0, 0), (0, 0), (0, 128 - N_EXPERTS)))
    rel = W['rel_bias']
    Wc['rel_nsa'] = rel[:, :NSA_H]
    Wc['rel_all'] = rel
    Wc['tiles_nsa'] = t5_tiles(rel, 0, NSA_H, NSA_FLASH_TILE, False).transpose(1, 0, 2, 3)[None]
    Wc['tiles_win'] = t5_tiles(rel, 0, NSA_H, NSA_FLASH_TILE, True).transpose(1, 0, 2, 3)[None]
    Wc['tiles_diff'] = t5_tiles(rel, NSA_H, DF_H, DIFF_FLASH_TILE, False)[:, :, None]
    return Wc


def token_mix(h, l, W, Wc, past, page_table):
    B, T, _ = h.shape
    m = B * T
    nb_state = 2 if past is None else 8
    proj = matmul(h.reshape(m, D_MODEL), Wc['w_main'][l], MAIN_PAD // 2).reshape(B, T, MAIN_PAD)
    sl = lambda i: proj[:, :, IN_OFFS[i]:IN_OFFS[i + 1]]
    g_q, g_k, g_v, g_a, g_r, n_q, n_kv, n_g, d_q, d_k, d_v, rw = [sl(i) for i in range(12)]
    st = {}
    s0 = jnp.zeros((B, GLA_H * GLA_DK, GLA_H * GLA_DV), F32) if past is None else gla_state_to_bd(past['gla'][l])
    o_a, s_bd = gla_mixer(g_q, g_k, g_v, g_a, g_r, W['gla_wa2'][l], W['gla_ba'][l], W['gla_norm_g'][l], s0, nb_state)
    st['gla'] = gla_state_from_bd(s_bd)
    kvn = n_kv.reshape(B, T, 6, NSA_DH)
    if past is None:
        o_b = nsa_prompt(n_q, n_kv, n_g, W['nsa_cmp_w'][l], Wc)
        st['win'] = kvn[:, T - min(WINDOW, T):, 4:6]
    else:
        o_b = nsa_sample(n_q, n_kv, n_g, W['nsa_cmp_w'][l], Wc, l, past['cmp_t'], past['sel_t'], page_table,
                         past['win_t'])
        st['win'] = jnp.concatenate([past['win'][l], kvn[:, :, 4:6]], axis=1)[:, T:]
    st['cmp'] = kvn[:, :, 0:2]
    st['sel'] = kvn[:, :, 2:4]
    prev = jnp.zeros((B, RW_PROJ), F32) if past is None else past['shift'][l]
    s0 = jnp.zeros((B, RW_H, RW_N, RW_N), F32) if past is None else past['rwkv'][l]
    o_c, st['rwkv'], st['shift'] = rwkv_mixer(
        rw, prev, s0, W['rw_mu'][l], W['rw_w0'][l], W['rw_w2'][l], W['rw_a0'][l], W['rw_a2'][l], W['rw_g2'][l],
        W['rw_kk'][l], W['rw_ka'][l], W['rw_rk'][l], W['rw_norm_g'][l], nb_state)
    lam_init = _lam_init(l)
    if past is None:
        o_d = diff_prompt(d_q, d_k, d_v, W['df_lam'][l], lam_init, W['df_norm_g'][l], Wc)
    else:
        o_d = diff_sample(d_q, d_k, d_v, W['df_lam'][l], lam_init, W['df_norm_g'][l], Wc, l, past['diff_t'],
                          page_table)
    st['diff'] = jnp.stack([d_k.reshape(B, T, DF_H, 2 * DF_D), d_v.reshape(B, T, DF_H, DF_DV)], axis=2)
    return [t.reshape(m, BR_WIDTH) for t in (o_a, o_b, o_c, o_d)], st


def trunk(x, W, Wc, cache, page_table):
    B, T, _ = x.shape
    x2 = x.reshape(B * T, D_MODEL)
    new = {}
    for l in range(DEPTH):
        past = cache
        h = rmsnorm(x2, W['norm1_g'][l], BF16)
        brs, st = token_mix(h.reshape(B, T, D_MODEL), l, W, Wc, past, page_table)
        x2 = merge(h, brs, x2, Wc['w_gate'][l], Wc['w_br'][l], Wc['w_out'][l])
        j = l // 2
        if l % 2 == 0:
            x2 = ffn(x2, W['norm2_g'][l], Wc['ffn_w1'][j], Wc['ffn_w3'][j], Wc['ffn_w2'][j])
        else:
            x2 = moe(x2, W['norm2_g'][l], Wc['moe_router'][j], Wc['moe_w1'][j], Wc['moe_w3'][j], Wc['moe_w2'][j])
        for name, arr in st.items():
            new.setdefault(name, []).append(arr)
    y = rmsnorm(x2, W['final_norm_g'], F32).reshape(B, T, D_MODEL)
    return y, {name: jnp.stack(arrs) for name, arrs in new.items()}


def kernel(x_prompt, x_sample, cache_nsa_cmp, cache_nsa_sel, cache_diff, state_nsa_win, state_gla, state_rwkv, state_rwkv_shift, page_table, norm1_g, norm2_g, final_norm_g, w_in, gla_wa2, gla_ba, gla_norm_g, nsa_cmp_w, rw_mu, rw_w0, rw_w2, rw_a0, rw_a2, rw_g2, rw_kk, rw_ka, rw_rk, rw_norm_g, df_lam, df_norm_g, w_br, w_out, rel_bias, ffn_w1, ffn_w3, ffn_w2, moe_router, moe_w1, moe_w3, moe_w2):
    W = dict(norm1_g=norm1_g, norm2_g=norm2_g, final_norm_g=final_norm_g, w_in=w_in, gla_wa2=gla_wa2,
             gla_ba=gla_ba, gla_norm_g=gla_norm_g, nsa_cmp_w=nsa_cmp_w, rw_mu=rw_mu, rw_w0=rw_w0, rw_w2=rw_w2,
             rw_a0=rw_a0, rw_a2=rw_a2, rw_g2=rw_g2, rw_kk=rw_kk, rw_ka=rw_ka, rw_rk=rw_rk, rw_norm_g=rw_norm_g,
             df_lam=df_lam, df_norm_g=df_norm_g, w_br=w_br, w_out=w_out, rel_bias=rel_bias, ffn_w1=ffn_w1,
             ffn_w3=ffn_w3, ffn_w2=ffn_w2, moe_router=moe_router, moe_w1=moe_w1, moe_w3=moe_w3, moe_w2=moe_w2)
    rows_last = lambda a: jnp.moveaxis(a, 2, -1)
    cache = dict(cmp_t=rows_last(cache_nsa_cmp)[:, :, :, None], sel_t=rows_last(cache_nsa_sel)[:, :, :, None],
                 diff_t=rows_last(cache_diff), win_t=rows_last(state_nsa_win)[:, :, :, None], win=state_nsa_win,
                 gla=state_gla, rwkv=state_rwkv, shift=state_rwkv_shift)
    Wc = prepare_weights(W)
    y_prompt, sp = trunk(x_prompt, W, Wc, None, None)
    y_sample, ss = trunk(x_sample, W, Wc, cache, page_table)
    return (y_prompt, y_sample,
            sp['cmp'], sp['sel'], sp['diff'], sp['win'], sp['gla'], sp['rwkv'], sp['shift'],
            ss['cmp'], ss['sel'], ss['diff'], ss['win'], ss['gla'], ss['rwkv'], ss['shift'])
```

```python
import functools
import math

import numpy as np
import jax
import jax.numpy as jnp
from jax import lax
from jax.experimental import pallas as pl
from jax.experimental.pallas import tpu as pltpu

F32 = jnp.float32
BF16 = jnp.bfloat16
I32 = jnp.int32

D_MODEL = 1024
DEPTH = 2
PAST_LEN = 16384
PAGE_SIZE = 128
N_BRANCH = 4
BR_WIDTH = 256
GLA_H, GLA_DK, GLA_DV, GLA_RANK = 4, 32, 64, 16
GLA_TAU = 16.0
GLA_CHUNK = 64
NSA_H, NSA_DH = 4, 64
CMP_BLOCK, CMP_STRIDE, SEL_BLOCK, SEL_TOPK, WINDOW = 32, 16, 64, 16, 512
FORCE_SCORE = 1.0e4
RW_H, RW_N = 4, 64
RW_PROJ = 1024
DF_H, DF_D, DF_DV = 4, 32, 64
REL_BUCKETS, REL_MAX_DIST = 32, 128
N_EXPERTS, TOP_K = 8, 2
Q_BLOCK = 128
EPS = 1e-6
NEG = -1e30

IN_WIDTHS = (128, 128, 256, 16, 256, 256, 384, 12, 256, 256, 256, RW_PROJ, N_BRANCH * D_MODEL)
IN_OFFS = tuple(int(s) for s in np.cumsum((0,) + IN_WIDTHS))
MAIN_COLS = IN_OFFS[12]
LANES = 128
GROUP_PADS = tuple(-(-w // LANES) * LANES for w in IN_WIDTHS[:12])
GROUP_OFFS = tuple(int(s) for s in np.cumsum((0,) + GROUP_PADS))
MAIN_PAD = GROUP_OFFS[-1]

VMEM_LIMIT_BYTES = 56 * 1024 * 1024
CMP_CHUNK_ROWS = 256
FLASH_COLS = 512
NSA_FLASH_TILE = 512
DIFF_FLASH_TILE = 1024
RW_LOOKAHEAD = 2


def _cparams(*sem):
    return pltpu.CompilerParams(dimension_semantics=sem, vmem_limit_bytes=VMEM_LIMIT_BYTES)


def _dot(a, b):
    return jnp.dot(a, b, preferred_element_type=F32)


def _dot_nt(a, b):
    return lax.dot_general(a, b, (((1,), (1,)), ((), ())), preferred_element_type=F32)


def _dot_tn(a, b):
    return lax.dot_general(a, b, (((0,), (0,)), ((), ())), preferred_element_type=F32)


def _split2(x):
    hi = x.astype(BF16)
    lo = (x - hi.astype(F32)).astype(BF16)
    return hi, lo


def _split3(x):
    hi = x.astype(BF16)
    r = x - hi.astype(F32)
    mid = r.astype(BF16)
    lo = (r - mid.astype(F32)).astype(BF16)
    return hi, mid, lo


def _dot_x2(x, e):
    hi, lo = _split2(x)
    return _dot(hi, e) + _dot(lo, e)


def _dot_e3(e, x):
    hi, mid, lo = _split3(x)
    return _dot(e, hi) + _dot(e, mid) + _dot(e, lo)


def _dot_3x(a, b, dot=_dot):
    ah, al = _split2(a)
    bh, bl = _split2(b)
    return dot(ah, bh) + dot(al, bh) + dot(ah, bl)


def _iota(shape, dim):
    return lax.broadcasted_iota(I32, shape, dim)


def _block_ones(n, seg):
    r = _iota((n, n), 0) // seg
    c = _iota((n, n), 1) // seg
    return (r == c).astype(BF16)


def _rms_rows(x, g):
    ms = jnp.mean(x * x, axis=-1, keepdims=True)
    return x * lax.rsqrt(ms + EPS) * g


def _log_sigmoid(x):
    return -(jnp.maximum(-x, 0.0) + jnp.log1p(jnp.exp(-jnp.abs(x))))


def _pick_tile(n, pref):
    t = min(n, pref)
    while n % t:
        t //= 2
    return t


def _rmsnorm_kernel(x_ref, g_ref, o_ref):
    o_ref[...] = _rms_rows(x_ref[...], g_ref[...]).astype(o_ref.dtype)


def rmsnorm(x, g, out_dtype):
    m, d = x.shape
    tm = _pick_tile(m, 512)
    return pl.pallas_call(
        _rmsnorm_kernel,
        grid=(m // tm,),
        in_specs=[pl.BlockSpec((tm, d), lambda i: (i, 0)), pl.BlockSpec((1, d), lambda i: (0, 0))],
        out_specs=pl.BlockSpec((tm, d), lambda i: (i, 0)),
        out_shape=jax.ShapeDtypeStruct((m, d), out_dtype),
        compiler_params=_cparams("parallel"),
        name="rmsnorm",
    )(x, g.reshape(1, d))


def _in_proj_kernel(a_ref, b_ref, *o_refs):
    a = a_ref[...]
    for g, o_ref in enumerate(o_refs):
        o_ref[...] = _dot(a, b_ref[:, GROUP_OFFS[g]:GROUP_OFFS[g + 1]])


def in_proj(a, b):
    m, k = a.shape
    tm = _pick_tile(m, 512)
    return pl.pallas_call(
        _in_proj_kernel,
        grid=(m // tm,),
        in_specs=[pl.BlockSpec((tm, k), lambda i: (i, 0)), pl.BlockSpec((k, MAIN_PAD), lambda i: (0, 0))],
        out_specs=[pl.BlockSpec((tm, w), lambda i: (i, 0)) for w in GROUP_PADS],
        out_shape=[jax.ShapeDtypeStruct((m, w), F32) for w in GROUP_PADS],
        compiler_params=_cparams("parallel"),
        name="in_proj",
    )(a, b)


def _merge_kernel(h_ref, a_ref, b_ref, c_ref, d_ref, x_ref, wg_ref, wbr_ref, wout_ref, o_ref):
    h = h_ref[...]
    acc = None
    for n, br_ref in enumerate((a_ref, b_ref, c_ref, d_ref)):
        gate = jax.nn.sigmoid(_dot(h, wg_ref[n]))
        up = _dot(br_ref[...].astype(BF16), wbr_ref[n])
        acc = gate * up if acc is None else acc + gate * up
    o_ref[...] = x_ref[...] + _dot(acc.astype(BF16), wout_ref[...])


def merge(h, brs, x, wg, wbr, wout):
    m, d = x.shape
    tm = _pick_tile(m, 256)
    row = lambda i: (i, 0)
    return pl.pallas_call(
        _merge_kernel,
        grid=(m // tm,),
        in_specs=[pl.BlockSpec((tm, d), row)] + [pl.BlockSpec((tm, BR_WIDTH), row)] * 4 + [
            pl.BlockSpec((tm, d), row),
            pl.BlockSpec((N_BRANCH, d, d), lambda i: (0, 0, 0)),
            pl.BlockSpec((N_BRANCH, BR_WIDTH, d), lambda i: (0, 0, 0)),
            pl.BlockSpec((d, d), lambda i: (0, 0)),
        ],
        out_specs=pl.BlockSpec((tm, d), row),
        out_shape=jax.ShapeDtypeStruct((m, d), F32),
        compiler_params=_cparams("parallel"),
        name="merge",
    )(h, *brs, x, wg, wbr, wout)


def _ffn_kernel(x_ref, g_ref, w1_ref, w3_ref, w2_ref, o_ref, h_sc, acc_sc):
    j = pl.program_id(1)

    @pl.when(j == 0)
    def _():
        h_sc[...] = _rms_rows(x_ref[...], g_ref[...]).astype(BF16)
        acc_sc[...] = jnp.zeros_like(acc_sc)

    h = h_sc[...]
    a = _dot(h, w1_ref[...])
    b = _dot(h, w3_ref[...])
    t = (a * jax.nn.sigmoid(a)) * b
    acc_sc[...] += _dot(t.astype(BF16), w2_ref[...])

    @pl.when(j == pl.num_programs(1) - 1)
    def _():
        o_ref[...] = x_ref[...] + acc_sc[...]


def ffn(x, g, w1, w3, w2):
    m, d = x.shape
    ff = w1.shape[1]
    tm = _pick_tile(m, 512)
    tf = 256
    return pl.pallas_call(
        _ffn_kernel,
        grid=(m // tm, ff // tf),
        in_specs=[
            pl.BlockSpec((tm, d), lambda i, j: (i, 0)),
            pl.BlockSpec((1, d), lambda i, j: (0, 0)),
            pl.BlockSpec((d, tf), lambda i, j: (0, j)),
            pl.BlockSpec((d, tf), lambda i, j: (0, j)),
            pl.BlockSpec((tf, d), lambda i, j: (j, 0)),
        ],
        out_specs=pl.BlockSpec((tm, d), lambda i, j: (i, 0)),
        out_shape=jax.ShapeDtypeStruct((m, d), F32),
        scratch_shapes=[pltpu.VMEM((tm, d), BF16), pltpu.VMEM((tm, d), F32)],
        compiler_params=_cparams("parallel", "arbitrary"),
        name="ffn",
    )(x, g.reshape(1, d), w1, w3, w2)


def _moe_kernel(x_ref, g_ref, wr_ref, w1_ref, w3_ref, w2_ref, o_ref, h_sc, acc_sc, comb_sc):
    e = pl.program_id(1)
    j = pl.program_id(2)
    first = (e == 0) & (j == 0)
    last = (e == pl.num_programs(1) - 1) & (j == pl.num_programs(2) - 1)

    @pl.when(first)
    def _():
        hf = _rms_rows(x_ref[...], g_ref[...])
        h_sc[...] = hf.astype(BF16)
        acc_sc[...] = jnp.zeros_like(acc_sc)
        logits = _dot_3x(hf, wr_ref[...])
        lane = _iota(logits.shape, 1)
        lg = jnp.where(lane < N_EXPERTS, logits, -jnp.inf)
        m1 = jnp.max(lg, axis=-1, keepdims=True)
        i1 = jnp.min(jnp.where(lg == m1, lane, 128), axis=-1, keepdims=True)
        lg2 = jnp.where(lane == i1, -jnp.inf, lg)
        m2 = jnp.max(lg2, axis=-1, keepdims=True)
        i2 = jnp.min(jnp.where(lg2 == m2, lane, 128), axis=-1, keepdims=True)
        e2 = jnp.exp(m2 - m1)
        den = 1.0 + e2
        comb_sc[...] = jnp.where(lane == i1, 1.0 / den, 0.0) + jnp.where(lane == i2, e2 / den, 0.0)

    h = h_sc[...]
    a = _dot(h, w1_ref[0])
    b = _dot(h, w3_ref[0])
    t = (a * jax.nn.sigmoid(a)) * b
    comb = comb_sc[...]
    c = jnp.sum(jnp.where(_iota(comb.shape, 1) == e, comb, 0.0), axis=-1, keepdims=True)
    acc_sc[...] += c * _dot(t.astype(BF16), w2_ref[0])

    @pl.when(last)
    def _():
        o_ref[...] = x_ref[...] + acc_sc[...]


def moe(x, g, wr_pad, w1, w3, w2):
    m, d = x.shape
    ne, _, ff = w1.shape
    tm = _pick_tile(m, 512)
    tf = 256
    return pl.pallas_call(
        _moe_kernel,
        grid=(m // tm, ne, ff // tf),
        in_specs=[
            pl.BlockSpec((tm, d), lambda i, e, j: (i, 0)),
            pl.BlockSpec((1, d), lambda i, e, j: (0, 0)),
            pl.BlockSpec((d, 128), lambda i, e, j: (0, 0)),
            pl.BlockSpec((1, d, tf), lambda i, e, j: (e, 0, j)),
            pl.BlockSpec((1, d, tf), lambda i, e, j: (e, 0, j)),
            pl.BlockSpec((1, tf, d), lambda i, e, j: (e, j, 0)),
        ],
        out_specs=pl.BlockSpec((tm, d), lambda i, e, j: (i, 0)),
        out_shape=jax.ShapeDtypeStruct((m, d), F32),
        scratch_shapes=[pltpu.VMEM((tm, d), BF16), pltpu.VMEM((tm, d), F32), pltpu.VMEM((tm, 128), F32)],
        compiler_params=_cparams("parallel", "arbitrary", "arbitrary"),
        name="moe",
    )(x, g.reshape(1, d), wr_pad, w1, w3, w2)


def _gla_kernel(q_ref, k_ref, v_ref, a_ref, r_ref, wa2_ref, wa2t_ref, ba_ref, bacol_ref, gn_ref, s0_ref,
                o_ref, sout_ref, s_sc, b_sc, k_sc, v_sc):
    nb, c, _ = q_ref.shape
    ci = pl.program_id(1)

    @pl.when(ci == 0)
    def _():
        s_sc[...] = s0_ref[...]

    hk = GLA_H * GLA_DK
    hv = GLA_H * GLA_DV
    tri = (_iota((c, c), 0) >= _iota((c, c), 1)).astype(BF16)
    expand = (_iota((hk, hv), 0) // GLA_DK == _iota((hk, hv), 1) // GLA_DV)
    expand_bf = expand.astype(BF16)
    ones_v = _block_ones(hv, GLA_DV)
    wa2 = wa2_ref[...].astype(BF16)
    wa2t = wa2t_ref[...].astype(BF16)
    rowi = _iota((nb, c, hk), 1)

    qs, bs, os1 = [], [], []
    for n in range(nb):
        a_in = a_ref[n].astype(BF16)
        g = _log_sigmoid(_dot(a_in, wa2) + ba_ref[...]) / GLA_TAU
        b = _dot_e3(tri, g)
        gt = _log_sigmoid(_dot_nt(wa2t, a_in) + bacol_ref[...]) / GLA_TAU
        bl_col = jnp.sum(gt, axis=1, keepdims=True)
        q = q_ref[n] * (GLA_DK ** -0.5)
        k = k_ref[n]
        v = v_ref[n]
        s_old = s_sc[n]
        os1.append(_dot_3x(q * jnp.exp(b), s_old))
        kd = k * jnp.exp(b[c - 1:c, :] - b)
        upd = _dot_3x(kd, v, dot=_dot_tn)
        s_sc[n] = s_old * jnp.exp(bl_col) + jnp.where(expand, upd, 0.0)
        qs.append(q)
        b_sc[n] = b
        k_sc[n] = k
        v_sc[n] = v
        bs.append(b)
    q3 = jnp.stack(qs)
    b3 = jnp.stack(bs)

    def body(s, o2):
        b_s = b_sc[:, pl.ds(s, 1), :]
        k_s = k_sc[:, pl.ds(s, 1), :]
        v_s = v_sc[:, pl.ds(s, 1), :]
        dec = jnp.exp(jnp.where(rowi >= s, b3 - b_s, -jnp.inf))
        contrib = (q3 * k_s * dec).reshape(nb * c, hk)
        att = _dot(contrib.astype(BF16), expand_bf).reshape(nb, c, hv)
        return o2 + att * v_s

    o2 = lax.fori_loop(0, c, body, jnp.zeros((nb, c, hv), F32), unroll=4)
    for n in range(nb):
        o = os1[n] + o2[n]
        ms = _dot_x2(o * o, ones_v) * (1.0 / GLA_DV)
        o = o * lax.rsqrt(ms + EPS) * gn_ref[...]
        r = r_ref[n]
        o_ref[n] = o * (r * jax.nn.sigmoid(r))

    @pl.when(ci == pl.num_programs(1) - 1)
    def _():
        sout_ref[...] = s_sc[...]


def gla_mixer(q, k, v, a_in, r, wa2, ba, gn, s0_bd, nb):
    bsz, t, hk = q.shape
    hv = v.shape[-1]
    ra = a_in.shape[-1]
    wa2 = jnp.pad(wa2, ((0, ra - GLA_RANK), (0, 0)))
    c = GLA_CHUNK if t % GLA_CHUNK == 0 else t
    tok = lambda w: pl.BlockSpec((nb, c, w), lambda b, i: (b, i, 0))
    full = lambda s: pl.BlockSpec(s, lambda b, i: (0,) * len(s))
    st = pl.BlockSpec((nb, hk, hv), lambda b, i: (b, 0, 0))
    return pl.pallas_call(
        _gla_kernel,
        grid=(bsz // nb, t // c),
        in_specs=[tok(hk), tok(hk), tok(hv), tok(ra), tok(hv),
                  full((ra, hk)), full((hk, ra)), full((1, hk)), full((hk, 1)), full((1, hv)), st],
        out_specs=[tok(hv), st],
        out_shape=[jax.ShapeDtypeStruct((bsz, t, hv), F32), jax.ShapeDtypeStruct((bsz, hk, hv), F32)],
        scratch_shapes=[pltpu.VMEM((nb, hk, hv), F32), pltpu.VMEM((nb, c, hk), F32),
                        pltpu.VMEM((nb, c, hk), F32), pltpu.VMEM((nb, c, hv), F32)],
        compiler_params=_cparams("parallel", "arbitrary"),
        name="gla",
    )(q, k, v, a_in, r, wa2, wa2.T, ba.reshape(1, hk), ba.reshape(hk, 1), jnp.tile(gn, GLA_H).reshape(1, hv), s0_bd)


def gla_state_to_bd(s):
    b = s.shape[0]
    eye = jnp.eye(GLA_H, dtype=s.dtype)
    return jnp.einsum('bhkv,hg->bhkgv', s, eye).reshape(b, GLA_H * GLA_DK, GLA_H * GLA_DV)


def gla_state_from_bd(sbd):
    b = sbd.shape[0]
    s5 = sbd.reshape(b, GLA_H, GLA_DK, GLA_H, GLA_DV)
    return jnp.stack([s5[:, h, :, h, :] for h in range(GLA_H)], axis=1)


def _rwkv_prep_kernel(p_ref, first_ref, mu_ref, w0_ref, w2_ref, a0_ref, a2_ref, g2_ref, kk_ref, ka_ref, rk_ref,
                      r_o, w_o, k_o, v_o, kk_o, kka_o, g_o, bonus_o):
    p = p_ref[0]
    prev = jnp.where(_iota(p.shape, 0) == 0, first_ref[0, 0], pltpu.roll(p, 1, 0))
    xm = p + (prev - p) * mu_ref[...]
    n = BR_WIDTH
    r, k, v = xm[:, 0:n], xm[:, n:2 * n], xm[:, 2 * n:3 * n]
    xwa = xm[:, 3 * n:3 * n + 128]
    xg = xm[:, 3 * n + 128:]
    decay = jnp.exp(-math.exp(-0.5) * jax.nn.sigmoid(w0_ref[...] + _dot(jnp.tanh(xwa).astype(BF16), w2_ref[...])))
    a = jax.nn.sigmoid(a0_ref[...] + _dot(xwa.astype(BF16), a2_ref[...]))
    g = _dot(jax.nn.sigmoid(xg).astype(BF16), g2_ref[...])
    ones = _block_ones(n, RW_N)
    kk = k * kk_ref[...]
    kk = kk * lax.rsqrt(jnp.maximum(_dot_x2(kk * kk, ones), 1e-12))
    k2 = k * (1.0 + (a - 1.0) * ka_ref[...])
    r_o[0] = r
    w_o[0] = decay
    k_o[0] = k2
    v_o[0] = v
    kk_o[0] = kk
    kka_o[0] = kk * a
    g_o[0] = g
    bonus_o[0] = _dot_x2(r * k2 * rk_ref[...], ones) * v


def rwkv_prep(p, prev, mu, w0, w2, a0, a2, g2, k_k, k_a, r_k):
    bsz, t, d = p.shape
    tm = _pick_tile(t, 512)
    nt = t // tm
    first = jnp.concatenate([prev[:, None, :], p[:, tm - 1:t - 1:tm, :]], axis=1).reshape(bsz, nt, 1, d)
    n = BR_WIDTH
    w2p = jnp.concatenate([w2, jnp.zeros_like(w2)], axis=0).astype(BF16)
    a2p = jnp.concatenate([jnp.zeros_like(a2), a2], axis=0).astype(BF16)
    row = lambda v: v.reshape(1, -1)
    full = lambda s: pl.BlockSpec(s, lambda b, i: (0,) * len(s))
    tok = pl.BlockSpec((1, tm, n), lambda b, i: (b, i, 0))
    return pl.pallas_call(
        _rwkv_prep_kernel,
        grid=(bsz, nt),
        in_specs=[pl.BlockSpec((1, tm, d), lambda b, i: (b, i, 0)),
                  pl.BlockSpec((1, 1, 1, d), lambda b, i: (b, i, 0, 0)),
                  full((1, d)), full((1, n)), full((128, n)), full((1, n)), full((128, n)), full((128, n)),
                  full((1, n)), full((1, n)), full((1, n))],
        out_specs=[tok] * 8,
        out_shape=[jax.ShapeDtypeStruct((bsz, t, n), F32)] * 8,
        compiler_params=_cparams("parallel", "parallel"),
        name="rwkv_prep",
    )(p, first, row(mu), row(w0), w2p, row(a0), a2p, g2.astype(BF16), row(k_k), row(k_a), row(r_k))


def _rwkv_scan_kernel(r_ref, w_ref, k_ref, v_ref, kk_ref, kka_ref, s0_ref, o_ref, sout_ref, s_sc):
    _, tc, nb, n = r_ref.shape
    ti = pl.program_id(1)

    @pl.when(ti == 0)
    def _():
        s_sc[...] = s0_ref[0]

    ones = _block_ones(n, RW_N)
    diag = (_iota((RW_N, n), 0) == (_iota((RW_N, n), 1) % RW_N)).astype(F32)

    nj = RW_LOOKAHEAD

    def seg_many(xs, single_pass=False):
        x = jnp.concatenate([x.reshape(nb * RW_N, n) for x in xs], axis=0)
        y = _dot(x.astype(BF16), ones) if single_pass else _dot_x2(x, ones)
        return [y[j * nb * RW_N:(j + 1) * nb * RW_N].reshape(nb, RW_N, n) for j in range(len(xs))]

    def body(bi, s):
        t0 = bi * nj
        row = lambda ref, j: ref[0, t0 + j]
        w, kk, kka = [row(w_ref, j) for j in range(nj)], [row(kk_ref, j) for j in range(nj)], [row(kka_ref, j) for j in range(nj)]
        k, r, v = [row(k_ref, j) for j in range(nj)], [row(r_ref, j) for j in range(nj)], [row(v_ref, j) for j in range(nj)]
        lift = lambda x: x[:, None, :]
        decay = [jnp.ones_like(w[0])]
        for j in range(1, nj):
            decay.append(decay[-1] * w[j - 1])
        seg_rows = lambda x: _dot_x2(x, ones)
        c, d = {}, {}
        for j in range(1, nj):
            between = jnp.ones_like(w[0])
            for i in range(j - 1, -1, -1):
                c[i, j] = lift(seg_rows(kka[i] * between * kk[j]))
                d[i, j] = lift(seg_rows(k[i] * between * kk[j]))
                between = between * w[i]
        u = seg_many([s * lift(decay[j] * kk[j]) for j in range(nj)])
        vcol = seg_many([lift(v[j]) * diag for j in range(nj)])
        sa, states = [], []
        for j in range(nj):
            x = u[j]
            for i in range(j):
                x = x - sa[i] * c[i, j] + vcol[i] * d[i, j]
            sa.append(x)
            s = s * lift(w[j]) - x * lift(kka[j]) + vcol[j] * lift(k[j])
            states.append(s)
        ocol = seg_many([states[j] * lift(r[j]) for j in range(nj)], single_pass=True)
        for j in range(nj):
            o_ref[0, t0 + j] = jnp.sum(ocol[j] * diag, axis=1)
        return s

    s = lax.fori_loop(0, tc // nj, body, s_sc[...], unroll=2)
    s_sc[...] = s

    @pl.when(ti == pl.num_programs(1) - 1)
    def _():
        sout_ref[0] = s


def rwkv_scan(r, w, k, v, kk, kka, s0, nb):
    bsz, t, n = r.shape
    bg = bsz // nb
    tc = _pick_tile(t, 256)
    tm = lambda x: x.reshape(bg, nb, t, n).transpose(0, 2, 1, 3)
    tok = pl.BlockSpec((1, tc, nb, n), lambda b, i: (b, i, 0, 0))
    st = pl.BlockSpec((1, nb, RW_N, n), lambda b, i: (b, 0, 0, 0))
    o, s = pl.pallas_call(
        _rwkv_scan_kernel,
        grid=(bg, t // tc),
        in_specs=[tok] * 6 + [st],
        out_specs=[tok, st],
        out_shape=[jax.ShapeDtypeStruct((bg, t, nb, n), F32), jax.ShapeDtypeStruct((bg, nb, RW_N, n), F32)],
        scratch_shapes=[pltpu.VMEM((nb, RW_N, n), F32)],
        compiler_params=_cparams("parallel", "arbitrary"),
        name="rwkv_scan",
    )(tm(r), tm(w), tm(k), tm(v), tm(kk), tm(kka), s0.reshape(bg, nb, RW_N, n))
    return o.transpose(0, 2, 1, 3).reshape(bsz, t, n), s.reshape(bsz, RW_N, n)


def _rwkv_post_kernel(o_ref, bonus_ref, g_ref, gn_ref, out_ref):
    o = o_ref[...]
    ms = _dot_x2(o * o, _block_ones(BR_WIDTH, RW_N)) * (1.0 / RW_N)
    out_ref[...] = (o * lax.rsqrt(ms + EPS) * gn_ref[...] + bonus_ref[...]) * g_ref[...]


def rwkv_post(o, bonus, g, gn):
    m, n = o.shape
    tm = _pick_tile(m, 1024)
    row = pl.BlockSpec((tm, n), lambda i: (i, 0))
    return pl.pallas_call(
        _rwkv_post_kernel,
        grid=(m // tm,),
        in_specs=[row, row, row, pl.BlockSpec((1, n), lambda i: (0, 0))],
        out_specs=row,
        out_shape=jax.ShapeDtypeStruct((m, n), F32),
        compiler_params=_cparams("parallel"),
        name="rwkv_post",
    )(o, bonus, g, jnp.tile(gn, RW_H).reshape(1, n))


def rwkv_mixer(p, prev, s0, mu, w0, w2, a0, a2, g2, k_k, k_a, r_k, gn, nb):
    bsz, t, _ = p.shape
    r, w, k, v, kk, kka, g, bonus = rwkv_prep(p, prev, mu, w0, w2, a0, a2, g2, k_k, k_a, r_k.reshape(-1))
    s0l = s0.transpose(0, 2, 1, 3).reshape(bsz, RW_N, BR_WIDTH)
    o, s = rwkv_scan(r, w, k, v, kk, kka, s0l, nb)
    out = rwkv_post(o.reshape(bsz * t, BR_WIDTH), bonus.reshape(bsz * t, BR_WIDTH), g.reshape(bsz * t, BR_WIDTH), gn)
    s_new = s.reshape(bsz, RW_N, RW_H, RW_N).transpose(0, 2, 1, 3)
    return out.reshape(bsz, t, BR_WIDTH), s_new, p[:, -1]


def _rel_bucket(dist):
    n = jnp.maximum(dist, 0)
    exact = REL_BUCKETS // 2
    nf = jnp.maximum(n, 1).astype(F32)
    large = exact + (jnp.log(nf / exact) / math.log(REL_MAX_DIST / exact) * (REL_BUCKETS - exact)).astype(I32)
    return jnp.where(n < exact, n, jnp.minimum(large, REL_BUCKETS - 1))


def _bucket_bits(bucket):
    return [((bucket >> i) & 1) == 1 for i in range(REL_BUCKETS.bit_length() - 1)]


def _bias_from_bits(bits, tab_ref, head):
    level = [tab_ref[b, head] for b in range(REL_BUCKETS)]
    for bit in bits:
        level = [jnp.where(bit, level[2 * i + 1], level[2 * i]) for i in range(len(level) // 2)]
    return level[0]


def _bias_from_bucket(bucket, tab_ref, head):
    return _bias_from_bits(_bucket_bits(bucket), tab_ref, head)


def _t5_tiles_kernel(tab_ref, o_ref, *, t, head0, window):
    h = pl.program_id(0)
    d = pl.program_id(1)
    dist = d * t + _iota((t, t), 1) - _iota((t, t), 0)
    val = _bias_from_bucket(_rel_bucket(dist), tab_ref, head0 + h)
    valid = dist >= 0
    if window:
        valid = valid & (dist <= WINDOW)
    o_ref[0, 0] = jnp.where(valid, val, NEG)


def t5_tiles(rel_bias, head0, nh, t, window):
    return pl.pallas_call(
        functools.partial(_t5_tiles_kernel, t=t, head0=head0, window=window),
        grid=(nh, 3),
        in_specs=[pl.BlockSpec(memory_space=pltpu.SMEM)],
        out_specs=pl.BlockSpec((1, 1, t, t), lambda h, d: (h, d, 0, 0)),
        out_shape=jax.ShapeDtypeStruct((nh, 3, t, t), F32),
        compiler_params=_cparams("parallel", "parallel"),
        name="t5_tiles",
    )(rel_bias)


def _pair_tables(nq, back):
    qi, kj, bt, fl = [], [], [], []
    for q in range(nq):
        lo = 0 if back is None else max(q - back, 0)
        for k in range(lo, q + 1):
            qi.append(q)
            kj.append(k)
            bt.append(min(q - k, 2))
            fl.append((1 if k == lo else 0) | (2 if k == q else 0))
    return tuple(jnp.asarray(np.asarray(a, np.int32)) for a in (qi, kj, bt, fl))


def _flash_kernel(qi_t, kj_t, bt_t, fl_t, q_ref, k_ref, v_ref, bias_ref, *rest, nrow, t, cw, use_sel, epi, lam_init):
    rest = list(rest)
    sel_ref = rest.pop(0) if use_sel else None
    m_sc, l_sc, acc_sc = rest[-3:]
    o_ref = rest[-4]
    extras = rest[:-4]
    p = pl.program_id(2)
    flags = fl_t[p]
    ncol = nrow * t

    @pl.when((flags & 1) != 0)
    def _():
        m_sc[...] = jnp.full_like(m_sc, NEG)
        l_sc[...] = jnp.zeros_like(l_sc)
        acc_sc[...] = jnp.zeros_like(acc_sc)

    k = k_ref[0, 0]
    vt = v_ref[0, 0]
    bt = bt_t[p]
    nbias = bias_ref.shape[2]
    if use_sel:
        ns = sel_ref.shape[1]
        blk = kj_t[p] * (t // SEL_BLOCK) + _iota((t, ns), 0) // SEL_BLOCK
        expand = (_iota((t, ns), 1) == blk).astype(BF16)
        chosen = _dot(expand, sel_ref[0].astype(BF16))
    m_all, l_all, acc_all = m_sc[...], l_sc[...], acc_sc[...]
    m_out, l_out, acc_out = [], [], []
    scores = [_dot(k, q_ref[0, 0, 0, :, c * cw:(c + 1) * cw]) for c in range(ncol // cw)]
    for c in range(ncol // cw):
        r, off = divmod(c * cw, t)
        cols = slice(c * cw, (c + 1) * cw)
        s = scores[c] + bias_ref[0, bt, r % nbias, :, off:off + cw]
        if use_sel:
            s = jnp.where(chosen[:, off:off + cw] > 0.5, s, NEG)
        m_prev = m_all[:, cols]
        m_new = jnp.maximum(m_prev, jnp.max(s, axis=0, keepdims=True))
        alpha = jnp.exp(m_prev - m_new)
        pr = jnp.exp(s - m_new)
        if use_sel:
            pr = jnp.where(s > 0.5 * NEG, pr, 0.0)
        l_out.append(alpha * l_all[:, cols] + jnp.sum(pr, axis=0, keepdims=True))
        acc_out.append(alpha * acc_all[:, cols] + _dot(vt, pr.astype(BF16)))
        m_out.append(m_new)
    m_sc[...] = jnp.concatenate(m_out, axis=1)
    l_sc[...] = jnp.concatenate(l_out, axis=1)
    acc_sc[...] = jnp.concatenate(acc_out, axis=1)

    @pl.when((flags & 2) != 0)
    def _():
        o = acc_sc[...] / l_sc[...]
        if epi == "plain":
            o_ref[0, 0, 0] = o
        elif epi == "diff":
            lam_ref, gn_ref = extras
            lv = lam_ref[...]
            lam = (jnp.exp(jnp.sum(lv[0:1] * lv[1:2], keepdims=True)) - jnp.exp(jnp.sum(lv[2:3] * lv[3:4], keepdims=True))
                   + lam_init)
            od = o[:, :t] - lam * o[:, t:]
            ms = jnp.mean(od * od, axis=0, keepdims=True)
            o_ref[0, 0, 0] = od * lax.rsqrt(ms + EPS) * gn_ref[...] * (1.0 - lam_init)
        else:
            gate_ref, oc_ref, os_ref = extras
            g = jax.nn.sigmoid(gate_ref[0, 0, 0])
            o_ref[0, 0, 0] = g[0:1] * oc_ref[0, 0, 0] + g[1:2] * os_ref[0, 0, 0] + g[2:3] * o


def flash(qt, k, vt, bias, tables, *, t, sel=None, epi="plain", extras=(), extra_specs=(), lam_init=None):
    bsz, hg, nq, dh, ncol = qt.shape
    nrow = ncol // t
    npairs = tables[0].shape[0]
    in_specs = [
        pl.BlockSpec((1, 1, 1, dh, ncol), lambda b, h, p, qi, kj, bt, fl: (b, h, qi[p], 0, 0)),
        pl.BlockSpec((1, 1, t, dh), lambda b, h, p, qi, kj, bt, fl: (b, h, kj[p], 0)),
        pl.BlockSpec((1, 1, dh, t), lambda b, h, p, qi, kj, bt, fl: (b, h, 0, kj[p])),
        pl.BlockSpec((1,) + bias.shape[1:], lambda b, h, p, qi, kj, bt, fl: (h, 0, 0, 0, 0)),
    ]
    args = [qt, k, vt, bias]
    if sel is not None:
        in_specs.append(pl.BlockSpec((1, sel.shape[1], t), lambda b, h, p, qi, kj, bt, fl: (b, 0, qi[p])))
        args.append(sel)
    in_specs += list(extra_specs)
    args += list(extras)
    n_out = t if epi == "diff" else ncol
    return pl.pallas_call(
        functools.partial(_flash_kernel, nrow=nrow, t=t, cw=min(t, FLASH_COLS), use_sel=sel is not None, epi=epi,
                          lam_init=lam_init),
        grid_spec=pltpu.PrefetchScalarGridSpec(
            num_scalar_prefetch=4,
            grid=(bsz, hg, npairs),
            in_specs=in_specs,
            out_specs=pl.BlockSpec((1, 1, 1, dh, n_out), lambda b, h, p, qi, kj, bt, fl: (b, h, qi[p], 0, 0)),
            scratch_shapes=[pltpu.VMEM((1, ncol), F32), pltpu.VMEM((1, ncol), F32), pltpu.VMEM((dh, ncol), F32)],
        ),
        out_shape=jax.ShapeDtypeStruct((bsz, hg, nq, dh, n_out), F32),
        compiler_params=_cparams("parallel", "parallel", "arbitrary"),
        name="flash_" + epi + ("_sel" if sel is not None else ""),
    )(*tables, *args)


def _to_tiles(x, t):
    b, g, r, tl, d = x.shape
    return x.reshape(b, g, r, tl // t, t, d).transpose(0, 1, 3, 5, 2, 4).reshape(b, g, tl // t, d, r * t)


def _from_tiles(x, r):
    b, g, nq, d, rt = x.shape
    t = rt // r
    return x.reshape(b, g, nq, d, r, t).transpose(0, 1, 4, 2, 5, 3).reshape(b, g, r, nq * t, d)


def _compress_kernel(pt_ref, *refs, npp):
    del pt_ref
    wt_ref = refs[npp]
    a_ref, b_ref = refs[npp + 1:]
    wt = wt_ref[...]
    for i in range(npp):
        x3 = refs[i][0].reshape(PAGE_SIZE // CMP_STRIDE, CMP_STRIDE, 2 * NSA_DH)
        a_ref[0, i * 8:(i + 1) * 8, :] = jnp.sum(x3 * wt[None, 0:CMP_STRIDE], axis=1)
        b_ref[0, i * 8:(i + 1) * 8, :] = jnp.sum(x3 * wt[None, CMP_STRIDE:], axis=1)


def compress(pool, pt, wt, npp):
    bsz, n_pages = pt.shape
    g = PAGE_SIZE // CMP_STRIDE
    page = lambda i: pl.BlockSpec((1, PAGE_SIZE, 2 * NSA_DH), lambda b, j, pt_ref: (pt_ref[b, j * npp + i], 0, 0))
    out = pl.BlockSpec((1, npp * g, 2 * NSA_DH), lambda b, j, pt_ref: (b, j, 0))
    shape = jax.ShapeDtypeStruct((bsz, n_pages * g, 2 * NSA_DH), F32)
    return pl.pallas_call(
        functools.partial(_compress_kernel, npp=npp),
        grid_spec=pltpu.PrefetchScalarGridSpec(
            num_scalar_prefetch=1,
            grid=(bsz, n_pages // npp),
            in_specs=[page(i) for i in range(npp)] + [pl.BlockSpec((CMP_BLOCK, 2 * NSA_DH), lambda b, j, pt_ref: (0, 0))],
            out_specs=[out, out],
        ),
        out_shape=[shape, shape],
        compiler_params=_cparams("parallel", "arbitrary"),
        name="nsa_compress",
    )(pt, *([pool] * npp), wt)


def _nsa_cmp_kernel(tab_ref, q_ref, a_ref, b_ref, tail_ref, o_ref, sel_ref, *, t, qpos0, n_cmp, n_sel):
    nc = a_ref.shape[1]
    ns = sel_ref.shape[-1]
    qi = pl.program_id(1)
    rown = _iota((nc, 2 * NSA_DH), 0)
    bsh = jnp.where(rown == nc - 1, tail_ref[0], pltpu.roll(b_ref[0], nc - 1, 0))
    kcv = jnp.where(rown < n_cmp, a_ref[0] + bsh, 0.0)
    vc = kcv[:, NSA_DH:].astype(BF16)
    kc_hi, kc_lo = _split2(kcv[:, :NSA_DH])
    start = _iota((nc, ns), 0) * CMP_STRIDE
    sblk = _iota((nc, ns), 1) * SEL_BLOCK
    ov = ((start < sblk + SEL_BLOCK) & (start + CMP_BLOCK > sblk)).astype(BF16)
    rc = max(8, min(t, CMP_CHUNK_ROWS))
    scores = []
    for c in range(t // rc):
        qpos = qpos0 + qi * t + c * rc + _iota((rc, nc), 0)
        n = _iota((rc, nc), 1)
        dist = qpos - (n * CMP_STRIDE + CMP_BLOCK - 1)
        valid = (dist >= 0) & (n < n_cmp)
        bits = _bucket_bits(_rel_bucket(dist))
        psum = jnp.zeros((rc, nc), F32)
        for h in range(NSA_H):
            q_hi, q_lo = _split2(q_ref[0, 0, h, c * rc:(c + 1) * rc, :])
            s = _dot_nt(q_hi, kc_hi) + _dot_nt(q_lo, kc_hi) + _dot_nt(q_hi, kc_lo)
            sh = jnp.where(valid, s + _bias_from_bits(bits, tab_ref, h), NEG)
            m = jnp.max(sh, axis=-1, keepdims=True)
            p = jnp.where(valid, jnp.exp(sh - m), 0.0)
            p = p / jnp.maximum(jnp.sum(p, axis=-1, keepdims=True), 1e-30)
            o_ref[0, 0, h, c * rc:(c + 1) * rc, :] = _dot(p.astype(BF16), vc)
            psum = psum + p
        hi, mid, lo = _split3(psum)
        scores.append(_dot(hi, ov) + _dot(mid, ov) + _dot(lo, ov))
    score = scores[0] if len(scores) == 1 else jnp.concatenate(scores, axis=0)
    j = _iota((t, ns), 1)
    cur = (qpos0 + qi * t + _iota((t, ns), 0)) // SEL_BLOCK
    forced = (j == 0) | (j == cur) | (j == cur - 1)
    sc = jnp.where(j <= cur, score + jnp.where(forced, FORCE_SCORE, 0.0), -1.0)
    sc = jnp.where(j < n_sel, sc, -jnp.inf)
    chosen = jnp.zeros((t, ns), F32)
    for _ in range(min(SEL_TOPK, n_sel)):
        m = jnp.max(sc, axis=-1, keepdims=True)
        idx = jnp.min(jnp.where(sc == m, j, ns), axis=-1, keepdims=True)
        hit = j == idx
        chosen = jnp.where(hit, 1.0, chosen)
        sc = jnp.where(hit, -jnp.inf, sc)
    sel_ref[0] = chosen


def nsa_cmp(q4, a, b, tail, rel_bias_nsa, *, t, qpos0, n_cmp, n_sel):
    bsz, _, _, tq, dh = q4.shape
    nc = a.shape[1]
    ns = -(-n_sel // 128) * 128
    return pl.pallas_call(
        functools.partial(_nsa_cmp_kernel, t=t, qpos0=qpos0, n_cmp=n_cmp, n_sel=n_sel),
        grid=(bsz, tq // t),
        in_specs=[
            pl.BlockSpec(memory_space=pltpu.SMEM),
            pl.BlockSpec((1, 1, NSA_H, t, dh), lambda b, i: (b, 0, 0, i, 0)),
            pl.BlockSpec((1, nc, 2 * dh), lambda b, i: (b, 0, 0)),
            pl.BlockSpec((1, nc, 2 * dh), lambda b, i: (b, 0, 0)),
            pl.BlockSpec((1, 1, 2 * dh), lambda b, i: (b, 0, 0)),
        ],
        out_specs=[pl.BlockSpec((1, 1, NSA_H, t, dh), lambda b, i: (b, 0, 0, i, 0)),
                   pl.BlockSpec((1, t, ns), lambda b, i: (b, i, 0))],
        out_shape=[jax.ShapeDtypeStruct((bsz, 1, NSA_H, tq, dh), F32), jax.ShapeDtypeStruct((bsz, tq, ns), F32)],
        compiler_params=_cparams("parallel", "parallel"),
        name="nsa_cmp",
    )(rel_bias_nsa, q4, a, b, tail)


def _cmp_weight_tile(w_cmp):
    return jnp.repeat(w_cmp.T, NSA_DH, axis=1)


def nsa_prompt(qn, kvn, n_g, w_cmp, Wc):
    bsz, t_len, _ = qn.shape
    t = _pick_tile(t_len, 256)
    q4f = (qn.reshape(bsz, t_len, NSA_H, NSA_DH) * NSA_DH ** -0.5).transpose(0, 2, 1, 3)[:, None]
    q4 = q4f.astype(BF16)
    n_pages = t_len // PAGE_SIZE
    pool = kvn[:, :, 0:2 * NSA_DH].reshape(bsz * n_pages, PAGE_SIZE, 2 * NSA_DH)
    pt = jnp.arange(bsz * n_pages, dtype=I32).reshape(bsz, n_pages)
    a, b = compress(pool, pt, _cmp_weight_tile(w_cmp), _pick_tile(n_pages, 16))
    o_c, chosen = nsa_cmp(q4f, a, b, jnp.zeros((bsz, 1, 2 * NSA_DH), F32), Wc['rel_nsa'], t=t, qpos0=0,
                          n_cmp=t_len // CMP_STRIDE - 1, n_sel=t_len // SEL_BLOCK)
    t = _pick_tile(t_len, NSA_FLASH_TILE)
    kv = lambda i: kvn[:, :, i * NSA_DH:(i + 1) * NSA_DH].astype(BF16)[:, None]
    kvt = lambda i: kv(i).transpose(0, 1, 3, 2)
    nq = t_len // t
    qt = _to_tiles(q4, t)
    o_s = flash(qt, kv(2), kvt(3), Wc['tiles_nsa'], _pair_tables(nq, None), t=t, sel=chosen.transpose(0, 2, 1))
    gates = n_g.reshape(bsz, nq, t, NSA_H, 3).transpose(0, 1, 4, 3, 2).reshape(bsz, 1, nq, 3, NSA_H * t)
    tile = lambda w: pl.BlockSpec((1, 1, 1, w, NSA_H * t), lambda b, h, p, qi, kj, bt, fl: (b, h, qi[p], 0, 0))
    o = flash(qt, kv(4), kvt(5), Wc['tiles_win'], _pair_tables(nq, WINDOW // t), t=t, epi="win",
              extras=(gates, _to_tiles(o_c, t), o_s), extra_specs=(tile(3), tile(NSA_DH), tile(NSA_DH)))
    return _from_tiles(o, NSA_H)[:, 0].transpose(0, 2, 1, 3).reshape(bsz, t_len, NSA_H * NSA_DH)


def _lam_init(l):
    return 0.8 - 0.6 * math.exp(-0.3 * l)


def diff_prompt(d_q, d_k, d_v, lam_rows, lam_init, gn, Wc):
    bsz, t_len, _ = d_q.shape
    t = _pick_tile(t_len, DIFF_FLASH_TILE)
    q = d_q.reshape(bsz, t_len, DF_H, 2 * DF_D).transpose(0, 2, 1, 3) * DF_D ** -0.5
    lane = jnp.arange(2 * DF_D) < DF_D
    q2 = jnp.stack([jnp.where(lane, q, 0.0), jnp.where(lane, 0.0, q)], axis=2).astype(BF16)
    k = d_k.reshape(bsz, t_len, DF_H, 2 * DF_D).transpose(0, 2, 1, 3).astype(BF16)
    vt = d_v.reshape(bsz, t_len, DF_H, DF_DV).transpose(0, 2, 3, 1).astype(BF16)
    full = lambda s: pl.BlockSpec(s, lambda b, h, p, qi, kj, bt, fl: (0,) * len(s))
    o = flash(_to_tiles(q2, t), k, vt, Wc['tiles_diff'], _pair_tables(t_len // t, None), t=t, epi="diff",
              lam_init=lam_init, extras=(lam_rows, gn.reshape(DF_DV, 1)),
              extra_specs=(full((4, DF_D)), full((DF_DV, 1))))
    return _from_tiles(o, 1)[:, :, 0].transpose(0, 2, 1, 3).reshape(bsz, t_len, BR_WIDTH)


NEW_PAD = 16


def _paged_attn_kernel(pt_ref, tab_ref, q_ref, *refs, npp, head_cols, kpos0, qpos0, t_new, window, use_sel):
    del pt_ref
    pages = refs[:npp]
    newk_ref, newv_ref = refs[npp], refs[npp + 1]
    sel_ref = refs[npp + 2] if use_sel else None
    o_ref, m_sc, l_sc, acc_sc = refs[-4:]
    j = pl.program_id(1)
    tq = t_new
    ng = len(head_cols)
    nrow = ng * tq
    ks = npp * PAGE_SIZE
    hkv, rh = q_ref.shape[1], q_ref.shape[2]
    hrows = lambda x, h: x[h * rh:(h + 1) * rh]

    @pl.when(j == 0)
    def _():
        m_sc[...] = jnp.full_like(m_sc, NEG)
        l_sc[...] = jnp.zeros_like(l_sc)
        acc_sc[...] = jnp.zeros_like(acc_sc)

    qs = [q_ref[0, h] for h in range(hkv)]

    def update(s, kpos, extra_valid, blocks, pv):
        n = s.shape[-1]
        dist = (qpos0 + _iota((tq, n), 0)) - kpos
        valid = dist >= 0
        if window:
            valid = valid & (dist <= WINDOW)
        if extra_valid is not None:
            valid = valid & extra_valid
        if use_sel:
            nsb = sel_ref.shape[-1]
            expand = (_iota((nsb, n), 0) == blocks).astype(BF16)
            valid = valid & (_dot(sel_ref[0].astype(BF16), expand) > 0.5)
        bits = _bucket_bits(_rel_bucket(dist))
        bias = {c: _bias_from_bits(bits, tab_ref, c) for c in sorted(set(head_cols))}
        s3 = s.reshape(ng, tq, n) + jnp.stack([bias[c] for c in head_cols])
        s = jnp.where(valid[None], s3, NEG).reshape(nrow, n)
        m_prev = m_sc[...]
        m_new = jnp.maximum(m_prev, jnp.max(s, axis=-1, keepdims=True))
        alpha = jnp.exp(m_prev - m_new)
        pr = jnp.where(s > 0.5 * NEG, jnp.exp(s - m_new), 0.0)
        l_sc[...] = alpha * l_sc[...] + jnp.sum(pr, axis=-1, keepdims=True)
        acc_sc[...] = alpha * acc_sc[...] + pv(pr.astype(BF16))
        m_sc[...] = m_new

    cat = lambda xs, axis: xs[0] if len(xs) == 1 else jnp.concatenate(xs, axis=axis)
    kt = [[pages[i][0, 0, 0, h].astype(BF16) for h in range(hkv)] for i in range(npp)]
    vt = [[pages[i][0, 0, 1, h].astype(BF16) for h in range(hkv)] for i in range(npp)]
    s = cat([cat([_dot(qs[h], kt[i][h]) for i in range(npp)], 1) for h in range(hkv)], 0)
    kpos = kpos0 + j * ks + _iota((tq, ks), 1)
    blocks = (kpos0 + j * ks + _iota((1, ks), 1)) // SEL_BLOCK

    def pv_pages(pb):
        outs = []
        for h in range(hkv):
            ph = hrows(pb, h)
            out = _dot_nt(ph[:, 0:PAGE_SIZE], vt[0][h])
            for i in range(1, npp):
                out = out + _dot_nt(ph[:, i * PAGE_SIZE:(i + 1) * PAGE_SIZE], vt[i][h])
            outs.append(out)
        return cat(outs, 0)

    update(s, kpos, None, blocks, pv_pages)

    @pl.when(j == pl.num_programs(1) - 1)
    def _():
        nk = [newk_ref[0, h].astype(BF16) for h in range(hkv)]
        nv = [newv_ref[0, h].astype(BF16) for h in range(hkv)]
        col = _iota((tq, NEW_PAD), 1)
        update(cat([_dot_nt(qs[h], nk[h]) for h in range(hkv)], 0), qpos0 + col, col < t_new,
               (qpos0 + _iota((1, NEW_PAD), 1)) // SEL_BLOCK,
               lambda pb: cat([_dot(hrows(pb, h), nv[h]) for h in range(hkv)], 0))
        o_ref[0] = acc_sc[...] / l_sc[...]


def paged_attn(q, pool, layer, pt, page_index, newk, newv, tab, *, npp, head_cols, kpos0, qpos0, window=False,
               sel=None):
    bsz, hkv, rh, dh = q.shape
    nrow = hkv * rh
    lw = dh
    n_pages = pt.shape[1]
    t_new = nrow // len(head_cols)
    page = lambda i: pl.BlockSpec((1, 1, 2, hkv, dh, PAGE_SIZE),
                                  lambda b, j, pt_ref: (layer,) + page_index(b, j * npp + i, pt_ref))
    new_spec = pl.BlockSpec((1, hkv, NEW_PAD, dh), lambda b, j, pt_ref: (b, 0, 0, 0))
    in_specs = [pl.BlockSpec(memory_space=pltpu.SMEM), pl.BlockSpec((1, hkv, rh, dh), lambda b, j, pt_ref: (b, 0, 0, 0))]
    in_specs += [page(i) for i in range(npp)]
    in_specs += [new_spec, new_spec]
    args = [tab, q] + [pool] * npp + [newk, newv]
    if sel is not None:
        in_specs.append(pl.BlockSpec((1, t_new, sel.shape[-1]), lambda b, j, pt_ref: (b, 0, 0)))
        args.append(sel)
    return pl.pallas_call(
        functools.partial(_paged_attn_kernel, npp=npp, head_cols=tuple(head_cols), kpos0=kpos0, qpos0=qpos0,
                          t_new=t_new, window=window, use_sel=sel is not None),
        grid_spec=pltpu.PrefetchScalarGridSpec(
            num_scalar_prefetch=1,
            grid=(bsz, n_pages // npp),
            in_specs=in_specs,
            out_specs=pl.BlockSpec((1, nrow, lw), lambda b, j, pt_ref: (b, 0, 0)),
            scratch_shapes=[pltpu.VMEM((nrow, 1), F32), pltpu.VMEM((nrow, 1), F32), pltpu.VMEM((nrow, lw), F32)],
        ),
        out_shape=jax.ShapeDtypeStruct((bsz, nrow, lw), F32),
        compiler_params=_cparams("parallel", "arbitrary"),
        name="paged_attn",
    )(pt, *args)


def _nsa_combine_kernel(g_ref, oc_ref, os_ref, ow_ref, o_ref):
    g = jax.nn.sigmoid(g_ref[...])
    o_ref[...] = g[..., 0:1] * oc_ref[...] + g[..., 1:2] * os_ref[...] + g[..., 2:3] * ow_ref[...]


def nsa_combine(gates, o_c, o_s, o_w):
    n, dh = o_c.shape
    full = lambda w: pl.BlockSpec((n, w), lambda i: (0, 0))
    return pl.pallas_call(
        _nsa_combine_kernel, grid=(1,),
        in_specs=[full(3), full(dh), full(dh), full(dh)], out_specs=full(dh),
        out_shape=jax.ShapeDtypeStruct((n, dh), F32), name="nsa_combine",
    )(gates, o_c, o_s, o_w)


def _pad_rows(x, n):
    return jnp.pad(x, ((0, 0), (0, n - x.shape[1]), (0, 0)))


def _compress_t_kernel(pt_ref, *refs, npp):
    del pt_ref
    pages = refs[:npp]
    whi_ref, wlo_ref = refs[npp], refs[npp + 1]
    a_ref, b_ref = refs[npp + 2:]
    for kv in range(2):
        acc = None
        for i in range(npp):
            xh, xl = _split2(pages[i][0, 0, kv, 0])
            y = _dot(xh, whi_ref[kv, i]) + _dot(xl, whi_ref[kv, i]) + _dot(xh, wlo_ref[kv, i])
            acc = y if acc is None else acc + y
        a_ref[0, kv] = acc[:, :128]
        b_ref[0, kv] = acc[:, 128:]


def compress_t(pool, layer, pt, w_cmp):
    npp = 16
    bsz, n_pages = pt.shape
    g = PAGE_SIZE // CMP_STRIDE
    r = np.arange(PAGE_SIZE)
    grp = jnp.asarray((r[:, None] // CMP_STRIDE == np.arange(g)[None, :]).astype(np.float32))
    slot = jnp.eye(npp, dtype=F32)
    halves = []
    for half in range(2):
        wr = w_cmp[:, half * CMP_STRIDE + r % CMP_STRIDE]
        halves.append(jnp.einsum('kr,rg,ip->kirpg', wr, grp, slot).reshape(2, npp, PAGE_SIZE, npp * g))
    wbig = jnp.concatenate(halves, axis=-1)
    whi = wbig.astype(BF16)
    wlo = (wbig - whi.astype(F32)).astype(BF16)
    dh = pool.shape[-2]
    page = lambda i: pl.BlockSpec((1, 1, 2, 1, dh, PAGE_SIZE),
                                  lambda b, j, pt_ref: (layer, pt_ref[b, j * npp + i], 0, 0, 0, 0))
    wspec = pl.BlockSpec((2, npp, PAGE_SIZE, 2 * npp * g), lambda b, j, pt_ref: (0, 0, 0, 0))
    out = pl.BlockSpec((1, 2, dh, npp * g), lambda b, j, pt_ref: (b, 0, 0, j))
    shape = jax.ShapeDtypeStruct((bsz, 2, dh, n_pages * g), F32)
    return pl.pallas_call(
        functools.partial(_compress_t_kernel, npp=npp),
        grid_spec=pltpu.PrefetchScalarGridSpec(
            num_scalar_prefetch=1,
            grid=(bsz, n_pages // npp),
            in_specs=[page(i) for i in range(npp)] + [wspec, wspec],
            out_specs=[out, out],
        ),
        out_shape=[shape, shape],
        compiler_params=_cparams("parallel", "arbitrary"),
        name="nsa_compress_t",
    )(pt, *([pool] * npp), whi, wlo)


def _paged_index(b, page, pt_ref):
    return (pt_ref[b, page], 0, 0, 0, 0)


def _window_index(b, page, pt_ref):
    return (b, 0, 0, 0, page)


def nsa_sample(qn, kvn, n_g, w_cmp, Wc, layer, pool_cmp, pool_sel, page_table, win_t):
    bsz, t_len, _ = qn.shape
    lw = 2 * NSA_DH
    total = PAST_LEN + t_len
    n_grp = -(-total // CMP_STRIDE)
    n_cmp = n_grp - CMP_BLOCK // CMP_STRIDE + 1
    n_sel = -(-total // SEL_BLOCK)
    qf = (qn.reshape(bsz, t_len, NSA_H, NSA_DH) * NSA_DH ** -0.5).transpose(0, 2, 1, 3)
    q1 = qf.astype(BF16).reshape(bsz, 1, NSA_H * t_len, NSA_DH)
    at, bt = compress_t(pool_cmp, layer, page_table, w_cmp)
    rows = lambda x: x.transpose(0, 3, 1, 2).reshape(bsz, x.shape[-1], lw)
    new_page = _pad_rows(kvn[:, :, 0:lw], PAGE_SIZE)
    _, b_new = compress(new_page, jnp.arange(bsz, dtype=I32).reshape(bsz, 1), _cmp_weight_tile(w_cmp), 1)
    o_c, chosen = nsa_cmp(qf[:, None], rows(at), rows(bt), b_new[:, 0:1], Wc['rel_nsa'], t=t_len, qpos0=PAST_LEN,
                          n_cmp=n_cmp, n_sel=n_sel)
    new = lambda i: _pad_rows(kvn[:, :, i * NSA_DH:(i + 1) * NSA_DH], NEW_PAD)[:, None]
    heads = tuple(range(NSA_H))
    o_s = paged_attn(q1, pool_sel, layer, page_table, _paged_index, new(2), new(3), Wc['rel_nsa'], npp=16,
                     head_cols=heads, kpos0=0, qpos0=PAST_LEN, sel=chosen)
    wb = win_t.shape[-1]
    wpages = wb // PAGE_SIZE
    o_w = paged_attn(q1, win_t, layer, jnp.zeros((bsz, wpages), I32), _window_index, new(4), new(5), Wc['rel_nsa'],
                     npp=wpages, head_cols=heads, kpos0=PAST_LEN - wb, qpos0=PAST_LEN, window=True)
    n = bsz * NSA_H * t_len
    gates = n_g.reshape(bsz, t_len, NSA_H, 3).transpose(0, 2, 1, 3).reshape(n, 3)
    o = nsa_combine(gates, o_c.reshape(n, NSA_DH), o_s.reshape(n, NSA_DH), o_w.reshape(n, NSA_DH))
    return o.reshape(bsz, NSA_H, t_len, NSA_DH).transpose(0, 2, 1, 3).reshape(bsz, t_len, NSA_H * NSA_DH)


def _diff_post_kernel(o_ref, lam_ref, gn_ref, out_ref, *, lam_init):
    lv = lam_ref[...]
    lam = (jnp.exp(jnp.sum(lv[0:1] * lv[1:2], keepdims=True)) - jnp.exp(jnp.sum(lv[2:3] * lv[3:4], keepdims=True))
           + lam_init)
    od = o_ref[0] - lam * o_ref[1]
    out_ref[...] = _rms_rows(od, gn_ref[...]) * (1.0 - lam_init)


def diff_post(o2, lam_rows, lam_init, gn):
    _, n, dv = o2.shape
    return pl.pallas_call(
        functools.partial(_diff_post_kernel, lam_init=lam_init), grid=(1,),
        in_specs=[pl.BlockSpec((2, n, dv), lambda i: (0, 0, 0)), pl.BlockSpec((4, DF_D), lambda i: (0, 0)),
                  pl.BlockSpec((1, dv), lambda i: (0, 0))],
        out_specs=pl.BlockSpec((n, dv), lambda i: (0, 0)),
        out_shape=jax.ShapeDtypeStruct((n, dv), F32), name="diff_post",
    )(o2, lam_rows, gn.reshape(1, dv))


def diff_sample(d_q, d_k, d_v, lam_rows, lam_init, gn, Wc, layer, pool, page_table):
    bsz, t_len, _ = d_q.shape
    q = (d_q.reshape(bsz, t_len, DF_H, 2 * DF_D) * DF_D ** -0.5).transpose(0, 2, 1, 3)
    lane = jnp.arange(2 * DF_D) < DF_D
    q2 = jnp.stack([jnp.where(lane, q, 0.0), jnp.where(lane, 0.0, q)], axis=2).astype(BF16)
    q2 = q2.reshape(bsz, DF_H, 2 * t_len, 2 * DF_D)
    new = lambda x: _pad_rows(x, NEW_PAD).reshape(bsz, NEW_PAD, DF_H, DF_DV).transpose(0, 2, 1, 3)
    head_cols = tuple(NSA_H + h for h in range(DF_H) for _ in range(2))
    o = paged_attn(q2, pool, layer, page_table, _paged_index, new(d_k), new(d_v), Wc['rel_all'], npp=16,
                   head_cols=head_cols, kpos0=0, qpos0=PAST_LEN)
    n = bsz * DF_H * t_len
    o2 = o.reshape(bsz, DF_H, 2, t_len, DF_DV).transpose(2, 0, 1, 3, 4).reshape(2, n, DF_DV)
    od = diff_post(o2, lam_rows, lam_init, gn).reshape(bsz, DF_H, t_len, DF_DV)
    return od.transpose(0, 2, 1, 3).reshape(bsz, t_len, BR_WIDTH)


def prepare_weights(W):
    Wc = {}
    w_in = W['w_in']
    Wc['w_main'] = jnp.concatenate(
        [jnp.pad(w_in[:, :, IN_OFFS[g]:IN_OFFS[g + 1]], ((0, 0), (0, 0), (0, GROUP_PADS[g] - IN_WIDTHS[g])))
         for g in range(len(GROUP_PADS))], axis=-1).astype(BF16)
    Wc['w_gate'] = w_in[:, :, MAIN_COLS:].reshape(DEPTH, D_MODEL, N_BRANCH, D_MODEL).transpose(0, 2, 1, 3).astype(BF16)
    for name in ('w_br', 'w_out', 'ffn_w1', 'ffn_w3', 'ffn_w2', 'moe_w1', 'moe_w3', 'moe_w2'):
        Wc[name] = W[name].astype(BF16)
    Wc['moe_router'] = jnp.pad(W['moe_router'], ((0, 0), (0, 0), (0, 128 - N_EXPERTS)))
    rel = W['rel_bias']
    Wc['rel_nsa'] = rel[:, :NSA_H]
    Wc['rel_all'] = rel
    Wc['tiles_nsa'] = t5_tiles(rel, 0, NSA_H, NSA_FLASH_TILE, False).transpose(1, 0, 2, 3)[None]
    Wc['tiles_win'] = t5_tiles(rel, 0, NSA_H, NSA_FLASH_TILE, True).transpose(1, 0, 2, 3)[None]
    Wc['tiles_diff'] = t5_tiles(rel, NSA_H, DF_H, DIFF_FLASH_TILE, False)[:, :, None]
    return Wc


def token_mix(h, l, W, Wc, past, page_table):
    B, T, _ = h.shape
    m = B * T
    nb_state = 2 if past is None else 8
    proj = in_proj(h.reshape(m, D_MODEL), Wc['w_main'][l])
    g_q, g_k, g_v, g_a, g_r, n_q, n_kv, n_g, d_q, d_k, d_v, rw = [p.reshape(B, T, p.shape[-1]) for p in proj]
    n_g = n_g[:, :, :IN_WIDTHS[7]]
    st = {}
    s0 = jnp.zeros((B, GLA_H * GLA_DK, GLA_H * GLA_DV), F32) if past is None else gla_state_to_bd(past['gla'][l])
    o_a, s_bd = gla_mixer(g_q, g_k, g_v, g_a, g_r, W['gla_wa2'][l], W['gla_ba'][l], W['gla_norm_g'][l], s0, nb_state)
    st['gla'] = gla_state_from_bd(s_bd)
    kvn = n_kv.reshape(B, T, 6, NSA_DH)
    if past is None:
        o_b = nsa_prompt(n_q, n_kv, n_g, W['nsa_cmp_w'][l], Wc)
        st['win'] = kvn[:, T - min(WINDOW, T):, 4:6]
    else:
        o_b = nsa_sample(n_q, n_kv, n_g, W['nsa_cmp_w'][l], Wc, l, past['cmp_t'], past['sel_t'], page_table,
                         past['win_t'])
        st['win'] = jnp.concatenate([past['win'][l], kvn[:, :, 4:6]], axis=1)[:, T:]
    st['cmp'] = kvn[:, :, 0:2]
    st['sel'] = kvn[:, :, 2:4]
    prev = jnp.zeros((B, RW_PROJ), F32) if past is None else past['shift'][l]
    s0 = jnp.zeros((B, RW_H, RW_N, RW_N), F32) if past is None else past['rwkv'][l]
    o_c, st['rwkv'], st['shift'] = rwkv_mixer(
        rw, prev, s0, W['rw_mu'][l], W['rw_w0'][l], W['rw_w2'][l], W['rw_a0'][l], W['rw_a2'][l], W['rw_g2'][l],
        W['rw_kk'][l], W['rw_ka'][l], W['rw_rk'][l], W['rw_norm_g'][l], nb_state)
    lam_init = _lam_init(l)
    if past is None:
        o_d = diff_prompt(d_q, d_k, d_v, W['df_lam'][l], lam_init, W['df_norm_g'][l], Wc)
    else:
        o_d = diff_sample(d_q, d_k, d_v, W['df_lam'][l], lam_init, W['df_norm_g'][l], Wc, l, past['diff_t'],
                          page_table)
    st['diff'] = jnp.stack([d_k.reshape(B, T, DF_H, 2 * DF_D), d_v.reshape(B, T, DF_H, DF_DV)], axis=2)
    return [t.reshape(m, BR_WIDTH) for t in (o_a, o_b, o_c, o_d)], st


def trunk(x, W, Wc, cache, page_table):
    B, T, _ = x.shape
    x2 = x.reshape(B * T, D_MODEL)
    new = {}
    for l in range(DEPTH):
        past = cache
        h = rmsnorm(x2, W['norm1_g'][l], BF16)
        brs, st = token_mix(h.reshape(B, T, D_MODEL), l, W, Wc, past, page_table)
        x2 = merge(h, brs, x2, Wc['w_gate'][l], Wc['w_br'][l], Wc['w_out'][l])
        j = l // 2
        if l % 2 == 0:
            x2 = ffn(x2, W['norm2_g'][l], Wc['ffn_w1'][j], Wc['ffn_w3'][j], Wc['ffn_w2'][j])
        else:
            x2 = moe(x2, W['norm2_g'][l], Wc['moe_router'][j], Wc['moe_w1'][j], Wc['moe_w3'][j], Wc['moe_w2'][j])
        for name, arr in st.items():
            new.setdefault(name, []).append(arr)
    y = rmsnorm(x2, W['final_norm_g'], F32).reshape(B, T, D_MODEL)
    return y, {name: jnp.stack(arrs) for name, arrs in new.items()}


def kernel(x_prompt, x_sample, cache_nsa_cmp, cache_nsa_sel, cache_diff, state_nsa_win, state_gla, state_rwkv, state_rwkv_shift, page_table, norm1_g, norm2_g, final_norm_g, w_in, gla_wa2, gla_ba, gla_norm_g, nsa_cmp_w, rw_mu, rw_w0, rw_w2, rw_a0, rw_a2, rw_g2, rw_kk, rw_ka, rw_rk, rw_norm_g, df_lam, df_norm_g, w_br, w_out, rel_bias, ffn_w1, ffn_w3, ffn_w2, moe_router, moe_w1, moe_w3, moe_w2):
    W = dict(norm1_g=norm1_g, norm2_g=norm2_g, final_norm_g=final_norm_g, w_in=w_in, gla_wa2=gla_wa2,
             gla_ba=gla_ba, gla_norm_g=gla_norm_g, nsa_cmp_w=nsa_cmp_w, rw_mu=rw_mu, rw_w0=rw_w0, rw_w2=rw_w2,
             rw_a0=rw_a0, rw_a2=rw_a2, rw_g2=rw_g2, rw_kk=rw_kk, rw_ka=rw_ka, rw_rk=rw_rk, rw_norm_g=rw_norm_g,
             df_lam=df_lam, df_norm_g=df_norm_g, w_br=w_br, w_out=w_out, rel_bias=rel_bias, ffn_w1=ffn_w1,
             ffn_w3=ffn_w3, ffn_w2=ffn_w2, moe_router=moe_router, moe_w1=moe_w1, moe_w3=moe_w3, moe_w2=moe_w2)
    rows_last = lambda a: jnp.moveaxis(a, 2, -1)
    cache = dict(cmp_t=rows_last(cache_nsa_cmp)[:, :, :, None], sel_t=rows_last(cache_nsa_sel)[:, :, :, None],
                 diff_t=rows_last(cache_diff), win_t=rows_last(state_nsa_win)[:, :, :, None], win=state_nsa_win,
                 gla=state_gla, rwkv=state_rwkv, shift=state_rwkv_shift)
    Wc = prepare_weights(W)
    y_prompt, sp = trunk(x_prompt, W, Wc, None, None)
    y_sample, ss = trunk(x_sample, W, Wc, cache, page_table)
    return (y_prompt, y_sample,
            sp['cmp'], sp['sel'], sp['diff'], sp['win'], sp['gla'], sp['rwkv'], sp['shift'],
            ss['cmp'], ss['sel'], ss['diff'], ss['win'], ss['gla'], ss['rwkv'], ss['shift'])
```

```python
import functools
import math

import numpy as np
import jax
import jax.numpy as jnp
from jax import lax
from jax.experimental import pallas as pl
from jax.experimental.pallas import tpu as pltpu

F32 = jnp.float32
BF16 = jnp.bfloat16
I32 = jnp.int32

D_MODEL = 1024
DEPTH = 2
PAST_LEN = 16384
PAGE_SIZE = 128
N_BRANCH = 4
BR_WIDTH = 256
GLA_H, GLA_DK, GLA_DV, GLA_RANK = 4, 32, 64, 16
GLA_TAU = 16.0
GLA_CHUNK = 64
NSA_H, NSA_DH = 4, 64
CMP_BLOCK, CMP_STRIDE, SEL_BLOCK, SEL_TOPK, WINDOW = 32, 16, 64, 16, 512
FORCE_SCORE = 1.0e4
RW_H, RW_N = 4, 64
RW_PROJ = 1024
DF_H, DF_D, DF_DV = 4, 32, 64
REL_BUCKETS, REL_MAX_DIST = 32, 128
N_EXPERTS, TOP_K = 8, 2
Q_BLOCK = 128
EPS = 1e-6
NEG = -1e30

IN_WIDTHS = (128, 128, 256, 16, 256, 256, 384, 12, 256, 256, 256, RW_PROJ, N_BRANCH * D_MODEL)
IN_OFFS = tuple(int(s) for s in np.cumsum((0,) + IN_WIDTHS))
MAIN_COLS = IN_OFFS[12]
LANES = 128
GROUP_PADS = tuple(-(-w // LANES) * LANES for w in IN_WIDTHS[:12])
GROUP_OFFS = tuple(int(s) for s in np.cumsum((0,) + GROUP_PADS))
MAIN_PAD = GROUP_OFFS[-1]

VMEM_LIMIT_BYTES = 56 * 1024 * 1024
CMP_CHUNK_ROWS = 256
FLASH_COLS = 512
NSA_FLASH_TILE = 512
DIFF_FLASH_TILE = 1024
RW_LOOKAHEAD = 2


def _cparams(*sem):
    return pltpu.CompilerParams(dimension_semantics=sem, vmem_limit_bytes=VMEM_LIMIT_BYTES)


def _dot(a, b):
    return jnp.dot(a, b, preferred_element_type=F32)


def _dot_nt(a, b):
    return lax.dot_general(a, b, (((1,), (1,)), ((), ())), preferred_element_type=F32)


def _dot_tn(a, b):
    return lax.dot_general(a, b, (((0,), (0,)), ((), ())), preferred_element_type=F32)


def _split2(x):
    hi = x.astype(BF16)
    lo = (x - hi.astype(F32)).astype(BF16)
    return hi, lo


def _split3(x):
    hi = x.astype(BF16)
    r = x - hi.astype(F32)
    mid = r.astype(BF16)
    lo = (r - mid.astype(F32)).astype(BF16)
    return hi, mid, lo


def _dot_x2(x, e):
    hi, lo = _split2(x)
    return _dot(hi, e) + _dot(lo, e)


def _dot_e3(e, x):
    hi, mid, lo = _split3(x)
    return _dot(e, hi) + _dot(e, mid) + _dot(e, lo)


def _dot_3x(a, b, dot=_dot):
    ah, al = _split2(a)
    bh, bl = _split2(b)
    return dot(ah, bh) + dot(al, bh) + dot(ah, bl)


def _iota(shape, dim):
    return lax.broadcasted_iota(I32, shape, dim)


def _block_ones(n, seg):
    r = _iota((n, n), 0) // seg
    c = _iota((n, n), 1) // seg
    return (r == c).astype(BF16)


def _rms_rows(x, g):
    ms = jnp.mean(x * x, axis=-1, keepdims=True)
    return x * lax.rsqrt(ms + EPS) * g


def _log_sigmoid(x):
    return -(jnp.maximum(-x, 0.0) + jnp.log1p(jnp.exp(-jnp.abs(x))))


def _pick_tile(n, pref):
    t = min(n, pref)
    while n % t:
        t //= 2
    return t


def _rmsnorm_kernel(x_ref, g_ref, o_ref):
    o_ref[...] = _rms_rows(x_ref[...], g_ref[...]).astype(o_ref.dtype)


def rmsnorm(x, g, out_dtype):
    m, d = x.shape
    tm = _pick_tile(m, 512)
    return pl.pallas_call(
        _rmsnorm_kernel,
        grid=(m // tm,),
        in_specs=[pl.BlockSpec((tm, d), lambda i: (i, 0)), pl.BlockSpec((1, d), lambda i: (0, 0))],
        out_specs=pl.BlockSpec((tm, d), lambda i: (i, 0)),
        out_shape=jax.ShapeDtypeStruct((m, d), out_dtype),
        compiler_params=_cparams("parallel"),
        name="rmsnorm",
    )(x, g.reshape(1, d))


def _in_proj_kernel(a_ref, b_ref, *o_refs):
    a = a_ref[...]
    for g, o_ref in enumerate(o_refs):
        o_ref[...] = _dot(a, b_ref[:, GROUP_OFFS[g]:GROUP_OFFS[g + 1]])


def in_proj(a, b):
    m, k = a.shape
    tm = _pick_tile(m, 512)
    return pl.pallas_call(
        _in_proj_kernel,
        grid=(m // tm,),
        in_specs=[pl.BlockSpec((tm, k), lambda i: (i, 0)), pl.BlockSpec((k, MAIN_PAD), lambda i: (0, 0))],
        out_specs=[pl.BlockSpec((tm, w), lambda i: (i, 0)) for w in GROUP_PADS],
        out_shape=[jax.ShapeDtypeStruct((m, w), F32) for w in GROUP_PADS],
        compiler_params=_cparams("parallel"),
        name="in_proj",
    )(a, b)


def _merge_kernel(h_ref, a_ref, b_ref, c_ref, d_ref, x_ref, wg_ref, wbr_ref, wout_ref, o_ref):
    h = h_ref[...]
    acc = None
    for n, br_ref in enumerate((a_ref, b_ref, c_ref, d_ref)):
        gate = jax.nn.sigmoid(_dot(h, wg_ref[n]))
        up = _dot(br_ref[...].astype(BF16), wbr_ref[n])
        acc = gate * up if acc is None else acc + gate * up
    o_ref[...] = x_ref[...] + _dot(acc.astype(BF16), wout_ref[...])


def merge(h, brs, x, wg, wbr, wout):
    m, d = x.shape
    tm = _pick_tile(m, 256)
    row = lambda i: (i, 0)
    return pl.pallas_call(
        _merge_kernel,
        grid=(m // tm,),
        in_specs=[pl.BlockSpec((tm, d), row)] + [pl.BlockSpec((tm, BR_WIDTH), row)] * 4 + [
            pl.BlockSpec((tm, d), row),
            pl.BlockSpec((N_BRANCH, d, d), lambda i: (0, 0, 0)),
            pl.BlockSpec((N_BRANCH, BR_WIDTH, d), lambda i: (0, 0, 0)),
            pl.BlockSpec((d, d), lambda i: (0, 0)),
        ],
        out_specs=pl.BlockSpec((tm, d), row),
        out_shape=jax.ShapeDtypeStruct((m, d), F32),
        compiler_params=_cparams("parallel"),
        name="merge",
    )(h, *brs, x, wg, wbr, wout)


def _ffn_kernel(x_ref, g_ref, w1_ref, w3_ref, w2_ref, o_ref, h_sc, acc_sc):
    j = pl.program_id(1)

    @pl.when(j == 0)
    def _():
        h_sc[...] = _rms_rows(x_ref[...], g_ref[...]).astype(BF16)
        acc_sc[...] = jnp.zeros_like(acc_sc)

    h = h_sc[...]
    a = _dot(h, w1_ref[...])
    b = _dot(h, w3_ref[...])
    t = (a * jax.nn.sigmoid(a)) * b
    acc_sc[...] += _dot(t.astype(BF16), w2_ref[...])

    @pl.when(j == pl.num_programs(1) - 1)
    def _():
        o_ref[...] = x_ref[...] + acc_sc[...]


def ffn(x, g, w1, w3, w2):
    m, d = x.shape
    ff = w1.shape[1]
    tm = _pick_tile(m, 1024)
    tf = 256
    return pl.pallas_call(
        _ffn_kernel,
        grid=(m // tm, ff // tf),
        in_specs=[
            pl.BlockSpec((tm, d), lambda i, j: (i, 0)),
            pl.BlockSpec((1, d), lambda i, j: (0, 0)),
            pl.BlockSpec((d, tf), lambda i, j: (0, j)),
            pl.BlockSpec((d, tf), lambda i, j: (0, j)),
            pl.BlockSpec((tf, d), lambda i, j: (j, 0)),
        ],
        out_specs=pl.BlockSpec((tm, d), lambda i, j: (i, 0)),
        out_shape=jax.ShapeDtypeStruct((m, d), F32),
        scratch_shapes=[pltpu.VMEM((tm, d), BF16), pltpu.VMEM((tm, d), F32)],
        compiler_params=_cparams("parallel", "arbitrary"),
        name="ffn",
    )(x, g.reshape(1, d), w1, w3, w2)


def _moe_kernel(x_ref, g_ref, wr_ref, w1_ref, w3_ref, w2_ref, o_ref, h_sc, acc_sc, comb_sc):
    e = pl.program_id(1)
    j = pl.program_id(2)
    first = (e == 0) & (j == 0)
    last = (e == pl.num_programs(1) - 1) & (j == pl.num_programs(2) - 1)

    @pl.when(first)
    def _():
        hf = _rms_rows(x_ref[...], g_ref[...])
        h_sc[...] = hf.astype(BF16)
        acc_sc[...] = jnp.zeros_like(acc_sc)
        logits = _dot_3x(hf, wr_ref[...])
        lane = _iota(logits.shape, 1)
        lg = jnp.where(lane < N_EXPERTS, logits, -jnp.inf)
        m1 = jnp.max(lg, axis=-1, keepdims=True)
        i1 = jnp.min(jnp.where(lg == m1, lane, 128), axis=-1, keepdims=True)
        lg2 = jnp.where(lane == i1, -jnp.inf, lg)
        m2 = jnp.max(lg2, axis=-1, keepdims=True)
        i2 = jnp.min(jnp.where(lg2 == m2, lane, 128), axis=-1, keepdims=True)
        e2 = jnp.exp(m2 - m1)
        den = 1.0 + e2
        comb_sc[...] = jnp.where(lane == i1, 1.0 / den, 0.0) + jnp.where(lane == i2, e2 / den, 0.0)

    h = h_sc[...]
    a = _dot(h, w1_ref[0])
    b = _dot(h, w3_ref[0])
    t = (a * jax.nn.sigmoid(a)) * b
    comb = comb_sc[...]
    c = jnp.sum(jnp.where(_iota(comb.shape, 1) == e, comb, 0.0), axis=-1, keepdims=True)
    acc_sc[...] += c * _dot(t.astype(BF16), w2_ref[0])

    @pl.when(last)
    def _():
        o_ref[...] = x_ref[...] + acc_sc[...]


def moe(x, g, wr_pad, w1, w3, w2):
    m, d = x.shape
    ne, _, ff = w1.shape
    tm = _pick_tile(m, 1024)
    tf = _pick_tile(ff, 512)
    return pl.pallas_call(
        _moe_kernel,
        grid=(m // tm, ne, ff // tf),
        in_specs=[
            pl.BlockSpec((tm, d), lambda i, e, j: (i, 0)),
            pl.BlockSpec((1, d), lambda i, e, j: (0, 0)),
            pl.BlockSpec((d, 128), lambda i, e, j: (0, 0)),
            pl.BlockSpec((1, d, tf), lambda i, e, j: (e, 0, j)),
            pl.BlockSpec((1, d, tf), lambda i, e, j: (e, 0, j)),
            pl.BlockSpec((1, tf, d), lambda i, e, j: (e, j, 0)),
        ],
        out_specs=pl.BlockSpec((tm, d), lambda i, e, j: (i, 0)),
        out_shape=jax.ShapeDtypeStruct((m, d), F32),
        scratch_shapes=[pltpu.VMEM((tm, d), BF16), pltpu.VMEM((tm, d), F32), pltpu.VMEM((tm, 128), F32)],
        compiler_params=_cparams("parallel", "arbitrary", "arbitrary"),
        name="moe",
    )(x, g.reshape(1, d), wr_pad, w1, w3, w2)


def _gla_kernel(q_ref, k_ref, v_ref, a_ref, r_ref, wa2_ref, wa2t_ref, ba_ref, bacol_ref, gn_ref, s0_ref,
                o_ref, sout_ref, s_sc, b_sc, k_sc, v_sc):
    nb, c, _ = q_ref.shape
    ci = pl.program_id(1)

    @pl.when(ci == 0)
    def _():
        s_sc[...] = s0_ref[...]

    hk = GLA_H * GLA_DK
    hv = GLA_H * GLA_DV
    tri = (_iota((c, c), 0) >= _iota((c, c), 1)).astype(BF16)
    expand = (_iota((hk, hv), 0) // GLA_DK == _iota((hk, hv), 1) // GLA_DV)
    expand_bf = expand.astype(BF16)
    ones_v = _block_ones(hv, GLA_DV)
    wa2 = wa2_ref[...].astype(BF16)
    wa2t = wa2t_ref[...].astype(BF16)
    rowi = _iota((nb, c, hk), 1)

    qs, bs, os1 = [], [], []
    for n in range(nb):
        a_in = a_ref[n].astype(BF16)
        g = _log_sigmoid(_dot(a_in, wa2) + ba_ref[...]) / GLA_TAU
        b = _dot_e3(tri, g)
        gt = _log_sigmoid(_dot_nt(wa2t, a_in) + bacol_ref[...]) / GLA_TAU
        bl_col = jnp.sum(gt, axis=1, keepdims=True)
        q = q_ref[n] * (GLA_DK ** -0.5)
        k = k_ref[n]
        v = v_ref[n]
        s_old = s_sc[n]
        os1.append(_dot_3x(q * jnp.exp(b), s_old))
        kd = k * jnp.exp(b[c - 1:c, :] - b)
        upd = _dot_3x(kd, v, dot=_dot_tn)
        s_sc[n] = s_old * jnp.exp(bl_col) + jnp.where(expand, upd, 0.0)
        qs.append(q)
        b_sc[n] = b
        k_sc[n] = k
        v_sc[n] = v
        bs.append(b)
    q3 = jnp.stack(qs)
    b3 = jnp.stack(bs)

    def body(s, o2):
        b_s = b_sc[:, pl.ds(s, 1), :]
        k_s = k_sc[:, pl.ds(s, 1), :]
        v_s = v_sc[:, pl.ds(s, 1), :]
        dec = jnp.exp(jnp.where(rowi >= s, b3 - b_s, -jnp.inf))
        contrib = (q3 * k_s * dec).reshape(nb * c, hk)
        att = _dot(contrib.astype(BF16), expand_bf).reshape(nb, c, hv)
        return o2 + att * v_s

    o2 = lax.fori_loop(0, c, body, jnp.zeros((nb, c, hv), F32), unroll=4)
    for n in range(nb):
        o = os1[n] + o2[n]
        ms = _dot_x2(o * o, ones_v) * (1.0 / GLA_DV)
        o = o * lax.rsqrt(ms + EPS) * gn_ref[...]
        r = r_ref[n]
        o_ref[n] = o * (r * jax.nn.sigmoid(r))

    @pl.when(ci == pl.num_programs(1) - 1)
    def _():
        sout_ref[...] = s_sc[...]


def gla_mixer(q, k, v, a_in, r, wa2, ba, gn, s0_bd, nb):
    bsz, t, hk = q.shape
    hv = v.shape[-1]
    ra = a_in.shape[-1]
    wa2 = jnp.pad(wa2, ((0, ra - GLA_RANK), (0, 0)))
    c = GLA_CHUNK if t % GLA_CHUNK == 0 else t
    tok = lambda w: pl.BlockSpec((nb, c, w), lambda b, i: (b, i, 0))
    full = lambda s: pl.BlockSpec(s, lambda b, i: (0,) * len(s))
    st = pl.BlockSpec((nb, hk, hv), lambda b, i: (b, 0, 0))
    return pl.pallas_call(
        _gla_kernel,
        grid=(bsz // nb, t // c),
        in_specs=[tok(hk), tok(hk), tok(hv), tok(ra), tok(hv),
                  full((ra, hk)), full((hk, ra)), full((1, hk)), full((hk, 1)), full((1, hv)), st],
        out_specs=[tok(hv), st],
        out_shape=[jax.ShapeDtypeStruct((bsz, t, hv), F32), jax.ShapeDtypeStruct((bsz, hk, hv), F32)],
        scratch_shapes=[pltpu.VMEM((nb, hk, hv), F32), pltpu.VMEM((nb, c, hk), F32),
                        pltpu.VMEM((nb, c, hk), F32), pltpu.VMEM((nb, c, hv), F32)],
        compiler_params=_cparams("parallel", "arbitrary"),
        name="gla",
    )(q, k, v, a_in, r, wa2, wa2.T, ba.reshape(1, hk), ba.reshape(hk, 1), jnp.tile(gn, GLA_H).reshape(1, hv), s0_bd)


def gla_state_to_bd(s):
    b = s.shape[0]
    eye = jnp.eye(GLA_H, dtype=s.dtype)
    return jnp.einsum('bhkv,hg->bhkgv', s, eye).reshape(b, GLA_H * GLA_DK, GLA_H * GLA_DV)


def gla_state_from_bd(sbd):
    b = sbd.shape[0]
    s5 = sbd.reshape(b, GLA_H, GLA_DK, GLA_H, GLA_DV)
    return jnp.stack([s5[:, h, :, h, :] for h in range(GLA_H)], axis=1)


def _rwkv_prep_kernel(p_ref, first_ref, mu_ref, w0_ref, w2_ref, a0_ref, a2_ref, g2_ref, kk_ref, ka_ref, rk_ref,
                      r_o, w_o, k_o, v_o, kk_o, kka_o, g_o, bonus_o):
    p = p_ref[0]
    prev = jnp.where(_iota(p.shape, 0) == 0, first_ref[0, 0], pltpu.roll(p, 1, 0))
    xm = p + (prev - p) * mu_ref[...]
    n = BR_WIDTH
    r, k, v = xm[:, 0:n], xm[:, n:2 * n], xm[:, 2 * n:3 * n]
    xwa = xm[:, 3 * n:3 * n + 128]
    xg = xm[:, 3 * n + 128:]
    decay = jnp.exp(-math.exp(-0.5) * jax.nn.sigmoid(w0_ref[...] + _dot(jnp.tanh(xwa).astype(BF16), w2_ref[...])))
    a = jax.nn.sigmoid(a0_ref[...] + _dot(xwa.astype(BF16), a2_ref[...]))
    g = _dot(jax.nn.sigmoid(xg).astype(BF16), g2_ref[...])
    ones = _block_ones(n, RW_N)
    kk = k * kk_ref[...]
    kk = kk * lax.rsqrt(jnp.maximum(_dot_x2(kk * kk, ones), 1e-12))
    k2 = k * (1.0 + (a - 1.0) * ka_ref[...])
    r_o[0] = r
    w_o[0] = decay
    k_o[0] = k2
    v_o[0] = v
    kk_o[0] = kk
    kka_o[0] = kk * a
    g_o[0] = g
    bonus_o[0] = _dot_x2(r * k2 * rk_ref[...], ones) * v


def rwkv_prep(p, prev, mu, w0, w2, a0, a2, g2, k_k, k_a, r_k):
    bsz, t, d = p.shape
    tm = _pick_tile(t, 512)
    nt = t // tm
    first = jnp.concatenate([prev[:, None, :], p[:, tm - 1:t - 1:tm, :]], axis=1).reshape(bsz, nt, 1, d)
    n = BR_WIDTH
    w2p = jnp.concatenate([w2, jnp.zeros_like(w2)], axis=0).astype(BF16)
    a2p = jnp.concatenate([jnp.zeros_like(a2), a2], axis=0).astype(BF16)
    row = lambda v: v.reshape(1, -1)
    full = lambda s: pl.BlockSpec(s, lambda b, i: (0,) * len(s))
    tok = pl.BlockSpec((1, tm, n), lambda b, i: (b, i, 0))
    return pl.pallas_call(
        _rwkv_prep_kernel,
        grid=(bsz, nt),
        in_specs=[pl.BlockSpec((1, tm, d), lambda b, i: (b, i, 0)),
                  pl.BlockSpec((1, 1, 1, d), lambda b, i: (b, i, 0, 0)),
                  full((1, d)), full((1, n)), full((128, n)), full((1, n)), full((128, n)), full((128, n)),
                  full((1, n)), full((1, n)), full((1, n))],
        out_specs=[tok] * 8,
        out_shape=[jax.ShapeDtypeStruct((bsz, t, n), F32)] * 8,
        compiler_params=_cparams("parallel", "parallel"),
        name="rwkv_prep",
    )(p, first, row(mu), row(w0), w2p, row(a0), a2p, g2.astype(BF16), row(k_k), row(k_a), row(r_k))


def _rwkv_scan_kernel(r_ref, w_ref, k_ref, v_ref, kk_ref, kka_ref, s0_ref, o_ref, sout_ref, s_sc):
    nb, tc, n = r_ref.shape
    ti = pl.program_id(1)

    @pl.when(ti == 0)
    def _():
        s_sc[...] = s0_ref[...]

    ones = _block_ones(n, RW_N)
    diag = (_iota((RW_N, n), 0) == (_iota((RW_N, n), 1) % RW_N)).astype(F32)

    nj = RW_LOOKAHEAD

    def seg_many(xs, single_pass=False):
        x = jnp.concatenate([x.reshape(nb * RW_N, n) for x in xs], axis=0)
        y = _dot(x.astype(BF16), ones) if single_pass else _dot_x2(x, ones)
        return [y[j * nb * RW_N:(j + 1) * nb * RW_N].reshape(nb, RW_N, n) for j in range(len(xs))]

    def body(bi, s):
        t0 = bi * nj
        row = lambda ref, j: ref[:, pl.ds(t0 + j, 1), :].reshape(nb, n)
        w, kk, kka = [row(w_ref, j) for j in range(nj)], [row(kk_ref, j) for j in range(nj)], [row(kka_ref, j) for j in range(nj)]
        k, r, v = [row(k_ref, j) for j in range(nj)], [row(r_ref, j) for j in range(nj)], [row(v_ref, j) for j in range(nj)]
        lift = lambda x: x[:, None, :]
        decay = [jnp.ones_like(w[0])]
        for j in range(1, nj):
            decay.append(decay[-1] * w[j - 1])
        seg_rows = lambda x: _dot_x2(x, ones)
        c, d = {}, {}
        for j in range(1, nj):
            between = jnp.ones_like(w[0])
            for i in range(j - 1, -1, -1):
                c[i, j] = lift(seg_rows(kka[i] * between * kk[j]))
                d[i, j] = lift(seg_rows(k[i] * between * kk[j]))
                between = between * w[i]
        u = seg_many([s * lift(decay[j] * kk[j]) for j in range(nj)])
        vcol = seg_many([lift(v[j]) * diag for j in range(nj)])
        sa, states = [], []
        for j in range(nj):
            x = u[j]
            for i in range(j):
                x = x - sa[i] * c[i, j] + vcol[i] * d[i, j]
            sa.append(x)
            s = s * lift(w[j]) - x * lift(kka[j]) + vcol[j] * lift(k[j])
            states.append(s)
        ocol = seg_many([states[j] * lift(r[j]) for j in range(nj)], single_pass=True)
        for j in range(nj):
            o_ref[:, pl.ds(t0 + j, 1), :] = jnp.sum(ocol[j] * diag, axis=1, keepdims=True)
        return s

    s = lax.fori_loop(0, tc // nj, body, s_sc[...], unroll=2)
    s_sc[...] = s

    @pl.when(ti == pl.num_programs(1) - 1)
    def _():
        sout_ref[...] = s


def rwkv_scan(r, w, k, v, kk, kka, s0, nb):
    bsz, t, n = r.shape
    tc = _pick_tile(t, 256)
    tok = pl.BlockSpec((nb, tc, n), lambda b, i: (b, i, 0))
    st = pl.BlockSpec((nb, RW_N, n), lambda b, i: (b, 0, 0))
    return pl.pallas_call(
        _rwkv_scan_kernel,
        grid=(bsz // nb, t // tc),
        in_specs=[tok] * 6 + [st],
        out_specs=[tok, st],
        out_shape=[jax.ShapeDtypeStruct((bsz, t, n), F32), jax.ShapeDtypeStruct((bsz, RW_N, n), F32)],
        scratch_shapes=[pltpu.VMEM((nb, RW_N, n), F32)],
        compiler_params=_cparams("parallel", "arbitrary"),
        name="rwkv_scan",
    )(r, w, k, v, kk, kka, s0)


def _rwkv_post_kernel(o_ref, bonus_ref, g_ref, gn_ref, out_ref):
    o = o_ref[...]
    ms = _dot_x2(o * o, _block_ones(BR_WIDTH, RW_N)) * (1.0 / RW_N)
    out_ref[...] = (o * lax.rsqrt(ms + EPS) * gn_ref[...] + bonus_ref[...]) * g_ref[...]


def rwkv_post(o, bonus, g, gn):
    m, n = o.shape
    tm = _pick_tile(m, 1024)
    row = pl.BlockSpec((tm, n), lambda i: (i, 0))
    return pl.pallas_call(
        _rwkv_post_kernel,
        grid=(m // tm,),
        in_specs=[row, row, row, pl.BlockSpec((1, n), lambda i: (0, 0))],
        out_specs=row,
        out_shape=jax.ShapeDtypeStruct((m, n), F32),
        compiler_params=_cparams("parallel"),
        name="rwkv_post",
    )(o, bonus, g, jnp.tile(gn, RW_H).reshape(1, n))


def rwkv_mixer(p, prev, s0, mu, w0, w2, a0, a2, g2, k_k, k_a, r_k, gn, nb):
    bsz, t, _ = p.shape
    r, w, k, v, kk, kka, g, bonus = rwkv_prep(p, prev, mu, w0, w2, a0, a2, g2, k_k, k_a, r_k.reshape(-1))
    s0l = s0.transpose(0, 2, 1, 3).reshape(bsz, RW_N, BR_WIDTH)
    o, s = rwkv_scan(r, w, k, v, kk, kka, s0l, nb)
    out = rwkv_post(o.reshape(bsz * t, BR_WIDTH), bonus.reshape(bsz * t, BR_WIDTH), g.reshape(bsz * t, BR_WIDTH), gn)
    s_new = s.reshape(bsz, RW_N, RW_H, RW_N).transpose(0, 2, 1, 3)
    return out.reshape(bsz, t, BR_WIDTH), s_new, p[:, -1]


def _rel_bucket(dist):
    n = jnp.maximum(dist, 0)
    exact = REL_BUCKETS // 2
    nf = jnp.maximum(n, 1).astype(F32)
    large = exact + (jnp.log(nf / exact) / math.log(REL_MAX_DIST / exact) * (REL_BUCKETS - exact)).astype(I32)
    return jnp.where(n < exact, n, jnp.minimum(large, REL_BUCKETS - 1))


def _bucket_bits(bucket):
    return [((bucket >> i) & 1) == 1 for i in range(REL_BUCKETS.bit_length() - 1)]


def _bias_from_bits(bits, tab_ref, head):
    level = [tab_ref[b, head] for b in range(REL_BUCKETS)]
    for bit in bits:
        level = [jnp.where(bit, level[2 * i + 1], level[2 * i]) for i in range(len(level) // 2)]
    return level[0]


def _bias_from_bucket(bucket, tab_ref, head):
    return _bias_from_bits(_bucket_bits(bucket), tab_ref, head)


def _t5_tiles_kernel(tab_ref, o_ref, *, t, head0, window):
    h = pl.program_id(0)
    d = pl.program_id(1)
    dist = d * t + _iota((t, t), 1) - _iota((t, t), 0)
    val = _bias_from_bucket(_rel_bucket(dist), tab_ref, head0 + h)
    valid = dist >= 0
    if window:
        valid = valid & (dist <= WINDOW)
    o_ref[0, 0] = jnp.where(valid, val, NEG)


def t5_tiles(rel_bias, head0, nh, t, window):
    return pl.pallas_call(
        functools.partial(_t5_tiles_kernel, t=t, head0=head0, window=window),
        grid=(nh, 3),
        in_specs=[pl.BlockSpec(memory_space=pltpu.SMEM)],
        out_specs=pl.BlockSpec((1, 1, t, t), lambda h, d: (h, d, 0, 0)),
        out_shape=jax.ShapeDtypeStruct((nh, 3, t, t), F32),
        compiler_params=_cparams("parallel", "parallel"),
        name="t5_tiles",
    )(rel_bias)


def _pair_tables(nq, back):
    qi, kj, bt, fl = [], [], [], []
    for q in range(nq):
        lo = 0 if back is None else max(q - back, 0)
        for k in range(lo, q + 1):
            qi.append(q)
            kj.append(k)
            bt.append(min(q - k, 2))
            fl.append((1 if k == lo else 0) | (2 if k == q else 0))
    return tuple(jnp.asarray(np.asarray(a, np.int32)) for a in (qi, kj, bt, fl))


def _flash_kernel(qi_t, kj_t, bt_t, fl_t, q_ref, k_ref, v_ref, bias_ref, *rest, nrow, t, cw, use_sel, epi, lam_init):
    rest = list(rest)
    sel_ref = rest.pop(0) if use_sel else None
    m_sc, l_sc, acc_sc = rest[-3:]
    o_ref = rest[-4]
    extras = rest[:-4]
    p = pl.program_id(2)
    flags = fl_t[p]
    ncol = nrow * t

    @pl.when((flags & 1) != 0)
    def _():
        m_sc[...] = jnp.full_like(m_sc, NEG)
        l_sc[...] = jnp.zeros_like(l_sc)
        acc_sc[...] = jnp.zeros_like(acc_sc)

    k = k_ref[0, 0]
    vt = v_ref[0, 0]
    bt = bt_t[p]
    nbias = bias_ref.shape[2]
    if use_sel:
        ns = sel_ref.shape[1]
        blk = kj_t[p] * (t // SEL_BLOCK) + _iota((t, ns), 0) // SEL_BLOCK
        expand = (_iota((t, ns), 1) == blk).astype(BF16)
        chosen = _dot(expand, sel_ref[0].astype(BF16))
    m_all, l_all, acc_all = m_sc[...], l_sc[...], acc_sc[...]
    m_out, l_out, acc_out = [], [], []
    scores = [_dot(k, q_ref[0, 0, 0, :, c * cw:(c + 1) * cw]) for c in range(ncol // cw)]
    for c in range(ncol // cw):
        r, off = divmod(c * cw, t)
        cols = slice(c * cw, (c + 1) * cw)
        s = scores[c] + bias_ref[0, bt, r % nbias, :, off:off + cw]
        if use_sel:
            s = jnp.where(chosen[:, off:off + cw] > 0.5, s, NEG)
        m_prev = m_all[:, cols]
        m_new = jnp.maximum(m_prev, jnp.max(s, axis=0, keepdims=True))
        alpha = jnp.exp(m_prev - m_new)
        pr = jnp.exp(s - m_new)
        if use_sel:
            pr = jnp.where(s > 0.5 * NEG, pr, 0.0)
        l_out.append(alpha * l_all[:, cols] + jnp.sum(pr, axis=0, keepdims=True))
        acc_out.append(alpha * acc_all[:, cols] + _dot(vt, pr.astype(BF16)))
        m_out.append(m_new)
    m_sc[...] = jnp.concatenate(m_out, axis=1)
    l_sc[...] = jnp.concatenate(l_out, axis=1)
    acc_sc[...] = jnp.concatenate(acc_out, axis=1)

    @pl.when((flags & 2) != 0)
    def _():
        o = acc_sc[...] / l_sc[...]
        if epi == "plain":
            o_ref[0, 0, 0] = o
        elif epi == "diff":
            lam_ref, gn_ref = extras
            lv = lam_ref[...]
            lam = (jnp.exp(jnp.sum(lv[0:1] * lv[1:2], keepdims=True)) - jnp.exp(jnp.sum(lv[2:3] * lv[3:4], keepdims=True))
                   + lam_init)
            od = o[:, :t] - lam * o[:, t:]
            ms = jnp.mean(od * od, axis=0, keepdims=True)
            o_ref[0, 0, 0] = od * lax.rsqrt(ms + EPS) * gn_ref[...] * (1.0 - lam_init)
        else:
            gate_ref, oc_ref, os_ref = extras
            g = jax.nn.sigmoid(gate_ref[0, 0, 0])
            o_ref[0, 0, 0] = g[0:1] * oc_ref[0, 0, 0] + g[1:2] * os_ref[0, 0, 0] + g[2:3] * o


def flash(qt, k, vt, bias, tables, *, t, sel=None, epi="plain", extras=(), extra_specs=(), lam_init=None):
    bsz, hg, nq, dh, ncol = qt.shape
    nrow = ncol // t
    npairs = tables[0].shape[0]
    in_specs = [
        pl.BlockSpec((1, 1, 1, dh, ncol), lambda b, h, p, qi, kj, bt, fl: (b, h, qi[p], 0, 0)),
        pl.BlockSpec((1, 1, t, dh), lambda b, h, p, qi, kj, bt, fl: (b, h, kj[p], 0)),
        pl.BlockSpec((1, 1, dh, t), lambda b, h, p, qi, kj, bt, fl: (b, h, 0, kj[p])),
        pl.BlockSpec((1,) + bias.shape[1:], lambda b, h, p, qi, kj, bt, fl: (h, 0, 0, 0, 0)),
    ]
    args = [qt, k, vt, bias]
    if sel is not None:
        in_specs.append(pl.BlockSpec((1, sel.shape[1], t), lambda b, h, p, qi, kj, bt, fl: (b, 0, qi[p])))
        args.append(sel)
    in_specs += list(extra_specs)
    args += list(extras)
    n_out = t if epi == "diff" else ncol
    return pl.pallas_call(
        functools.partial(_flash_kernel, nrow=nrow, t=t, cw=min(t, FLASH_COLS), use_sel=sel is not None, epi=epi,
                          lam_init=lam_init),
        grid_spec=pltpu.PrefetchScalarGridSpec(
            num_scalar_prefetch=4,
            grid=(bsz, hg, npairs),
            in_specs=in_specs,
            out_specs=pl.BlockSpec((1, 1, 1, dh, n_out), lambda b, h, p, qi, kj, bt, fl: (b, h, qi[p], 0, 0)),
            scratch_shapes=[pltpu.VMEM((1, ncol), F32), pltpu.VMEM((1, ncol), F32), pltpu.VMEM((dh, ncol), F32)],
        ),
        out_shape=jax.ShapeDtypeStruct((bsz, hg, nq, dh, n_out), F32),
        compiler_params=_cparams("parallel", "parallel", "arbitrary"),
        name="flash_" + epi + ("_sel" if sel is not None else ""),
    )(*tables, *args)


def _to_tiles(x, t):
    b, g, r, tl, d = x.shape
    return x.reshape(b, g, r, tl // t, t, d).transpose(0, 1, 3, 5, 2, 4).reshape(b, g, tl // t, d, r * t)


def _from_tiles(x, r):
    b, g, nq, d, rt = x.shape
    t = rt // r
    return x.reshape(b, g, nq, d, r, t).transpose(0, 1, 4, 2, 5, 3).reshape(b, g, r, nq * t, d)


def _compress_kernel(pt_ref, *refs, npp):
    del pt_ref
    wt_ref = refs[npp]
    a_ref, b_ref = refs[npp + 1:]
    wt = wt_ref[...]
    for i in range(npp):
        x3 = refs[i][0].reshape(PAGE_SIZE // CMP_STRIDE, CMP_STRIDE, 2 * NSA_DH)
        a_ref[0, i * 8:(i + 1) * 8, :] = jnp.sum(x3 * wt[None, 0:CMP_STRIDE], axis=1)
        b_ref[0, i * 8:(i + 1) * 8, :] = jnp.sum(x3 * wt[None, CMP_STRIDE:], axis=1)


def compress(pool, pt, wt, npp):
    bsz, n_pages = pt.shape
    g = PAGE_SIZE // CMP_STRIDE
    page = lambda i: pl.BlockSpec((1, PAGE_SIZE, 2 * NSA_DH), lambda b, j, pt_ref: (pt_ref[b, j * npp + i], 0, 0))
    out = pl.BlockSpec((1, npp * g, 2 * NSA_DH), lambda b, j, pt_ref: (b, j, 0))
    shape = jax.ShapeDtypeStruct((bsz, n_pages * g, 2 * NSA_DH), F32)
    return pl.pallas_call(
        functools.partial(_compress_kernel, npp=npp),
        grid_spec=pltpu.PrefetchScalarGridSpec(
            num_scalar_prefetch=1,
            grid=(bsz, n_pages // npp),
            in_specs=[page(i) for i in range(npp)] + [pl.BlockSpec((CMP_BLOCK, 2 * NSA_DH), lambda b, j, pt_ref: (0, 0))],
            out_specs=[out, out],
        ),
        out_shape=[shape, shape],
        compiler_params=_cparams("parallel", "arbitrary"),
        name="nsa_compress",
    )(pt, *([pool] * npp), wt)


def _nsa_cmp_kernel(tab_ref, q_ref, a_ref, b_ref, tail_ref, o_ref, sel_ref, *, t, qpos0, n_cmp, n_sel):
    nc = a_ref.shape[1]
    ns = sel_ref.shape[-1]
    qi = pl.program_id(1)
    rown = _iota((nc, 2 * NSA_DH), 0)
    bsh = jnp.where(rown == nc - 1, tail_ref[0], pltpu.roll(b_ref[0], nc - 1, 0))
    kcv = jnp.where(rown < n_cmp, a_ref[0] + bsh, 0.0)
    vc = kcv[:, NSA_DH:].astype(BF16)
    kc_hi, kc_lo = _split2(kcv[:, :NSA_DH])
    start = _iota((nc, ns), 0) * CMP_STRIDE
    sblk = _iota((nc, ns), 1) * SEL_BLOCK
    ov = ((start < sblk + SEL_BLOCK) & (start + CMP_BLOCK > sblk)).astype(BF16)
    rc = max(8, min(t, CMP_CHUNK_ROWS))
    scores = []
    for c in range(t // rc):
        qpos = qpos0 + qi * t + c * rc + _iota((rc, nc), 0)
        n = _iota((rc, nc), 1)
        dist = qpos - (n * CMP_STRIDE + CMP_BLOCK - 1)
        valid = (dist >= 0) & (n < n_cmp)
        bits = _bucket_bits(_rel_bucket(dist))
        psum = jnp.zeros((rc, nc), F32)
        for h in range(NSA_H):
            q_hi, q_lo = _split2(q_ref[0, 0, h, c * rc:(c + 1) * rc, :])
            s = _dot_nt(q_hi, kc_hi) + _dot_nt(q_lo, kc_hi) + _dot_nt(q_hi, kc_lo)
            sh = jnp.where(valid, s + _bias_from_bits(bits, tab_ref, h), NEG)
            m = jnp.max(sh, axis=-1, keepdims=True)
            p = jnp.where(valid, jnp.exp(sh - m), 0.0)
            p = p / jnp.maximum(jnp.sum(p, axis=-1, keepdims=True), 1e-30)
            o_ref[0, 0, h, c * rc:(c + 1) * rc, :] = _dot(p.astype(BF16), vc)
            psum = psum + p
        hi, mid, lo = _split3(psum)
        scores.append(_dot(hi, ov) + _dot(mid, ov) + _dot(lo, ov))
    score = scores[0] if len(scores) == 1 else jnp.concatenate(scores, axis=0)
    j = _iota((t, ns), 1)
    cur = (qpos0 + qi * t + _iota((t, ns), 0)) // SEL_BLOCK
    forced = (j == 0) | (j == cur) | (j == cur - 1)
    sc = jnp.where(j <= cur, score + jnp.where(forced, FORCE_SCORE, 0.0), -1.0)
    sc = jnp.where(j < n_sel, sc, -jnp.inf)
    chosen = jnp.zeros((t, ns), F32)
    for _ in range(min(SEL_TOPK, n_sel)):
        m = jnp.max(sc, axis=-1, keepdims=True)
        idx = jnp.min(jnp.where(sc == m, j, ns), axis=-1, keepdims=True)
        hit = j == idx
        chosen = jnp.where(hit, 1.0, chosen)
        sc = jnp.where(hit, -jnp.inf, sc)
    sel_ref[0] = chosen


def nsa_cmp(q4, a, b, tail, rel_bias_nsa, *, t, qpos0, n_cmp, n_sel):
    bsz, _, _, tq, dh = q4.shape
    nc = a.shape[1]
    ns = -(-n_sel // 128) * 128
    return pl.pallas_call(
        functools.partial(_nsa_cmp_kernel, t=t, qpos0=qpos0, n_cmp=n_cmp, n_sel=n_sel),
        grid=(bsz, tq // t),
        in_specs=[
            pl.BlockSpec(memory_space=pltpu.SMEM),
            pl.BlockSpec((1, 1, NSA_H, t, dh), lambda b, i: (b, 0, 0, i, 0)),
            pl.BlockSpec((1, nc, 2 * dh), lambda b, i: (b, 0, 0)),
            pl.BlockSpec((1, nc, 2 * dh), lambda b, i: (b, 0, 0)),
            pl.BlockSpec((1, 1, 2 * dh), lambda b, i: (b, 0, 0)),
        ],
        out_specs=[pl.BlockSpec((1, 1, NSA_H, t, dh), lambda b, i: (b, 0, 0, i, 0)),
                   pl.BlockSpec((1, t, ns), lambda b, i: (b, i, 0))],
        out_shape=[jax.ShapeDtypeStruct((bsz, 1, NSA_H, tq, dh), F32), jax.ShapeDtypeStruct((bsz, tq, ns), F32)],
        compiler_params=_cparams("parallel", "parallel"),
        name="nsa_cmp",
    )(rel_bias_nsa, q4, a, b, tail)


def _cmp_weight_tile(w_cmp):
    return jnp.repeat(w_cmp.T, NSA_DH, axis=1)


def nsa_prompt(qn, kvn, n_g, w_cmp, Wc):
    bsz, t_len, _ = qn.shape
    t = _pick_tile(t_len, 256)
    q4f = (qn.reshape(bsz, t_len, NSA_H, NSA_DH) * NSA_DH ** -0.5).transpose(0, 2, 1, 3)[:, None]
    q4 = q4f.astype(BF16)
    n_pages = t_len // PAGE_SIZE
    pool = kvn[:, :, 0:2 * NSA_DH].reshape(bsz * n_pages, PAGE_SIZE, 2 * NSA_DH)
    pt = jnp.arange(bsz * n_pages, dtype=I32).reshape(bsz, n_pages)
    a, b = compress(pool, pt, _cmp_weight_tile(w_cmp), _pick_tile(n_pages, 16))
    o_c, chosen = nsa_cmp(q4f, a, b, jnp.zeros((bsz, 1, 2 * NSA_DH), F32), Wc['rel_nsa'], t=t, qpos0=0,
                          n_cmp=t_len // CMP_STRIDE - 1, n_sel=t_len // SEL_BLOCK)
    t = _pick_tile(t_len, NSA_FLASH_TILE)
    kv = lambda i: kvn[:, :, i * NSA_DH:(i + 1) * NSA_DH].astype(BF16)[:, None]
    kvt = lambda i: kv(i).transpose(0, 1, 3, 2)
    nq = t_len // t
    qt = _to_tiles(q4, t)
    o_s = flash(qt, kv(2), kvt(3), Wc['tiles_nsa'], _pair_tables(nq, None), t=t, sel=chosen.transpose(0, 2, 1))
    gates = n_g.reshape(bsz, nq, t, NSA_H, 3).transpose(0, 1, 4, 3, 2).reshape(bsz, 1, nq, 3, NSA_H * t)
    tile = lambda w: pl.BlockSpec((1, 1, 1, w, NSA_H * t), lambda b, h, p, qi, kj, bt, fl: (b, h, qi[p], 0, 0))
    o = flash(qt, kv(4), kvt(5), Wc['tiles_win'], _pair_tables(nq, WINDOW // t), t=t, epi="win",
              extras=(gates, _to_tiles(o_c, t), o_s), extra_specs=(tile(3), tile(NSA_DH), tile(NSA_DH)))
    return _from_tiles(o, NSA_H)[:, 0].transpose(0, 2, 1, 3).reshape(bsz, t_len, NSA_H * NSA_DH)


def _lam_init(l):
    return 0.8 - 0.6 * math.exp(-0.3 * l)


def diff_prompt(d_q, d_k, d_v, lam_rows, lam_init, gn, Wc):
    bsz, t_len, _ = d_q.shape
    t = _pick_tile(t_len, DIFF_FLASH_TILE)
    q = d_q.reshape(bsz, t_len, DF_H, 2 * DF_D).transpose(0, 2, 1, 3) * DF_D ** -0.5
    lane = jnp.arange(2 * DF_D) < DF_D
    q2 = jnp.stack([jnp.where(lane, q, 0.0), jnp.where(lane, 0.0, q)], axis=2).astype(BF16)
    k = d_k.reshape(bsz, t_len, DF_H, 2 * DF_D).transpose(0, 2, 1, 3).astype(BF16)
    vt = d_v.reshape(bsz, t_len, DF_H, DF_DV).transpose(0, 2, 3, 1).astype(BF16)
    full = lambda s: pl.BlockSpec(s, lambda b, h, p, qi, kj, bt, fl: (0,) * len(s))
    o = flash(_to_tiles(q2, t), k, vt, Wc['tiles_diff'], _pair_tables(t_len // t, None), t=t, epi="diff",
              lam_init=lam_init, extras=(lam_rows, gn.reshape(DF_DV, 1)),
              extra_specs=(full((4, DF_D)), full((DF_DV, 1))))
    return _from_tiles(o, 1)[:, :, 0].transpose(0, 2, 1, 3).reshape(bsz, t_len, BR_WIDTH)


NEW_PAD = 16


def _paged_attn_kernel(pt_ref, tab_ref, q_ref, *refs, npp, head_cols, kpos0, qpos0, t_new, window, use_sel):
    del pt_ref
    pages = refs[:npp]
    newk_ref, newv_ref = refs[npp], refs[npp + 1]
    sel_ref = refs[npp + 2] if use_sel else None
    o_ref, m_sc, l_sc, acc_sc = refs[-4:]
    j = pl.program_id(1)
    tq = t_new
    ng = len(head_cols)
    nrow = ng * tq
    ks = npp * PAGE_SIZE
    hkv, rh = q_ref.shape[1], q_ref.shape[2]
    hrows = lambda x, h: x[h * rh:(h + 1) * rh]

    @pl.when(j == 0)
    def _():
        m_sc[...] = jnp.full_like(m_sc, NEG)
        l_sc[...] = jnp.zeros_like(l_sc)
        acc_sc[...] = jnp.zeros_like(acc_sc)

    qs = [q_ref[0, h] for h in range(hkv)]

    def update(s, kpos, extra_valid, blocks, pv):
        n = s.shape[-1]
        dist = (qpos0 + _iota((tq, n), 0)) - kpos
        valid = dist >= 0
        if window:
            valid = valid & (dist <= WINDOW)
        if extra_valid is not None:
            valid = valid & extra_valid
        if use_sel:
            nsb = sel_ref.shape[-1]
            expand = (_iota((nsb, n), 0) == blocks).astype(BF16)
            valid = valid & (_dot(sel_ref[0].astype(BF16), expand) > 0.5)
        bits = _bucket_bits(_rel_bucket(dist))
        bias = {c: _bias_from_bits(bits, tab_ref, c) for c in sorted(set(head_cols))}
        bias3 = jnp.stack([bias[c] for c in head_cols])
        s = jnp.where(valid[None], s.reshape(ng, tq, n) + bias3, NEG).reshape(nrow, n)
        m_prev = m_sc[...]
        m_new = jnp.maximum(m_prev, jnp.max(s, axis=-1, keepdims=True))
        alpha = jnp.exp(m_prev - m_new)
        pr = jnp.where(s > 0.5 * NEG, jnp.exp(s - m_new), 0.0)
        l_sc[...] = alpha * l_sc[...] + jnp.sum(pr, axis=-1, keepdims=True)
        acc_sc[...] = alpha * acc_sc[...] + pv(pr.astype(BF16))
        m_sc[...] = m_new

    cat = lambda xs, axis: xs[0] if len(xs) == 1 else jnp.concatenate(xs, axis=axis)
    kt = [[pages[i][0, 0, 0, h].astype(BF16) for h in range(hkv)] for i in range(npp)]
    vt = [[pages[i][0, 0, 1, h].astype(BF16) for h in range(hkv)] for i in range(npp)]
    s = cat([cat([_dot(qs[h], kt[i][h]) for i in range(npp)], 1) for h in range(hkv)], 0)
    kpos = kpos0 + j * ks + _iota((tq, ks), 1)
    blocks = (kpos0 + j * ks + _iota((1, ks), 1)) // SEL_BLOCK

    def pv_pages(pb):
        outs = []
        for h in range(hkv):
            ph = hrows(pb, h)
            out = _dot_nt(ph[:, 0:PAGE_SIZE], vt[0][h])
            for i in range(1, npp):
                out = out + _dot_nt(ph[:, i * PAGE_SIZE:(i + 1) * PAGE_SIZE], vt[i][h])
            outs.append(out)
        return cat(outs, 0)

    update(s, kpos, None, blocks, pv_pages)

    @pl.when(j == pl.num_programs(1) - 1)
    def _():
        nk = [newk_ref[0, h].astype(BF16) for h in range(hkv)]
        nv = [newv_ref[0, h].astype(BF16) for h in range(hkv)]
        col = _iota((tq, NEW_PAD), 1)
        update(cat([_dot_nt(qs[h], nk[h]) for h in range(hkv)], 0), qpos0 + col, col < t_new,
               (qpos0 + _iota((1, NEW_PAD), 1)) // SEL_BLOCK,
               lambda pb: cat([_dot(hrows(pb, h), nv[h]) for h in range(hkv)], 0))
        o_ref[0] = acc_sc[...] / l_sc[...]


def paged_attn(q, pool, layer, pt, page_index, newk, newv, tab, *, npp, head_cols, kpos0, qpos0, window=False,
               sel=None):
    bsz, hkv, rh, dh = q.shape
    nrow = hkv * rh
    lw = dh
    n_pages = pt.shape[1]
    t_new = nrow // len(head_cols)
    page = lambda i: pl.BlockSpec((1, 1, 2, hkv, dh, PAGE_SIZE),
                                  lambda b, j, pt_ref: (layer,) + page_index(b, j * npp + i, pt_ref))
    new_spec = pl.BlockSpec((1, hkv, NEW_PAD, dh), lambda b, j, pt_ref: (b, 0, 0, 0))
    in_specs = [pl.BlockSpec(memory_space=pltpu.SMEM), pl.BlockSpec((1, hkv, rh, dh), lambda b, j, pt_ref: (b, 0, 0, 0))]
    in_specs += [page(i) for i in range(npp)]
    in_specs += [new_spec, new_spec]
    args = [tab, q] + [pool] * npp + [newk, newv]
    if sel is not None:
        in_specs.append(pl.BlockSpec((1, t_new, sel.shape[-1]), lambda b, j, pt_ref: (b, 0, 0)))
        args.append(sel)
    return pl.pallas_call(
        functools.partial(_paged_attn_kernel, npp=npp, head_cols=tuple(head_cols), kpos0=kpos0, qpos0=qpos0,
                          t_new=t_new, window=window, use_sel=sel is not None),
        grid_spec=pltpu.PrefetchScalarGridSpec(
            num_scalar_prefetch=1,
            grid=(bsz, n_pages // npp),
            in_specs=in_specs,
            out_specs=pl.BlockSpec((1, nrow, lw), lambda b, j, pt_ref: (b, 0, 0)),
            scratch_shapes=[pltpu.VMEM((nrow, 1), F32), pltpu.VMEM((nrow, 1), F32), pltpu.VMEM((nrow, lw), F32)],
        ),
        out_shape=jax.ShapeDtypeStruct((bsz, nrow, lw), F32),
        compiler_params=_cparams("parallel", "arbitrary"),
        name="paged_attn",
    )(pt, *args)


def _nsa_combine_kernel(g_ref, oc_ref, os_ref, ow_ref, o_ref):
    g = jax.nn.sigmoid(g_ref[...])
    o_ref[...] = g[..., 0:1] * oc_ref[...] + g[..., 1:2] * os_ref[...] + g[..., 2:3] * ow_ref[...]


def nsa_combine(gates, o_c, o_s, o_w):
    n, dh = o_c.shape
    full = lambda w: pl.BlockSpec((n, w), lambda i: (0, 0))
    return pl.pallas_call(
        _nsa_combine_kernel, grid=(1,),
        in_specs=[full(3), full(dh), full(dh), full(dh)], out_specs=full(dh),
        out_shape=jax.ShapeDtypeStruct((n, dh), F32), name="nsa_combine",
    )(gates, o_c, o_s, o_w)


def _pad_rows(x, n):
    return jnp.pad(x, ((0, 0), (0, n - x.shape[1]), (0, 0)))


def _compress_t_kernel(pt_ref, *refs, npp):
    del pt_ref
    pages = refs[:npp]
    whi_ref, wlo_ref = refs[npp], refs[npp + 1]
    a_ref, b_ref = refs[npp + 2:]
    for kv in range(2):
        acc = None
        for i in range(npp):
            xh, xl = _split2(pages[i][0, 0, kv, 0])
            y = _dot(xh, whi_ref[kv, i]) + _dot(xl, whi_ref[kv, i]) + _dot(xh, wlo_ref[kv, i])
            acc = y if acc is None else acc + y
        a_ref[0, kv] = acc[:, :128]
        b_ref[0, kv] = acc[:, 128:]


def compress_t(pool, layer, pt, w_cmp):
    npp = 16
    bsz, n_pages = pt.shape
    g = PAGE_SIZE // CMP_STRIDE
    r = np.arange(PAGE_SIZE)
    grp = jnp.asarray((r[:, None] // CMP_STRIDE == np.arange(g)[None, :]).astype(np.float32))
    slot = jnp.eye(npp, dtype=F32)
    halves = []
    for half in range(2):
        wr = w_cmp[:, half * CMP_STRIDE + r % CMP_STRIDE]
        halves.append(jnp.einsum('kr,rg,ip->kirpg', wr, grp, slot).reshape(2, npp, PAGE_SIZE, npp * g))
    wbig = jnp.concatenate(halves, axis=-1)
    whi = wbig.astype(BF16)
    wlo = (wbig - whi.astype(F32)).astype(BF16)
    dh = pool.shape[-2]
    page = lambda i: pl.BlockSpec((1, 1, 2, 1, dh, PAGE_SIZE),
                                  lambda b, j, pt_ref: (layer, pt_ref[b, j * npp + i], 0, 0, 0, 0))
    wspec = pl.BlockSpec((2, npp, PAGE_SIZE, 2 * npp * g), lambda b, j, pt_ref: (0, 0, 0, 0))
    out = pl.BlockSpec((1, 2, dh, npp * g), lambda b, j, pt_ref: (b, 0, 0, j))
    shape = jax.ShapeDtypeStruct((bsz, 2, dh, n_pages * g), F32)
    return pl.pallas_call(
        functools.partial(_compress_t_kernel, npp=npp),
        grid_spec=pltpu.PrefetchScalarGridSpec(
            num_scalar_prefetch=1,
            grid=(bsz, n_pages // npp),
            in_specs=[page(i) for i in range(npp)] + [wspec, wspec],
            out_specs=[out, out],
        ),
        out_shape=[shape, shape],
        compiler_params=_cparams("parallel", "arbitrary"),
        name="nsa_compress_t",
    )(pt, *([pool] * npp), whi, wlo)


def _paged_index(b, page, pt_ref):
    return (pt_ref[b, page], 0, 0, 0, 0)


def _window_index(b, page, pt_ref):
    return (b, 0, 0, 0, page)


def nsa_sample(qn, kvn, n_g, w_cmp, Wc, layer, pool_cmp, pool_sel, page_table, win_t):
    bsz, t_len, _ = qn.shape
    lw = 2 * NSA_DH
    total = PAST_LEN + t_len
    n_grp = -(-total // CMP_STRIDE)
    n_cmp = n_grp - CMP_BLOCK // CMP_STRIDE + 1
    n_sel = -(-total // SEL_BLOCK)
    qf = (qn.reshape(bsz, t_len, NSA_H, NSA_DH) * NSA_DH ** -0.5).transpose(0, 2, 1, 3)
    q1 = qf.astype(BF16).reshape(bsz, 1, NSA_H * t_len, NSA_DH)
    at, bt = compress_t(pool_cmp, layer, page_table, w_cmp)
    rows = lambda x: x.transpose(0, 3, 1, 2).reshape(bsz, x.shape[-1], lw)
    new_page = _pad_rows(kvn[:, :, 0:lw], PAGE_SIZE)
    _, b_new = compress(new_page, jnp.arange(bsz, dtype=I32).reshape(bsz, 1), _cmp_weight_tile(w_cmp), 1)
    o_c, chosen = nsa_cmp(qf[:, None], rows(at), rows(bt), b_new[:, 0:1], Wc['rel_nsa'], t=t_len, qpos0=PAST_LEN,
                          n_cmp=n_cmp, n_sel=n_sel)
    new = lambda i: _pad_rows(kvn[:, :, i * NSA_DH:(i + 1) * NSA_DH], NEW_PAD)[:, None]
    heads = tuple(range(NSA_H))
    o_s = paged_attn(q1, pool_sel, layer, page_table, _paged_index, new(2), new(3), Wc['rel_nsa'], npp=16,
                     head_cols=heads, kpos0=0, qpos0=PAST_LEN, sel=chosen)
    wb = win_t.shape[-1]
    wpages = wb // PAGE_SIZE
    o_w = paged_attn(q1, win_t, layer, jnp.zeros((bsz, wpages), I32), _window_index, new(4), new(5), Wc['rel_nsa'],
                     npp=wpages, head_cols=heads, kpos0=PAST_LEN - wb, qpos0=PAST_LEN, window=True)
    n = bsz * NSA_H * t_len
    gates = n_g.reshape(bsz, t_len, NSA_H, 3).transpose(0, 2, 1, 3).reshape(n, 3)
    o = nsa_combine(gates, o_c.reshape(n, NSA_DH), o_s.reshape(n, NSA_DH), o_w.reshape(n, NSA_DH))
    return o.reshape(bsz, NSA_H, t_len, NSA_DH).transpose(0, 2, 1, 3).reshape(bsz, t_len, NSA_H * NSA_DH)


def _diff_post_kernel(o_ref, lam_ref, gn_ref, out_ref, *, lam_init):
    lv = lam_ref[...]
    lam = (jnp.exp(jnp.sum(lv[0:1] * lv[1:2], keepdims=True)) - jnp.exp(jnp.sum(lv[2:3] * lv[3:4], keepdims=True))
           + lam_init)
    od = o_ref[0] - lam * o_ref[1]
    out_ref[...] = _rms_rows(od, gn_ref[...]) * (1.0 - lam_init)


def diff_post(o2, lam_rows, lam_init, gn):
    _, n, dv = o2.shape
    return pl.pallas_call(
        functools.partial(_diff_post_kernel, lam_init=lam_init), grid=(1,),
        in_specs=[pl.BlockSpec((2, n, dv), lambda i: (0, 0, 0)), pl.BlockSpec((4, DF_D), lambda i: (0, 0)),
                  pl.BlockSpec((1, dv), lambda i: (0, 0))],
        out_specs=pl.BlockSpec((n, dv), lambda i: (0, 0)),
        out_shape=jax.ShapeDtypeStruct((n, dv), F32), name="diff_post",
    )(o2, lam_rows, gn.reshape(1, dv))


def diff_sample(d_q, d_k, d_v, lam_rows, lam_init, gn, Wc, layer, pool, page_table):
    bsz, t_len, _ = d_q.shape
    q = (d_q.reshape(bsz, t_len, DF_H, 2 * DF_D) * DF_D ** -0.5).transpose(0, 2, 1, 3)
    lane = jnp.arange(2 * DF_D) < DF_D
    q2 = jnp.stack([jnp.where(lane, q, 0.0), jnp.where(lane, 0.0, q)], axis=2).astype(BF16)
    q2 = q2.reshape(bsz, DF_H, 2 * t_len, 2 * DF_D)
    new = lambda x: _pad_rows(x, NEW_PAD).reshape(bsz, NEW_PAD, DF_H, DF_DV).transpose(0, 2, 1, 3)
    head_cols = tuple(NSA_H + h for h in range(DF_H) for _ in range(2))
    o = paged_attn(q2, pool, layer, page_table, _paged_index, new(d_k), new(d_v), Wc['rel_all'], npp=16,
                   head_cols=head_cols, kpos0=0, qpos0=PAST_LEN)
    n = bsz * DF_H * t_len
    o2 = o.reshape(bsz, DF_H, 2, t_len, DF_DV).transpose(2, 0, 1, 3, 4).reshape(2, n, DF_DV)
    od = diff_post(o2, lam_rows, lam_init, gn).reshape(bsz, DF_H, t_len, DF_DV)
    return od.transpose(0, 2, 1, 3).reshape(bsz, t_len, BR_WIDTH)


def prepare_weights(W):
    Wc = {}
    w_in = W['w_in']
    Wc['w_main'] = jnp.concatenate(
        [jnp.pad(w_in[:, :, IN_OFFS[g]:IN_OFFS[g + 1]], ((0, 0), (0, 0), (0, GROUP_PADS[g] - IN_WIDTHS[g])))
         for g in range(len(GROUP_PADS))], axis=-1).astype(BF16)
    Wc['w_gate'] = w_in[:, :, MAIN_COLS:].reshape(DEPTH, D_MODEL, N_BRANCH, D_MODEL).transpose(0, 2, 1, 3).astype(BF16)
    for name in ('w_br', 'w_out', 'ffn_w1', 'ffn_w3', 'ffn_w2', 'moe_w1', 'moe_w3', 'moe_w2'):
        Wc[name] = W[name].astype(BF16)
    Wc['moe_router'] = jnp.pad(W['moe_router'], ((0, 0), (0, 0), (0, 128 - N_EXPERTS)))
    rel = W['rel_bias']
    Wc['rel_nsa'] = rel[:, :NSA_H]
    Wc['rel_all'] = rel
    Wc['tiles_nsa'] = t5_tiles(rel, 0, NSA_H, NSA_FLASH_TILE, False).transpose(1, 0, 2, 3)[None]
    Wc['tiles_win'] = t5_tiles(rel, 0, NSA_H, NSA_FLASH_TILE, True).transpose(1, 0, 2, 3)[None]
    Wc['tiles_diff'] = t5_tiles(rel, NSA_H, DF_H, DIFF_FLASH_TILE, False)[:, :, None]
    return Wc


def token_mix(h, l, W, Wc, past, page_table):
    B, T, _ = h.shape
    m = B * T
    nb_state = 2 if past is None else 8
    proj = in_proj(h.reshape(m, D_MODEL), Wc['w_main'][l])
    g_q, g_k, g_v, g_a, g_r, n_q, n_kv, n_g, d_q, d_k, d_v, rw = [p.reshape(B, T, p.shape[-1]) for p in proj]
    n_g = n_g[:, :, :IN_WIDTHS[7]]
    st = {}
    s0 = jnp.zeros((B, GLA_H * GLA_DK, GLA_H * GLA_DV), F32) if past is None else gla_state_to_bd(past['gla'][l])
    o_a, s_bd = gla_mixer(g_q, g_k, g_v, g_a, g_r, W['gla_wa2'][l], W['gla_ba'][l], W['gla_norm_g'][l], s0, nb_state)
    st['gla'] = gla_state_from_bd(s_bd)
    kvn = n_kv.reshape(B, T, 6, NSA_DH)
    if past is None:
        o_b = nsa_prompt(n_q, n_kv, n_g, W['nsa_cmp_w'][l], Wc)
        st['win'] = kvn[:, T - min(WINDOW, T):, 4:6]
    else:
        o_b = nsa_sample(n_q, n_kv, n_g, W['nsa_cmp_w'][l], Wc, l, past['cmp_t'], past['sel_t'], page_table,
                         past['win_t'])
        st['win'] = jnp.concatenate([past['win'][l], kvn[:, :, 4:6]], axis=1)[:, T:]
    st['cmp'] = kvn[:, :, 0:2]
    st['sel'] = kvn[:, :, 2:4]
    prev = jnp.zeros((B, RW_PROJ), F32) if past is None else past['shift'][l]
    s0 = jnp.zeros((B, RW_H, RW_N, RW_N), F32) if past is None else past['rwkv'][l]
    o_c, st['rwkv'], st['shift'] = rwkv_mixer(
        rw, prev, s0, W['rw_mu'][l], W['rw_w0'][l], W['rw_w2'][l], W['rw_a0'][l], W['rw_a2'][l], W['rw_g2'][l],
        W['rw_kk'][l], W['rw_ka'][l], W['rw_rk'][l], W['rw_norm_g'][l], nb_state)
    lam_init = _lam_init(l)
    if past is None:
        o_d = diff_prompt(d_q, d_k, d_v, W['df_lam'][l], lam_init, W['df_norm_g'][l], Wc)
    else:
        o_d = diff_sample(d_q, d_k, d_v, W['df_lam'][l], lam_init, W['df_norm_g'][l], Wc, l, past['diff_t'],
                          page_table)
    st['diff'] = jnp.stack([d_k.reshape(B, T, DF_H, 2 * DF_D), d_v.reshape(B, T, DF_H, DF_DV)], axis=2)
    return [t.reshape(m, BR_WIDTH) for t in (o_a, o_b, o_c, o_d)], st


def trunk(x, W, Wc, cache, page_table):
    B, T, _ = x.shape
    x2 = x.reshape(B * T, D_MODEL)
    new = {}
    for l in range(DEPTH):
        past = cache
        h = rmsnorm(x2, W['norm1_g'][l], BF16)
        brs, st = token_mix(h.reshape(B, T, D_MODEL), l, W, Wc, past, page_table)
        x2 = merge(h, brs, x2, Wc['w_gate'][l], Wc['w_br'][l], Wc['w_out'][l])
        j = l // 2
        if l % 2 == 0:
            x2 = ffn(x2, W['norm2_g'][l], Wc['ffn_w1'][j], Wc['ffn_w3'][j], Wc['ffn_w2'][j])
        else:
            x2 = moe(x2, W['norm2_g'][l], Wc['moe_router'][j], Wc['moe_w1'][j], Wc['moe_w3'][j], Wc['moe_w2'][j])
        for name, arr in st.items():
            new.setdefault(name, []).append(arr)
    y = rmsnorm(x2, W['final_norm_g'], F32).reshape(B, T, D_MODEL)
    return y, {name: jnp.stack(arrs) for name, arrs in new.items()}


def kernel(x_prompt, x_sample, cache_nsa_cmp, cache_nsa_sel, cache_diff, state_nsa_win, state_gla, state_rwkv, state_rwkv_shift, page_table, norm1_g, norm2_g, final_norm_g, w_in, gla_wa2, gla_ba, gla_norm_g, nsa_cmp_w, rw_mu, rw_w0, rw_w2, rw_a0, rw_a2, rw_g2, rw_kk, rw_ka, rw_rk, rw_norm_g, df_lam, df_norm_g, w_br, w_out, rel_bias, ffn_w1, ffn_w3, ffn_w2, moe_router, moe_w1, moe_w3, moe_w2):
    W = dict(norm1_g=norm1_g, norm2_g=norm2_g, final_norm_g=final_norm_g, w_in=w_in, gla_wa2=gla_wa2,
             gla_ba=gla_ba, gla_norm_g=gla_norm_g, nsa_cmp_w=nsa_cmp_w, rw_mu=rw_mu, rw_w0=rw_w0, rw_w2=rw_w2,
             rw_a0=rw_a0, rw_a2=rw_a2, rw_g2=rw_g2, rw_kk=rw_kk, rw_ka=rw_ka, rw_rk=rw_rk, rw_norm_g=rw_norm_g,
             df_lam=df_lam, df_norm_g=df_norm_g, w_br=w_br, w_out=w_out, rel_bias=rel_bias, ffn_w1=ffn_w1,
             ffn_w3=ffn_w3, ffn_w2=ffn_w2, moe_router=moe_router, moe_w1=moe_w1, moe_w3=moe_w3, moe_w2=moe_w2)
    rows_last = lambda a: jnp.moveaxis(a, 2, -1)
    cache = dict(cmp_t=rows_last(cache_nsa_cmp)[:, :, :, None], sel_t=rows_last(cache_nsa_sel)[:, :, :, None],
                 diff_t=rows_last(cache_diff), win_t=rows_last(state_nsa_win)[:, :, :, None], win=state_nsa_win,
                 gla=state_gla, rwkv=state_rwkv, shift=state_rwkv_shift)
    Wc = prepare_weights(W)
    y_prompt, sp = trunk(x_prompt, W, Wc, None, None)
    y_sample, ss = trunk(x_sample, W, Wc, cache, page_table)
    return (y_prompt, y_sample,
            sp['cmp'], sp['sel'], sp['diff'], sp['win'], sp['gla'], sp['rwkv'], sp['shift'],
            ss['cmp'], ss['sel'], ss['diff'], ss['win'], ss['gla'], ss['rwkv'], ss['shift'])
```

```python
import functools
import math

import numpy as np
import jax
import jax.numpy as jnp
from jax import lax
from jax.experimental import pallas as pl
from jax.experimental.pallas import tpu as pltpu

F32 = jnp.float32
BF16 = jnp.bfloat16
I32 = jnp.int32

D_MODEL = 1024
DEPTH = 2
PAST_LEN = 16384
PAGE_SIZE = 128
N_BRANCH = 4
BR_WIDTH = 256
GLA_H, GLA_DK, GLA_DV, GLA_RANK = 4, 32, 64, 16
GLA_TAU = 16.0
GLA_CHUNK = 64
NSA_H, NSA_DH = 4, 64
CMP_BLOCK, CMP_STRIDE, SEL_BLOCK, SEL_TOPK, WINDOW = 32, 16, 64, 16, 512
FORCE_SCORE = 1.0e4
RW_H, RW_N = 4, 64
RW_PROJ = 1024
DF_H, DF_D, DF_DV = 4, 32, 64
REL_BUCKETS, REL_MAX_DIST = 32, 128
N_EXPERTS, TOP_K = 8, 2
Q_BLOCK = 128
EPS = 1e-6
NEG = -1e30

IN_WIDTHS = (128, 128, 256, 16, 256, 256, 384, 12, 256, 256, 256, RW_PROJ, N_BRANCH * D_MODEL)
IN_OFFS = tuple(int(s) for s in np.cumsum((0,) + IN_WIDTHS))
MAIN_COLS = IN_OFFS[12]
LANES = 128
GROUP_PADS = tuple(-(-w // LANES) * LANES for w in IN_WIDTHS[:12])
GROUP_OFFS = tuple(int(s) for s in np.cumsum((0,) + GROUP_PADS))
MAIN_PAD = GROUP_OFFS[-1]

VMEM_LIMIT_BYTES = 56 * 1024 * 1024
CMP_CHUNK_ROWS = 256
FLASH_COLS = 512
NSA_FLASH_TILE = 512
DIFF_FLASH_TILE = 1024
RW_LOOKAHEAD = 2


def _cparams(*sem):
    return pltpu.CompilerParams(dimension_semantics=sem, vmem_limit_bytes=VMEM_LIMIT_BYTES)


def _dot(a, b):
    return jnp.dot(a, b, preferred_element_type=F32)


def _dot_nt(a, b):
    return lax.dot_general(a, b, (((1,), (1,)), ((), ())), preferred_element_type=F32)


def _dot_tn(a, b):
    return lax.dot_general(a, b, (((0,), (0,)), ((), ())), preferred_element_type=F32)


def _split2(x):
    hi = x.astype(BF16)
    lo = (x - hi.astype(F32)).astype(BF16)
    return hi, lo


def _split3(x):
    hi = x.astype(BF16)
    r = x - hi.astype(F32)
    mid = r.astype(BF16)
    lo = (r - mid.astype(F32)).astype(BF16)
    return hi, mid, lo


def _dot_x2(x, e):
    hi, lo = _split2(x)
    return _dot(hi, e) + _dot(lo, e)


def _dot_e3(e, x):
    hi, mid, lo = _split3(x)
    return _dot(e, hi) + _dot(e, mid) + _dot(e, lo)


def _dot_3x(a, b, dot=_dot):
    ah, al = _split2(a)
    bh, bl = _split2(b)
    return dot(ah, bh) + dot(al, bh) + dot(ah, bl)


def _iota(shape, dim):
    return lax.broadcasted_iota(I32, shape, dim)


def _block_ones(n, seg):
    r = _iota((n, n), 0) // seg
    c = _iota((n, n), 1) // seg
    return (r == c).astype(BF16)


def _rms_rows(x, g):
    ms = jnp.mean(x * x, axis=-1, keepdims=True)
    return x * lax.rsqrt(ms + EPS) * g


def _log_sigmoid(x):
    return -(jnp.maximum(-x, 0.0) + jnp.log1p(jnp.exp(-jnp.abs(x))))


def _pick_tile(n, pref):
    t = min(n, pref)
    while n % t:
        t //= 2
    return t


def _rmsnorm_kernel(x_ref, g_ref, o_ref):
    o_ref[...] = _rms_rows(x_ref[...], g_ref[...]).astype(o_ref.dtype)


def rmsnorm(x, g, out_dtype):
    m, d = x.shape
    tm = _pick_tile(m, 512)
    return pl.pallas_call(
        _rmsnorm_kernel,
        grid=(m // tm,),
        in_specs=[pl.BlockSpec((tm, d), lambda i: (i, 0)), pl.BlockSpec((1, d), lambda i: (0, 0))],
        out_specs=pl.BlockSpec((tm, d), lambda i: (i, 0)),
        out_shape=jax.ShapeDtypeStruct((m, d), out_dtype),
        compiler_params=_cparams("parallel"),
        name="rmsnorm",
    )(x, g.reshape(1, d))


def _in_proj_kernel(a_ref, b_ref, *o_refs):
    a = a_ref[...]
    for g, o_ref in enumerate(o_refs):
        o_ref[...] = _dot(a, b_ref[:, GROUP_OFFS[g]:GROUP_OFFS[g + 1]])


def in_proj(a, b):
    m, k = a.shape
    tm = _pick_tile(m, 512)
    return pl.pallas_call(
        _in_proj_kernel,
        grid=(m // tm,),
        in_specs=[pl.BlockSpec((tm, k), lambda i: (i, 0)), pl.BlockSpec((k, MAIN_PAD), lambda i: (0, 0))],
        out_specs=[pl.BlockSpec((tm, w), lambda i: (i, 0)) for w in GROUP_PADS],
        out_shape=[jax.ShapeDtypeStruct((m, w), F32) for w in GROUP_PADS],
        compiler_params=_cparams("parallel"),
        name="in_proj",
    )(a, b)


def _merge_kernel(h_ref, a_ref, b_ref, c_ref, d_ref, x_ref, wg_ref, wbr_ref, wout_ref, o_ref):
    h = h_ref[...]
    acc = None
    for n, br_ref in enumerate((a_ref, b_ref, c_ref, d_ref)):
        gate = jax.nn.sigmoid(_dot(h, wg_ref[n]))
        up = _dot(br_ref[...].astype(BF16), wbr_ref[n])
        acc = gate * up if acc is None else acc + gate * up
    o_ref[...] = x_ref[...] + _dot(acc.astype(BF16), wout_ref[...])


def merge(h, brs, x, wg, wbr, wout):
    m, d = x.shape
    tm = _pick_tile(m, 512)
    row = lambda i: (i, 0)
    return pl.pallas_call(
        _merge_kernel,
        grid=(m // tm,),
        in_specs=[pl.BlockSpec((tm, d), row)] + [pl.BlockSpec((tm, BR_WIDTH), row)] * 4 + [
            pl.BlockSpec((tm, d), row),
            pl.BlockSpec((N_BRANCH, d, d), lambda i: (0, 0, 0)),
            pl.BlockSpec((N_BRANCH, BR_WIDTH, d), lambda i: (0, 0, 0)),
            pl.BlockSpec((d, d), lambda i: (0, 0)),
        ],
        out_specs=pl.BlockSpec((tm, d), row),
        out_shape=jax.ShapeDtypeStruct((m, d), F32),
        compiler_params=_cparams("parallel"),
        name="merge",
    )(h, *brs, x, wg, wbr, wout)


def _ffn_kernel(x_ref, g_ref, w1_ref, w3_ref, w2_ref, o_ref, h_sc, acc_sc):
    j = pl.program_id(1)

    @pl.when(j == 0)
    def _():
        h_sc[...] = _rms_rows(x_ref[...], g_ref[...]).astype(BF16)
        acc_sc[...] = jnp.zeros_like(acc_sc)

    h = h_sc[...]
    a = _dot(h, w1_ref[...])
    b = _dot(h, w3_ref[...])
    t = (a * jax.nn.sigmoid(a)) * b
    acc_sc[...] += _dot(t.astype(BF16), w2_ref[...])

    @pl.when(j == pl.num_programs(1) - 1)
    def _():
        o_ref[...] = x_ref[...] + acc_sc[...]


def ffn(x, g, w1, w3, w2):
    m, d = x.shape
    ff = w1.shape[1]
    tm = _pick_tile(m, 1024)
    tf = 256
    return pl.pallas_call(
        _ffn_kernel,
        grid=(m // tm, ff // tf),
        in_specs=[
            pl.BlockSpec((tm, d), lambda i, j: (i, 0)),
            pl.BlockSpec((1, d), lambda i, j: (0, 0)),
            pl.BlockSpec((d, tf), lambda i, j: (0, j)),
            pl.BlockSpec((d, tf), lambda i, j: (0, j)),
            pl.BlockSpec((tf, d), lambda i, j: (j, 0)),
        ],
        out_specs=pl.BlockSpec((tm, d), lambda i, j: (i, 0)),
        out_shape=jax.ShapeDtypeStruct((m, d), F32),
        scratch_shapes=[pltpu.VMEM((tm, d), BF16), pltpu.VMEM((tm, d), F32)],
        compiler_params=_cparams("parallel", "arbitrary"),
        name="ffn",
    )(x, g.reshape(1, d), w1, w3, w2)


def _moe_kernel(x_ref, g_ref, wr_ref, w1_ref, w3_ref, w2_ref, o_ref, h_sc, acc_sc, comb_sc):
    e = pl.program_id(1)
    j = pl.program_id(2)
    first = (e == 0) & (j == 0)
    last = (e == pl.num_programs(1) - 1) & (j == pl.num_programs(2) - 1)

    @pl.when(first)
    def _():
        hf = _rms_rows(x_ref[...], g_ref[...])
        h_sc[...] = hf.astype(BF16)
        acc_sc[...] = jnp.zeros_like(acc_sc)
        logits = _dot_3x(hf, wr_ref[...])
        lane = _iota(logits.shape, 1)
        lg = jnp.where(lane < N_EXPERTS, logits, -jnp.inf)
        m1 = jnp.max(lg, axis=-1, keepdims=True)
        i1 = jnp.min(jnp.where(lg == m1, lane, 128), axis=-1, keepdims=True)
        lg2 = jnp.where(lane == i1, -jnp.inf, lg)
        m2 = jnp.max(lg2, axis=-1, keepdims=True)
        i2 = jnp.min(jnp.where(lg2 == m2, lane, 128), axis=-1, keepdims=True)
        e2 = jnp.exp(m2 - m1)
        den = 1.0 + e2
        comb_sc[...] = jnp.where(lane == i1, 1.0 / den, 0.0) + jnp.where(lane == i2, e2 / den, 0.0)

    h = h_sc[...]
    a = _dot(h, w1_ref[0])
    b = _dot(h, w3_ref[0])
    t = (a * jax.nn.sigmoid(a)) * b
    comb = comb_sc[...]
    c = jnp.sum(jnp.where(_iota(comb.shape, 1) == e, comb, 0.0), axis=-1, keepdims=True)
    acc_sc[...] += c * _dot(t.astype(BF16), w2_ref[0])

    @pl.when(last)
    def _():
        o_ref[...] = x_ref[...] + acc_sc[...]


def moe(x, g, wr_pad, w1, w3, w2):
    m, d = x.shape
    ne, _, ff = w1.shape
    tm = _pick_tile(m, 1024)
    tf = _pick_tile(ff, 512)
    return pl.pallas_call(
        _moe_kernel,
        grid=(m // tm, ne, ff // tf),
        in_specs=[
            pl.BlockSpec((tm, d), lambda i, e, j: (i, 0)),
            pl.BlockSpec((1, d), lambda i, e, j: (0, 0)),
            pl.BlockSpec((d, 128), lambda i, e, j: (0, 0)),
            pl.BlockSpec((1, d, tf), lambda i, e, j: (e, 0, j)),
            pl.BlockSpec((1, d, tf), lambda i, e, j: (e, 0, j)),
            pl.BlockSpec((1, tf, d), lambda i, e, j: (e, j, 0)),
        ],
        out_specs=pl.BlockSpec((tm, d), lambda i, e, j: (i, 0)),
        out_shape=jax.ShapeDtypeStruct((m, d), F32),
        scratch_shapes=[pltpu.VMEM((tm, d), BF16), pltpu.VMEM((tm, d), F32), pltpu.VMEM((tm, 128), F32)],
        compiler_params=_cparams("parallel", "arbitrary", "arbitrary"),
        name="moe",
    )(x, g.reshape(1, d), wr_pad, w1, w3, w2)


def _gla_kernel(q_ref, k_ref, v_ref, a_ref, r_ref, wa2_ref, wa2t_ref, ba_ref, bacol_ref, gn_ref, s0_ref,
                o_ref, sout_ref, s_sc, b_sc, k_sc, v_sc):
    nb, c, _ = q_ref.shape
    ci = pl.program_id(1)

    @pl.when(ci == 0)
    def _():
        s_sc[...] = s0_ref[...]

    hk = GLA_H * GLA_DK
    hv = GLA_H * GLA_DV
    tri = (_iota((c, c), 0) >= _iota((c, c), 1)).astype(BF16)
    expand = (_iota((hk, hv), 0) // GLA_DK == _iota((hk, hv), 1) // GLA_DV)
    expand_bf = expand.astype(BF16)
    ones_v = _block_ones(hv, GLA_DV)
    wa2 = wa2_ref[...].astype(BF16)
    wa2t = wa2t_ref[...].astype(BF16)
    rowi = _iota((nb, c, hk), 1)

    qs, bs, os1 = [], [], []
    for n in range(nb):
        a_in = a_ref[n].astype(BF16)
        g = _log_sigmoid(_dot(a_in, wa2) + ba_ref[...]) / GLA_TAU
        b = _dot_e3(tri, g)
        gt = _log_sigmoid(_dot_nt(wa2t, a_in) + bacol_ref[...]) / GLA_TAU
        bl_col = jnp.sum(gt, axis=1, keepdims=True)
        q = q_ref[n] * (GLA_DK ** -0.5)
        k = k_ref[n]
        v = v_ref[n]
        s_old = s_sc[n]
        os1.append(_dot_3x(q * jnp.exp(b), s_old))
        kd = k * jnp.exp(b[c - 1:c, :] - b)
        upd = _dot_3x(kd, v, dot=_dot_tn)
        s_sc[n] = s_old * jnp.exp(bl_col) + jnp.where(expand, upd, 0.0)
        qs.append(q)
        b_sc[n] = b
        k_sc[n] = k
        v_sc[n] = v
        bs.append(b)
    q3 = jnp.stack(qs)
    b3 = jnp.stack(bs)

    def body(s, o2):
        b_s = b_sc[:, pl.ds(s, 1), :]
        k_s = k_sc[:, pl.ds(s, 1), :]
        v_s = v_sc[:, pl.ds(s, 1), :]
        dec = jnp.exp(jnp.where(rowi >= s, b3 - b_s, -jnp.inf))
        contrib = (q3 * k_s * dec).reshape(nb * c, hk)
        att = _dot(contrib.astype(BF16), expand_bf).reshape(nb, c, hv)
        return o2 + att * v_s

    o2 = lax.fori_loop(0, c, body, jnp.zeros((nb, c, hv), F32), unroll=4)
    for n in range(nb):
        o = os1[n] + o2[n]
        ms = _dot_x2(o * o, ones_v) * (1.0 / GLA_DV)
        o = o * lax.rsqrt(ms + EPS) * gn_ref[...]
        r = r_ref[n]
        o_ref[n] = o * (r * jax.nn.sigmoid(r))

    @pl.when(ci == pl.num_programs(1) - 1)
    def _():
        sout_ref[...] = s_sc[...]


def gla_mixer(q, k, v, a_in, r, wa2, ba, gn, s0_bd, nb):
    bsz, t, hk = q.shape
    hv = v.shape[-1]
    ra = a_in.shape[-1]
    wa2 = jnp.pad(wa2, ((0, ra - GLA_RANK), (0, 0)))
    c = GLA_CHUNK if t % GLA_CHUNK == 0 else t
    tok = lambda w: pl.BlockSpec((nb, c, w), lambda b, i: (b, i, 0))
    full = lambda s: pl.BlockSpec(s, lambda b, i: (0,) * len(s))
    st = pl.BlockSpec((nb, hk, hv), lambda b, i: (b, 0, 0))
    return pl.pallas_call(
        _gla_kernel,
        grid=(bsz // nb, t // c),
        in_specs=[tok(hk), tok(hk), tok(hv), tok(ra), tok(hv),
                  full((ra, hk)), full((hk, ra)), full((1, hk)), full((hk, 1)), full((1, hv)), st],
        out_specs=[tok(hv), st],
        out_shape=[jax.ShapeDtypeStruct((bsz, t, hv), F32), jax.ShapeDtypeStruct((bsz, hk, hv), F32)],
        scratch_shapes=[pltpu.VMEM((nb, hk, hv), F32), pltpu.VMEM((nb, c, hk), F32),
                        pltpu.VMEM((nb, c, hk), F32), pltpu.VMEM((nb, c, hv), F32)],
        compiler_params=_cparams("parallel", "arbitrary"),
        name="gla",
    )(q, k, v, a_in, r, wa2, wa2.T, ba.reshape(1, hk), ba.reshape(hk, 1), jnp.tile(gn, GLA_H).reshape(1, hv), s0_bd)


def gla_state_to_bd(s):
    b = s.shape[0]
    eye = jnp.eye(GLA_H, dtype=s.dtype)
    return jnp.einsum('bhkv,hg->bhkgv', s, eye).reshape(b, GLA_H * GLA_DK, GLA_H * GLA_DV)


def gla_state_from_bd(sbd):
    b = sbd.shape[0]
    s5 = sbd.reshape(b, GLA_H, GLA_DK, GLA_H, GLA_DV)
    return jnp.stack([s5[:, h, :, h, :] for h in range(GLA_H)], axis=1)


def _rwkv_prep_kernel(p_ref, first_ref, mu_ref, w0_ref, w2_ref, a0_ref, a2_ref, g2_ref, kk_ref, ka_ref, rk_ref,
                      r_o, w_o, k_o, v_o, kk_o, kka_o, g_o, bonus_o):
    p = p_ref[0]
    prev = jnp.where(_iota(p.shape, 0) == 0, first_ref[0, 0], pltpu.roll(p, 1, 0))
    xm = p + (prev - p) * mu_ref[...]
    n = BR_WIDTH
    r, k, v = xm[:, 0:n], xm[:, n:2 * n], xm[:, 2 * n:3 * n]
    xwa = xm[:, 3 * n:3 * n + 128]
    xg = xm[:, 3 * n + 128:]
    decay = jnp.exp(-math.exp(-0.5) * jax.nn.sigmoid(w0_ref[...] + _dot(jnp.tanh(xwa).astype(BF16), w2_ref[...])))
    a = jax.nn.sigmoid(a0_ref[...] + _dot(xwa.astype(BF16), a2_ref[...]))
    g = _dot(jax.nn.sigmoid(xg).astype(BF16), g2_ref[...])
    ones = _block_ones(n, RW_N)
    kk = k * kk_ref[...]
    kk = kk * lax.rsqrt(jnp.maximum(_dot_x2(kk * kk, ones), 1e-12))
    k2 = k * (1.0 + (a - 1.0) * ka_ref[...])
    r_o[0] = r
    w_o[0] = decay
    k_o[0] = k2
    v_o[0] = v
    kk_o[0] = kk
    kka_o[0] = kk * a
    g_o[0] = g
    bonus_o[0] = _dot_x2(r * k2 * rk_ref[...], ones) * v


def rwkv_prep(p, prev, mu, w0, w2, a0, a2, g2, k_k, k_a, r_k):
    bsz, t, d = p.shape
    tm = _pick_tile(t, 512)
    nt = t // tm
    first = jnp.concatenate([prev[:, None, :], p[:, tm - 1:t - 1:tm, :]], axis=1).reshape(bsz, nt, 1, d)
    n = BR_WIDTH
    w2p = jnp.concatenate([w2, jnp.zeros_like(w2)], axis=0).astype(BF16)
    a2p = jnp.concatenate([jnp.zeros_like(a2), a2], axis=0).astype(BF16)
    row = lambda v: v.reshape(1, -1)
    full = lambda s: pl.BlockSpec(s, lambda b, i: (0,) * len(s))
    tok = pl.BlockSpec((1, tm, n), lambda b, i: (b, i, 0))
    return pl.pallas_call(
        _rwkv_prep_kernel,
        grid=(bsz, nt),
        in_specs=[pl.BlockSpec((1, tm, d), lambda b, i: (b, i, 0)),
                  pl.BlockSpec((1, 1, 1, d), lambda b, i: (b, i, 0, 0)),
                  full((1, d)), full((1, n)), full((128, n)), full((1, n)), full((128, n)), full((128, n)),
                  full((1, n)), full((1, n)), full((1, n))],
        out_specs=[tok] * 8,
        out_shape=[jax.ShapeDtypeStruct((bsz, t, n), F32)] * 8,
        compiler_params=_cparams("parallel", "parallel"),
        name="rwkv_prep",
    )(p, first, row(mu), row(w0), w2p, row(a0), a2p, g2.astype(BF16), row(k_k), row(k_a), row(r_k))


def _rwkv_scan_kernel(r_ref, w_ref, k_ref, v_ref, kk_ref, kka_ref, s0_ref, o_ref, sout_ref, s_sc):
    nb, tc, n = r_ref.shape
    ti = pl.program_id(1)

    @pl.when(ti == 0)
    def _():
        s_sc[...] = s0_ref[...]

    ones = _block_ones(n, RW_N)
    diag = (_iota((RW_N, n), 0) == (_iota((RW_N, n), 1) % RW_N)).astype(F32)

    nj = RW_LOOKAHEAD

    def seg_many(xs, single_pass=False):
        x = jnp.concatenate([x.reshape(nb * RW_N, n) for x in xs], axis=0)
        y = _dot(x.astype(BF16), ones) if single_pass else _dot_x2(x, ones)
        return [y[j * nb * RW_N:(j + 1) * nb * RW_N].reshape(nb, RW_N, n) for j in range(len(xs))]

    def body(bi, s):
        t0 = bi * nj
        row = lambda ref, j: ref[:, pl.ds(t0 + j, 1), :].reshape(nb, n)
        w, kk, kka = [row(w_ref, j) for j in range(nj)], [row(kk_ref, j) for j in range(nj)], [row(kka_ref, j) for j in range(nj)]
        k, r, v = [row(k_ref, j) for j in range(nj)], [row(r_ref, j) for j in range(nj)], [row(v_ref, j) for j in range(nj)]
        lift = lambda x: x[:, None, :]
        decay = [jnp.ones_like(w[0])]
        for j in range(1, nj):
            decay.append(decay[-1] * w[j - 1])
        seg_rows = lambda x: _dot_x2(x, ones)
        c, d = {}, {}
        for j in range(1, nj):
            between = jnp.ones_like(w[0])
            for i in range(j - 1, -1, -1):
                c[i, j] = lift(seg_rows(kka[i] * between * kk[j]))
                d[i, j] = lift(seg_rows(k[i] * between * kk[j]))
                between = between * w[i]
        u = seg_many([s * lift(decay[j] * kk[j]) for j in range(nj)])
        vcol = seg_many([lift(v[j]) * diag for j in range(nj)])
        sa, states = [], []
        for j in range(nj):
            x = u[j]
            for i in range(j):
                x = x - sa[i] * c[i, j] + vcol[i] * d[i, j]
            sa.append(x)
            s = s * lift(w[j]) - x * lift(kka[j]) + vcol[j] * lift(k[j])
            states.append(s)
        ocol = seg_many([states[j] * lift(r[j]) for j in range(nj)], single_pass=True)
        for j in range(nj):
            o_ref[:, pl.ds(t0 + j, 1), :] = jnp.sum(ocol[j] * diag, axis=1, keepdims=True)
        return s

    s = lax.fori_loop(0, tc // nj, body, s_sc[...], unroll=4)
    s_sc[...] = s

    @pl.when(ti == pl.num_programs(1) - 1)
    def _():
        sout_ref[...] = s


def rwkv_scan(r, w, k, v, kk, kka, s0, nb):
    bsz, t, n = r.shape
    tc = _pick_tile(t, 256)
    tok = pl.BlockSpec((nb, tc, n), lambda b, i: (b, i, 0))
    st = pl.BlockSpec((nb, RW_N, n), lambda b, i: (b, 0, 0))
    return pl.pallas_call(
        _rwkv_scan_kernel,
        grid=(bsz // nb, t // tc),
        in_specs=[tok] * 6 + [st],
        out_specs=[tok, st],
        out_shape=[jax.ShapeDtypeStruct((bsz, t, n), F32), jax.ShapeDtypeStruct((bsz, RW_N, n), F32)],
        scratch_shapes=[pltpu.VMEM((nb, RW_N, n), F32)],
        compiler_params=_cparams("parallel", "arbitrary"),
        name="rwkv_scan",
    )(r, w, k, v, kk, kka, s0)


def _rwkv_post_kernel(o_ref, bonus_ref, g_ref, gn_ref, out_ref):
    o = o_ref[...]
    ms = _dot_x2(o * o, _block_ones(BR_WIDTH, RW_N)) * (1.0 / RW_N)
    out_ref[...] = (o * lax.rsqrt(ms + EPS) * gn_ref[...] + bonus_ref[...]) * g_ref[...]


def rwkv_post(o, bonus, g, gn):
    m, n = o.shape
    tm = _pick_tile(m, 1024)
    row = pl.BlockSpec((tm, n), lambda i: (i, 0))
    return pl.pallas_call(
        _rwkv_post_kernel,
        grid=(m // tm,),
        in_specs=[row, row, row, pl.BlockSpec((1, n), lambda i: (0, 0))],
        out_specs=row,
        out_shape=jax.ShapeDtypeStruct((m, n), F32),
        compiler_params=_cparams("parallel"),
        name="rwkv_post",
    )(o, bonus, g, jnp.tile(gn, RW_H).reshape(1, n))


def rwkv_mixer(p, prev, s0, mu, w0, w2, a0, a2, g2, k_k, k_a, r_k, gn, nb):
    bsz, t, _ = p.shape
    r, w, k, v, kk, kka, g, bonus = rwkv_prep(p, prev, mu, w0, w2, a0, a2, g2, k_k, k_a, r_k.reshape(-1))
    s0l = s0.transpose(0, 2, 1, 3).reshape(bsz, RW_N, BR_WIDTH)
    o, s = rwkv_scan(r, w, k, v, kk, kka, s0l, nb)
    out = rwkv_post(o.reshape(bsz * t, BR_WIDTH), bonus.reshape(bsz * t, BR_WIDTH), g.reshape(bsz * t, BR_WIDTH), gn)
    s_new = s.reshape(bsz, RW_N, RW_H, RW_N).transpose(0, 2, 1, 3)
    return out.reshape(bsz, t, BR_WIDTH), s_new, p[:, -1]


def _rel_bucket(dist):
    n = jnp.maximum(dist, 0)
    exact = REL_BUCKETS // 2
    nf = jnp.maximum(n, 1).astype(F32)
    large = exact + (jnp.log(nf / exact) / math.log(REL_MAX_DIST / exact) * (REL_BUCKETS - exact)).astype(I32)
    return jnp.where(n < exact, n, jnp.minimum(large, REL_BUCKETS - 1))


def _bucket_bits(bucket):
    return [((bucket >> i) & 1) == 1 for i in range(REL_BUCKETS.bit_length() - 1)]


def _bias_from_bits(bits, tab_ref, head):
    level = [tab_ref[b, head] for b in range(REL_BUCKETS)]
    for bit in bits:
        level = [jnp.where(bit, level[2 * i + 1], level[2 * i]) for i in range(len(level) // 2)]
    return level[0]


def _bias_from_bucket(bucket, tab_ref, head):
    return _bias_from_bits(_bucket_bits(bucket), tab_ref, head)


def _t5_tiles_kernel(tab_ref, o_ref, *, t, head0, window):
    h = pl.program_id(0)
    d = pl.program_id(1)
    dist = d * t + _iota((t, t), 1) - _iota((t, t), 0)
    val = _bias_from_bucket(_rel_bucket(dist), tab_ref, head0 + h)
    valid = dist >= 0
    if window:
        valid = valid & (dist <= WINDOW)
    o_ref[0, 0] = jnp.where(valid, val, NEG)


def t5_tiles(rel_bias, head0, nh, t, window):
    return pl.pallas_call(
        functools.partial(_t5_tiles_kernel, t=t, head0=head0, window=window),
        grid=(nh, 3),
        in_specs=[pl.BlockSpec(memory_space=pltpu.SMEM)],
        out_specs=pl.BlockSpec((1, 1, t, t), lambda h, d: (h, d, 0, 0)),
        out_shape=jax.ShapeDtypeStruct((nh, 3, t, t), F32),
        compiler_params=_cparams("parallel", "parallel"),
        name="t5_tiles",
    )(rel_bias)


def _pair_tables(nq, back):
    qi, kj, bt, fl = [], [], [], []
    for q in range(nq):
        lo = 0 if back is None else max(q - back, 0)
        for k in range(lo, q + 1):
            qi.append(q)
            kj.append(k)
            bt.append(min(q - k, 2))
            fl.append((1 if k == lo else 0) | (2 if k == q else 0))
    return tuple(jnp.asarray(np.asarray(a, np.int32)) for a in (qi, kj, bt, fl))


def _flash_kernel(qi_t, kj_t, bt_t, fl_t, q_ref, k_ref, v_ref, bias_ref, *rest, nrow, t, cw, use_sel, epi, lam_init):
    rest = list(rest)
    sel_ref = rest.pop(0) if use_sel else None
    m_sc, l_sc, acc_sc = rest[-3:]
    o_ref = rest[-4]
    extras = rest[:-4]
    p = pl.program_id(2)
    flags = fl_t[p]
    ncol = nrow * t

    @pl.when((flags & 1) != 0)
    def _():
        m_sc[...] = jnp.full_like(m_sc, NEG)
        l_sc[...] = jnp.zeros_like(l_sc)
        acc_sc[...] = jnp.zeros_like(acc_sc)

    k = k_ref[0, 0]
    vt = v_ref[0, 0]
    bt = bt_t[p]
    nbias = bias_ref.shape[2]
    if use_sel:
        ns = sel_ref.shape[1]
        blk = kj_t[p] * (t // SEL_BLOCK) + _iota((t, ns), 0) // SEL_BLOCK
        expand = (_iota((t, ns), 1) == blk).astype(BF16)
        chosen = _dot(expand, sel_ref[0].astype(BF16))
    m_all, l_all, acc_all = m_sc[...], l_sc[...], acc_sc[...]
    m_out, l_out, acc_out = [], [], []
    scores = [_dot(k, q_ref[0, 0, 0, :, c * cw:(c + 1) * cw]) for c in range(ncol // cw)]
    for c in range(ncol // cw):
        r, off = divmod(c * cw, t)
        cols = slice(c * cw, (c + 1) * cw)
        s = scores[c] + bias_ref[0, bt, r % nbias, :, off:off + cw]
        if use_sel:
            s = jnp.where(chosen[:, off:off + cw] > 0.5, s, NEG)
        m_prev = m_all[:, cols]
        m_new = jnp.maximum(m_prev, jnp.max(s, axis=0, keepdims=True))
        alpha = jnp.exp(m_prev - m_new)
        pr = jnp.exp(s - m_new)
        if use_sel:
            pr = jnp.where(s > 0.5 * NEG, pr, 0.0)
        l_out.append(alpha * l_all[:, cols] + jnp.sum(pr, axis=0, keepdims=True))
        acc_out.append(alpha * acc_all[:, cols] + _dot(vt, pr.astype(BF16)))
        m_out.append(m_new)
    m_sc[...] = jnp.concatenate(m_out, axis=1)
    l_sc[...] = jnp.concatenate(l_out, axis=1)
    acc_sc[...] = jnp.concatenate(acc_out, axis=1)

    @pl.when((flags & 2) != 0)
    def _():
        o = acc_sc[...] / l_sc[...]
        if epi == "plain":
            o_ref[0, 0, 0] = o
        elif epi == "diff":
            lam_ref, gn_ref = extras
            lv = lam_ref[...]
            lam = (jnp.exp(jnp.sum(lv[0:1] * lv[1:2], keepdims=True)) - jnp.exp(jnp.sum(lv[2:3] * lv[3:4], keepdims=True))
                   + lam_init)
            od = o[:, :t] - lam * o[:, t:]
            ms = jnp.mean(od * od, axis=0, keepdims=True)
            o_ref[0, 0, 0] = od * lax.rsqrt(ms + EPS) * gn_ref[...] * (1.0 - lam_init)
        else:
            gate_ref, oc_ref, os_ref = extras
            g = jax.nn.sigmoid(gate_ref[0, 0, 0])
            o_ref[0, 0, 0] = g[0:1] * oc_ref[0, 0, 0] + g[1:2] * os_ref[0, 0, 0] + g[2:3] * o


def flash(qt, k, vt, bias, tables, *, t, sel=None, epi="plain", extras=(), extra_specs=(), lam_init=None):
    bsz, hg, nq, dh, ncol = qt.shape
    nrow = ncol // t
    npairs = tables[0].shape[0]
    in_specs = [
        pl.BlockSpec((1, 1, 1, dh, ncol), lambda b, h, p, qi, kj, bt, fl: (b, h, qi[p], 0, 0)),
        pl.BlockSpec((1, 1, t, dh), lambda b, h, p, qi, kj, bt, fl: (b, h, kj[p], 0)),
        pl.BlockSpec((1, 1, dh, t), lambda b, h, p, qi, kj, bt, fl: (b, h, 0, kj[p])),
        pl.BlockSpec((1,) + bias.shape[1:], lambda b, h, p, qi, kj, bt, fl: (h, 0, 0, 0, 0)),
    ]
    args = [qt, k, vt, bias]
    if sel is not None:
        in_specs.append(pl.BlockSpec((1, sel.shape[1], t), lambda b, h, p, qi, kj, bt, fl: (b, 0, qi[p])))
        args.append(sel)
    in_specs += list(extra_specs)
    args += list(extras)
    n_out = t if epi == "diff" else ncol
    return pl.pallas_call(
        functools.partial(_flash_kernel, nrow=nrow, t=t, cw=min(t, FLASH_COLS), use_sel=sel is not None, epi=epi,
                          lam_init=lam_init),
        grid_spec=pltpu.PrefetchScalarGridSpec(
            num_scalar_prefetch=4,
            grid=(bsz, hg, npairs),
            in_specs=in_specs,
            out_specs=pl.BlockSpec((1, 1, 1, dh, n_out), lambda b, h, p, qi, kj, bt, fl: (b, h, qi[p], 0, 0)),
            scratch_shapes=[pltpu.VMEM((1, ncol), F32), pltpu.VMEM((1, ncol), F32), pltpu.VMEM((dh, ncol), F32)],
        ),
        out_shape=jax.ShapeDtypeStruct((bsz, hg, nq, dh, n_out), F32),
        compiler_params=_cparams("parallel", "parallel", "arbitrary"),
        name="flash_" + epi + ("_sel" if sel is not None else ""),
    )(*tables, *args)


def _to_tiles(x, t):
    b, g, r, tl, d = x.shape
    return x.reshape(b, g, r, tl // t, t, d).transpose(0, 1, 3, 5, 2, 4).reshape(b, g, tl // t, d, r * t)


def _from_tiles(x, r):
    b, g, nq, d, rt = x.shape
    t = rt // r
    return x.reshape(b, g, nq, d, r, t).transpose(0, 1, 4, 2, 5, 3).reshape(b, g, r, nq * t, d)


def _compress_kernel(pt_ref, *refs, npp):
    del pt_ref
    wt_ref = refs[npp]
    a_ref, b_ref = refs[npp + 1:]
    wt = wt_ref[...]
    for i in range(npp):
        x3 = refs[i][0].reshape(PAGE_SIZE // CMP_STRIDE, CMP_STRIDE, 2 * NSA_DH)
        a_ref[0, i * 8:(i + 1) * 8, :] = jnp.sum(x3 * wt[None, 0:CMP_STRIDE], axis=1)
        b_ref[0, i * 8:(i + 1) * 8, :] = jnp.sum(x3 * wt[None, CMP_STRIDE:], axis=1)


def compress(pool, pt, wt, npp):
    bsz, n_pages = pt.shape
    g = PAGE_SIZE // CMP_STRIDE
    page = lambda i: pl.BlockSpec((1, PAGE_SIZE, 2 * NSA_DH), lambda b, j, pt_ref: (pt_ref[b, j * npp + i], 0, 0))
    out = pl.BlockSpec((1, npp * g, 2 * NSA_DH), lambda b, j, pt_ref: (b, j, 0))
    shape = jax.ShapeDtypeStruct((bsz, n_pages * g, 2 * NSA_DH), F32)
    return pl.pallas_call(
        functools.partial(_compress_kernel, npp=npp),
        grid_spec=pltpu.PrefetchScalarGridSpec(
            num_scalar_prefetch=1,
            grid=(bsz, n_pages // npp),
            in_specs=[page(i) for i in range(npp)] + [pl.BlockSpec((CMP_BLOCK, 2 * NSA_DH), lambda b, j, pt_ref: (0, 0))],
            out_specs=[out, out],
        ),
        out_shape=[shape, shape],
        compiler_params=_cparams("parallel", "arbitrary"),
        name="nsa_compress",
    )(pt, *([pool] * npp), wt)


def _nsa_cmp_kernel(tab_ref, q_ref, a_ref, b_ref, tail_ref, o_ref, sel_ref, *, t, qpos0, n_cmp, n_sel):
    nc = a_ref.shape[1]
    ns = sel_ref.shape[-1]
    qi = pl.program_id(1)
    rown = _iota((nc, 2 * NSA_DH), 0)
    bsh = jnp.where(rown == nc - 1, tail_ref[0], pltpu.roll(b_ref[0], nc - 1, 0))
    kcv = jnp.where(rown < n_cmp, a_ref[0] + bsh, 0.0)
    vc = kcv[:, NSA_DH:].astype(BF16)
    kc_hi, kc_lo = _split2(kcv[:, :NSA_DH])
    start = _iota((nc, ns), 0) * CMP_STRIDE
    sblk = _iota((nc, ns), 1) * SEL_BLOCK
    ov = ((start < sblk + SEL_BLOCK) & (start + CMP_BLOCK > sblk)).astype(BF16)
    rc = max(8, min(t, CMP_CHUNK_ROWS))
    scores = []
    for c in range(t // rc):
        qpos = qpos0 + qi * t + c * rc + _iota((rc, nc), 0)
        n = _iota((rc, nc), 1)
        dist = qpos - (n * CMP_STRIDE + CMP_BLOCK - 1)
        valid = (dist >= 0) & (n < n_cmp)
        bits = _bucket_bits(_rel_bucket(dist))
        psum = jnp.zeros((rc, nc), F32)
        for h in range(NSA_H):
            q_hi, q_lo = _split2(q_ref[0, 0, h, c * rc:(c + 1) * rc, :])
            s = _dot_nt(q_hi, kc_hi) + _dot_nt(q_lo, kc_hi) + _dot_nt(q_hi, kc_lo)
            sh = jnp.where(valid, s + _bias_from_bits(bits, tab_ref, h), NEG)
            m = jnp.max(sh, axis=-1, keepdims=True)
            p = jnp.where(valid, jnp.exp(sh - m), 0.0)
            p = p / jnp.maximum(jnp.sum(p, axis=-1, keepdims=True), 1e-30)
            o_ref[0, 0, h, c * rc:(c + 1) * rc, :] = _dot(p.astype(BF16), vc)
            psum = psum + p
        hi, mid, lo = _split3(psum)
        scores.append(_dot(hi, ov) + _dot(mid, ov) + _dot(lo, ov))
    score = scores[0] if len(scores) == 1 else jnp.concatenate(scores, axis=0)
    j = _iota((t, ns), 1)
    cur = (qpos0 + qi * t + _iota((t, ns), 0)) // SEL_BLOCK
    forced = (j == 0) | (j == cur) | (j == cur - 1)
    sc = jnp.where(j <= cur, score + jnp.where(forced, FORCE_SCORE, 0.0), -1.0)
    sc = jnp.where(j < n_sel, sc, -jnp.inf)
    chosen = jnp.zeros((t, ns), F32)
    for _ in range(min(SEL_TOPK, n_sel)):
        m = jnp.max(sc, axis=-1, keepdims=True)
        idx = jnp.min(jnp.where(sc == m, j, ns), axis=-1, keepdims=True)
        hit = j == idx
        chosen = jnp.where(hit, 1.0, chosen)
        sc = jnp.where(hit, -jnp.inf, sc)
    sel_ref[0] = chosen


def nsa_cmp(q4, a, b, tail, rel_bias_nsa, *, t, qpos0, n_cmp, n_sel):
    bsz, _, _, tq, dh = q4.shape
    nc = a.shape[1]
    ns = -(-n_sel // 128) * 128
    return pl.pallas_call(
        functools.partial(_nsa_cmp_kernel, t=t, qpos0=qpos0, n_cmp=n_cmp, n_sel=n_sel),
        grid=(bsz, tq // t),
        in_specs=[
            pl.BlockSpec(memory_space=pltpu.SMEM),
            pl.BlockSpec((1, 1, NSA_H, t, dh), lambda b, i: (b, 0, 0, i, 0)),
            pl.BlockSpec((1, nc, 2 * dh), lambda b, i: (b, 0, 0)),
            pl.BlockSpec((1, nc, 2 * dh), lambda b, i: (b, 0, 0)),
            pl.BlockSpec((1, 1, 2 * dh), lambda b, i: (b, 0, 0)),
        ],
        out_specs=[pl.BlockSpec((1, 1, NSA_H, t, dh), lambda b, i: (b, 0, 0, i, 0)),
                   pl.BlockSpec((1, t, ns), lambda b, i: (b, i, 0))],
        out_shape=[jax.ShapeDtypeStruct((bsz, 1, NSA_H, tq, dh), F32), jax.ShapeDtypeStruct((bsz, tq, ns), F32)],
        compiler_params=_cparams("parallel", "parallel"),
        name="nsa_cmp",
    )(rel_bias_nsa, q4, a, b, tail)


def _cmp_weight_tile(w_cmp):
    return jnp.repeat(w_cmp.T, NSA_DH, axis=1)


def nsa_prompt(qn, kvn, n_g, w_cmp, Wc):
    bsz, t_len, _ = qn.shape
    t = _pick_tile(t_len, 256)
    q4f = (qn.reshape(bsz, t_len, NSA_H, NSA_DH) * NSA_DH ** -0.5).transpose(0, 2, 1, 3)[:, None]
    q4 = q4f.astype(BF16)
    n_pages = t_len // PAGE_SIZE
    pool = kvn[:, :, 0:2 * NSA_DH].reshape(bsz * n_pages, PAGE_SIZE, 2 * NSA_DH)
    pt = jnp.arange(bsz * n_pages, dtype=I32).reshape(bsz, n_pages)
    a, b = compress(pool, pt, _cmp_weight_tile(w_cmp), _pick_tile(n_pages, 16))
    o_c, chosen = nsa_cmp(q4f, a, b, jnp.zeros((bsz, 1, 2 * NSA_DH), F32), Wc['rel_nsa'], t=t, qpos0=0,
                          n_cmp=t_len // CMP_STRIDE - 1, n_sel=t_len // SEL_BLOCK)
    t = _pick_tile(t_len, NSA_FLASH_TILE)
    kv = lambda i: kvn[:, :, i * NSA_DH:(i + 1) * NSA_DH].astype(BF16)[:, None]
    kvt = lambda i: kv(i).transpose(0, 1, 3, 2)
    nq = t_len // t
    qt = _to_tiles(q4, t)
    o_s = flash(qt, kv(2), kvt(3), Wc['tiles_nsa'], _pair_tables(nq, None), t=t, sel=chosen.transpose(0, 2, 1))
    gates = n_g.reshape(bsz, nq, t, NSA_H, 3).transpose(0, 1, 4, 3, 2).reshape(bsz, 1, nq, 3, NSA_H * t)
    tile = lambda w: pl.BlockSpec((1, 1, 1, w, NSA_H * t), lambda b, h, p, qi, kj, bt, fl: (b, h, qi[p], 0, 0))
    o = flash(qt, kv(4), kvt(5), Wc['tiles_win'], _pair_tables(nq, WINDOW // t), t=t, epi="win",
              extras=(gates, _to_tiles(o_c, t), o_s), extra_specs=(tile(3), tile(NSA_DH), tile(NSA_DH)))
    return _from_tiles(o, NSA_H)[:, 0].transpose(0, 2, 1, 3).reshape(bsz, t_len, NSA_H * NSA_DH)


def _lam_init(l):
    return 0.8 - 0.6 * math.exp(-0.3 * l)


def diff_prompt(d_q, d_k, d_v, lam_rows, lam_init, gn, Wc):
    bsz, t_len, _ = d_q.shape
    t = _pick_tile(t_len, DIFF_FLASH_TILE)
    q = d_q.reshape(bsz, t_len, DF_H, 2 * DF_D).transpose(0, 2, 1, 3) * DF_D ** -0.5
    lane = jnp.arange(2 * DF_D) < DF_D
    q2 = jnp.stack([jnp.where(lane, q, 0.0), jnp.where(lane, 0.0, q)], axis=2).astype(BF16)
    k = d_k.reshape(bsz, t_len, DF_H, 2 * DF_D).transpose(0, 2, 1, 3).astype(BF16)
    vt = d_v.reshape(bsz, t_len, DF_H, DF_DV).transpose(0, 2, 3, 1).astype(BF16)
    full = lambda s: pl.BlockSpec(s, lambda b, h, p, qi, kj, bt, fl: (0,) * len(s))
    o = flash(_to_tiles(q2, t), k, vt, Wc['tiles_diff'], _pair_tables(t_len // t, None), t=t, epi="diff",
              lam_init=lam_init, extras=(lam_rows, gn.reshape(DF_DV, 1)),
              extra_specs=(full((4, DF_D)), full((DF_DV, 1))))
    return _from_tiles(o, 1)[:, :, 0].transpose(0, 2, 1, 3).reshape(bsz, t_len, BR_WIDTH)


NEW_PAD = 16


def _paged_attn_kernel(pt_ref, tab_ref, q_ref, *refs, npp, head_cols, kpos0, qpos0, t_new, window, use_sel):
    del pt_ref
    pages = refs[:npp]
    newk_ref, newv_ref = refs[npp], refs[npp + 1]
    sel_ref = refs[npp + 2] if use_sel else None
    o_ref, m_sc, l_sc, acc_sc = refs[-4:]
    j = pl.program_id(1)
    tq = t_new
    ng = len(head_cols)
    nrow = ng * tq
    ks = npp * PAGE_SIZE
    hkv, rh = q_ref.shape[1], q_ref.shape[2]
    hrows = lambda x, h: x[h * rh:(h + 1) * rh]

    @pl.when(j == 0)
    def _():
        m_sc[...] = jnp.full_like(m_sc, NEG)
        l_sc[...] = jnp.zeros_like(l_sc)
        acc_sc[...] = jnp.zeros_like(acc_sc)

    qs = [q_ref[0, h] for h in range(hkv)]

    def update(s, kpos, extra_valid, blocks, pv):
        n = s.shape[-1]
        dist = (qpos0 + _iota((tq, n), 0)) - kpos
        valid = dist >= 0
        if window:
            valid = valid & (dist <= WINDOW)
        if extra_valid is not None:
            valid = valid & extra_valid
        if use_sel:
            nsb = sel_ref.shape[-1]
            expand = (_iota((nsb, n), 0) == blocks).astype(BF16)
            valid = valid & (_dot(sel_ref[0].astype(BF16), expand) > 0.5)
        bits = _bucket_bits(_rel_bucket(dist))
        bias = {c: _bias_from_bits(bits, tab_ref, c) for c in sorted(set(head_cols))}
        bias3 = jnp.stack([bias[c] for c in head_cols])
        s = jnp.where(valid[None], s.reshape(ng, tq, n) + bias3, NEG).reshape(nrow, n)
        m_prev = m_sc[...]
        m_new = jnp.maximum(m_prev, jnp.max(s, axis=-1, keepdims=True))
        alpha = jnp.exp(m_prev - m_new)
        pr = jnp.where(s > 0.5 * NEG, jnp.exp(s - m_new), 0.0)
        l_sc[...] = alpha * l_sc[...] + jnp.sum(pr, axis=-1, keepdims=True)
        acc_sc[...] = alpha * acc_sc[...] + pv(pr.astype(BF16))
        m_sc[...] = m_new

    cat = lambda xs, axis: xs[0] if len(xs) == 1 else jnp.concatenate(xs, axis=axis)
    kt = [[pages[i][0, 0, 0, h].astype(BF16) for h in range(hkv)] for i in range(npp)]
    vt = [[pages[i][0, 0, 1, h].astype(BF16) for h in range(hkv)] for i in range(npp)]
    s = cat([cat([_dot(qs[h], kt[i][h]) for i in range(npp)], 1) for h in range(hkv)], 0)
    kpos = kpos0 + j * ks + _iota((tq, ks), 1)
    blocks = (kpos0 + j * ks + _iota((1, ks), 1)) // SEL_BLOCK

    def pv_pages(pb):
        outs = []
        for h in range(hkv):
            ph = hrows(pb, h)
            out = _dot_nt(ph[:, 0:PAGE_SIZE], vt[0][h])
            for i in range(1, npp):
                out = out + _dot_nt(ph[:, i * PAGE_SIZE:(i + 1) * PAGE_SIZE], vt[i][h])
            outs.append(out)
        return cat(outs, 0)

    update(s, kpos, None, blocks, pv_pages)

    @pl.when(j == pl.num_programs(1) - 1)
    def _():
        nk = [newk_ref[0, h].astype(BF16) for h in range(hkv)]
        nv = [newv_ref[0, h].astype(BF16) for h in range(hkv)]
        col = _iota((tq, NEW_PAD), 1)
        update(cat([_dot_nt(qs[h], nk[h]) for h in range(hkv)], 0), qpos0 + col, col < t_new,
               (qpos0 + _iota((1, NEW_PAD), 1)) // SEL_BLOCK,
               lambda pb: cat([_dot(hrows(pb, h), nv[h]) for h in range(hkv)], 0))
        o_ref[0] = acc_sc[...] / l_sc[...]


def paged_attn(q, pool, layer, pt, page_index, newk, newv, tab, *, npp, head_cols, kpos0, qpos0, window=False,
               sel=None):
    bsz, hkv, rh, dh = q.shape
    nrow = hkv * rh
    lw = dh
    n_pages = pt.shape[1]
    t_new = nrow // len(head_cols)
    page = lambda i: pl.BlockSpec((1, 1, 2, hkv, dh, PAGE_SIZE),
                                  lambda b, j, pt_ref: (layer,) + page_index(b, j * npp + i, pt_ref))
    new_spec = pl.BlockSpec((1, hkv, NEW_PAD, dh), lambda b, j, pt_ref: (b, 0, 0, 0))
    in_specs = [pl.BlockSpec(memory_space=pltpu.SMEM), pl.BlockSpec((1, hkv, rh, dh), lambda b, j, pt_ref: (b, 0, 0, 0))]
    in_specs += [page(i) for i in range(npp)]
    in_specs += [new_spec, new_spec]
    args = [tab, q] + [pool] * npp + [newk, newv]
    if sel is not None:
        in_specs.append(pl.BlockSpec((1, t_new, sel.shape[-1]), lambda b, j, pt_ref: (b, 0, 0)))
        args.append(sel)
    return pl.pallas_call(
        functools.partial(_paged_attn_kernel, npp=npp, head_cols=tuple(head_cols), kpos0=kpos0, qpos0=qpos0,
                          t_new=t_new, window=window, use_sel=sel is not None),
        grid_spec=pltpu.PrefetchScalarGridSpec(
            num_scalar_prefetch=1,
            grid=(bsz, n_pages // npp),
            in_specs=in_specs,
            out_specs=pl.BlockSpec((1, nrow, lw), lambda b, j, pt_ref: (b, 0, 0)),
            scratch_shapes=[pltpu.VMEM((nrow, 1), F32), pltpu.VMEM((nrow, 1), F32), pltpu.VMEM((nrow, lw), F32)],
        ),
        out_shape=jax.ShapeDtypeStruct((bsz, nrow, lw), F32),
        compiler_params=_cparams("parallel", "arbitrary"),
        name="paged_attn",
    )(pt, *args)


def _nsa_combine_kernel(g_ref, oc_ref, os_ref, ow_ref, o_ref):
    g = jax.nn.sigmoid(g_ref[...])
    o_ref[...] = g[..., 0:1] * oc_ref[...] + g[..., 1:2] * os_ref[...] + g[..., 2:3] * ow_ref[...]


def nsa_combine(gates, o_c, o_s, o_w):
    n, dh = o_c.shape
    full = lambda w: pl.BlockSpec((n, w), lambda i: (0, 0))
    return pl.pallas_call(
        _nsa_combine_kernel, grid=(1,),
        in_specs=[full(3), full(dh), full(dh), full(dh)], out_specs=full(dh),
        out_shape=jax.ShapeDtypeStruct((n, dh), F32), name="nsa_combine",
    )(gates, o_c, o_s, o_w)


def _pad_rows(x, n):
    return jnp.pad(x, ((0, 0), (0, n - x.shape[1]), (0, 0)))


def _compress_t_kernel(pt_ref, *refs, npp):
    del pt_ref
    pages = refs[:npp]
    whi_ref, wlo_ref = refs[npp], refs[npp + 1]
    a_ref, b_ref = refs[npp + 2:]
    for kv in range(2):
        acc = None
        for i in range(npp):
            xh, xl = _split2(pages[i][0, 0, kv, 0])
            y = _dot(xh, whi_ref[kv, i]) + _dot(xl, whi_ref[kv, i]) + _dot(xh, wlo_ref[kv, i])
            acc = y if acc is None else acc + y
        a_ref[0, kv] = acc[:, :128]
        b_ref[0, kv] = acc[:, 128:]


def compress_t(pool, layer, pt, w_cmp):
    npp = 16
    bsz, n_pages = pt.shape
    g = PAGE_SIZE // CMP_STRIDE
    r = np.arange(PAGE_SIZE)
    grp = jnp.asarray((r[:, None] // CMP_STRIDE == np.arange(g)[None, :]).astype(np.float32))
    slot = jnp.eye(npp, dtype=F32)
    halves = []
    for half in range(2):
        wr = w_cmp[:, half * CMP_STRIDE + r % CMP_STRIDE]
        halves.append(jnp.einsum('kr,rg,ip->kirpg', wr, grp, slot).reshape(2, npp, PAGE_SIZE, npp * g))
    wbig = jnp.concatenate(halves, axis=-1)
    whi = wbig.astype(BF16)
    wlo = (wbig - whi.astype(F32)).astype(BF16)
    dh = pool.shape[-2]
    page = lambda i: pl.BlockSpec((1, 1, 2, 1, dh, PAGE_SIZE),
                                  lambda b, j, pt_ref: (layer, pt_ref[b, j * npp + i], 0, 0, 0, 0))
    wspec = pl.BlockSpec((2, npp, PAGE_SIZE, 2 * npp * g), lambda b, j, pt_ref: (0, 0, 0, 0))
    out = pl.BlockSpec((1, 2, dh, npp * g), lambda b, j, pt_ref: (b, 0, 0, j))
    shape = jax.ShapeDtypeStruct((bsz, 2, dh, n_pages * g), F32)
    return pl.pallas_call(
        functools.partial(_compress_t_kernel, npp=npp),
        grid_spec=pltpu.PrefetchScalarGridSpec(
            num_scalar_prefetch=1,
            grid=(bsz, n_pages // npp),
            in_specs=[page(i) for i in range(npp)] + [wspec, wspec],
            out_specs=[out, out],
        ),
        out_shape=[shape, shape],
        compiler_params=_cparams("parallel", "arbitrary"),
        name="nsa_compress_t",
    )(pt, *([pool] * npp), whi, wlo)


def _paged_index(b, page, pt_ref):
    return (pt_ref[b, page], 0, 0, 0, 0)


def _window_index(b, page, pt_ref):
    return (b, 0, 0, 0, page)


def nsa_sample(qn, kvn, n_g, w_cmp, Wc, layer, pool_cmp, pool_sel, page_table, win_t):
    bsz, t_len, _ = qn.shape
    lw = 2 * NSA_DH
    total = PAST_LEN + t_len
    n_grp = -(-total // CMP_STRIDE)
    n_cmp = n_grp - CMP_BLOCK // CMP_STRIDE + 1
    n_sel = -(-total // SEL_BLOCK)
    qf = (qn.reshape(bsz, t_len, NSA_H, NSA_DH) * NSA_DH ** -0.5).transpose(0, 2, 1, 3)
    q1 = qf.astype(BF16).reshape(bsz, 1, NSA_H * t_len, NSA_DH)
    at, bt = compress_t(pool_cmp, layer, page_table, w_cmp)
    rows = lambda x: x.transpose(0, 3, 1, 2).reshape(bsz, x.shape[-1], lw)
    new_page = _pad_rows(kvn[:, :, 0:lw], PAGE_SIZE)
    _, b_new = compress(new_page, jnp.arange(bsz, dtype=I32).reshape(bsz, 1), _cmp_weight_tile(w_cmp), 1)
    o_c, chosen = nsa_cmp(qf[:, None], rows(at), rows(bt), b_new[:, 0:1], Wc['rel_nsa'], t=t_len, qpos0=PAST_LEN,
                          n_cmp=n_cmp, n_sel=n_sel)
    new = lambda i: _pad_rows(kvn[:, :, i * NSA_DH:(i + 1) * NSA_DH], NEW_PAD)[:, None]
    heads = tuple(range(NSA_H))
    o_s = paged_attn(q1, pool_sel, layer, page_table, _paged_index, new(2), new(3), Wc['rel_nsa'], npp=16,
                     head_cols=heads, kpos0=0, qpos0=PAST_LEN, sel=chosen)
    wb = win_t.shape[-1]
    wpages = wb // PAGE_SIZE
    o_w = paged_attn(q1, win_t, layer, jnp.zeros((bsz, wpages), I32), _window_index, new(4), new(5), Wc['rel_nsa'],
                     npp=wpages, head_cols=heads, kpos0=PAST_LEN - wb, qpos0=PAST_LEN, window=True)
    n = bsz * NSA_H * t_len
    gates = n_g.reshape(bsz, t_len, NSA_H, 3).transpose(0, 2, 1, 3).reshape(n, 3)
    o = nsa_combine(gates, o_c.reshape(n, NSA_DH), o_s.reshape(n, NSA_DH), o_w.reshape(n, NSA_DH))
    return o.reshape(bsz, NSA_H, t_len, NSA_DH).transpose(0, 2, 1, 3).reshape(bsz, t_len, NSA_H * NSA_DH)


def _diff_post_kernel(o_ref, lam_ref, gn_ref, out_ref, *, lam_init):
    lv = lam_ref[...]
    lam = (jnp.exp(jnp.sum(lv[0:1] * lv[1:2], keepdims=True)) - jnp.exp(jnp.sum(lv[2:3] * lv[3:4], keepdims=True))
           + lam_init)
    od = o_ref[0] - lam * o_ref[1]
    out_ref[...] = _rms_rows(od, gn_ref[...]) * (1.0 - lam_init)


def diff_post(o2, lam_rows, lam_init, gn):
    _, n, dv = o2.shape
    return pl.pallas_call(
        functools.partial(_diff_post_kernel, lam_init=lam_init), grid=(1,),
        in_specs=[pl.BlockSpec((2, n, dv), lambda i: (0, 0, 0)), pl.BlockSpec((4, DF_D), lambda i: (0, 0)),
                  pl.BlockSpec((1, dv), lambda i: (0, 0))],
        out_specs=pl.BlockSpec((n, dv), lambda i: (0, 0)),
        out_shape=jax.ShapeDtypeStruct((n, dv), F32), name="diff_post",
    )(o2, lam_rows, gn.reshape(1, dv))


def diff_sample(d_q, d_k, d_v, lam_rows, lam_init, gn, Wc, layer, pool, page_table):
    bsz, t_len, _ = d_q.shape
    q = (d_q.reshape(bsz, t_len, DF_H, 2 * DF_D) * DF_D ** -0.5).transpose(0, 2, 1, 3)
    lane = jnp.arange(2 * DF_D) < DF_D
    q2 = jnp.stack([jnp.where(lane, q, 0.0), jnp.where(lane, 0.0, q)], axis=2).astype(BF16)
    q2 = q2.reshape(bsz, DF_H, 2 * t_len, 2 * DF_D)
    new = lambda x: _pad_rows(x, NEW_PAD).reshape(bsz, NEW_PAD, DF_H, DF_DV).transpose(0, 2, 1, 3)
    head_cols = tuple(NSA_H + h for h in range(DF_H) for _ in range(2))
    o = paged_attn(q2, pool, layer, page_table, _paged_index, new(d_k), new(d_v), Wc['rel_all'], npp=16,
                   head_cols=head_cols, kpos0=0, qpos0=PAST_LEN)
    n = bsz * DF_H * t_len
    o2 = o.reshape(bsz, DF_H, 2, t_len, DF_DV).transpose(2, 0, 1, 3, 4).reshape(2, n, DF_DV)
    od = diff_post(o2, lam_rows, lam_init, gn).reshape(bsz, DF_H, t_len, DF_DV)
    return od.transpose(0, 2, 1, 3).reshape(bsz, t_len, BR_WIDTH)


def prepare_weights(W):
    Wc = {}
    w_in = W['w_in']
    Wc['w_main'] = jnp.concatenate(
        [jnp.pad(w_in[:, :, IN_OFFS[g]:IN_OFFS[g + 1]], ((0, 0), (0, 0), (0, GROUP_PADS[g] - IN_WIDTHS[g])))
         for g in range(len(GROUP_PADS))], axis=-1).astype(BF16)
    Wc['w_gate'] = w_in[:, :, MAIN_COLS:].reshape(DEPTH, D_MODEL, N_BRANCH, D_MODEL).transpose(0, 2, 1, 3).astype(BF16)
    for name in ('w_br', 'w_out', 'ffn_w1', 'ffn_w3', 'ffn_w2', 'moe_w1', 'moe_w3', 'moe_w2'):
        Wc[name] = W[name].astype(BF16)
    Wc['moe_router'] = jnp.pad(W['moe_router'], ((0, 0), (0, 0), (0, 128 - N_EXPERTS)))
    rel = W['rel_bias']
    Wc['rel_nsa'] = rel[:, :NSA_H]
    Wc['rel_all'] = rel
    Wc['tiles_nsa'] = t5_tiles(rel, 0, NSA_H, NSA_FLASH_TILE, False).transpose(1, 0, 2, 3)[None]
    Wc['tiles_win'] = t5_tiles(rel, 0, NSA_H, NSA_FLASH_TILE, True).transpose(1, 0, 2, 3)[None]
    Wc['tiles_diff'] = t5_tiles(rel, NSA_H, DF_H, DIFF_FLASH_TILE, False)[:, :, None]
    return Wc


def token_mix(h, l, W, Wc, past, page_table):
    B, T, _ = h.shape
    m = B * T
    nb_state = 2 if past is None else 8
    proj = in_proj(h.reshape(m, D_MODEL), Wc['w_main'][l])
    g_q, g_k, g_v, g_a, g_r, n_q, n_kv, n_g, d_q, d_k, d_v, rw = [p.reshape(B, T, p.shape[-1]) for p in proj]
    n_g = n_g[:, :, :IN_WIDTHS[7]]
    st = {}
    s0 = jnp.zeros((B, GLA_H * GLA_DK, GLA_H * GLA_DV), F32) if past is None else gla_state_to_bd(past['gla'][l])
    o_a, s_bd = gla_mixer(g_q, g_k, g_v, g_a, g_r, W['gla_wa2'][l], W['gla_ba'][l], W['gla_norm_g'][l], s0, nb_state)
    st['gla'] = gla_state_from_bd(s_bd)
    kvn = n_kv.reshape(B, T, 6, NSA_DH)
    if past is None:
        o_b = nsa_prompt(n_q, n_kv, n_g, W['nsa_cmp_w'][l], Wc)
        st['win'] = kvn[:, T - min(WINDOW, T):, 4:6]
    else:
        o_b = nsa_sample(n_q, n_kv, n_g, W['nsa_cmp_w'][l], Wc, l, past['cmp_t'], past['sel_t'], page_table,
                         past['win_t'])
        st['win'] = jnp.concatenate([past['win'][l], kvn[:, :, 4:6]], axis=1)[:, T:]
    st['cmp'] = kvn[:, :, 0:2]
    st['sel'] = kvn[:, :, 2:4]
    prev = jnp.zeros((B, RW_PROJ), F32) if past is None else past['shift'][l]
    s0 = jnp.zeros((B, RW_H, RW_N, RW_N), F32) if past is None else past['rwkv'][l]
    o_c, st['rwkv'], st['shift'] = rwkv_mixer(
        rw, prev, s0, W['rw_mu'][l], W['rw_w0'][l], W['rw_w2'][l], W['rw_a0'][l], W['rw_a2'][l], W['rw_g2'][l],
        W['rw_kk'][l], W['rw_ka'][l], W['rw_rk'][l], W['rw_norm_g'][l], nb_state)
    lam_init = _lam_init(l)
    if past is None:
        o_d = diff_prompt(d_q, d_k, d_v, W['df_lam'][l], lam_init, W['df_norm_g'][l], Wc)
    else:
        o_d = diff_sample(d_q, d_k, d_v, W['df_lam'][l], lam_init, W['df_norm_g'][l], Wc, l, past['diff_t'],
                          page_table)
    st['diff'] = jnp.stack([d_k.reshape(B, T, DF_H, 2 * DF_D), d_v.reshape(B, T, DF_H, DF_DV)], axis=2)
    return [t.reshape(m, BR_WIDTH) for t in (o_a, o_b, o_c, o_d)], st


def trunk(x, W, Wc, cache, page_table):
    B, T, _ = x.shape
    x2 = x.reshape(B * T, D_MODEL)
    new = {}
    for l in range(DEPTH):
        past = cache
        h = rmsnorm(x2, W['norm1_g'][l], BF16)
        brs, st = token_mix(h.reshape(B, T, D_MODEL), l, W, Wc, past, page_table)
        x2 = merge(h, brs, x2, Wc['w_gate'][l], Wc['w_br'][l], Wc['w_out'][l])
        j = l // 2
        if l % 2 == 0:
            x2 = ffn(x2, W['norm2_g'][l], Wc['ffn_w1'][j], Wc['ffn_w3'][j], Wc['ffn_w2'][j])
        else:
            x2 = moe(x2, W['norm2_g'][l], Wc['moe_router'][j], Wc['moe_w1'][j], Wc['moe_w3'][j], Wc['moe_w2'][j])
        for name, arr in st.items():
            new.setdefault(name, []).append(arr)
    y = rmsnorm(x2, W['final_norm_g'], F32).reshape(B, T, D_MODEL)
    return y, {name: jnp.stack(arrs) for name, arrs in new.items()}


def kernel(x_prompt, x_sample, cache_nsa_cmp, cache_nsa_sel, cache_diff, state_nsa_win, state_gla, state_rwkv, state_rwkv_shift, page_table, norm1_g, norm2_g, final_norm_g, w_in, gla_wa2, gla_ba, gla_norm_g, nsa_cmp_w, rw_mu, rw_w0, rw_w2, rw_a0, rw_a2, rw_g2, rw_kk, rw_ka, rw_rk, rw_norm_g, df_lam, df_norm_g, w_br, w_out, rel_bias, ffn_w1, ffn_w3, ffn_w2, moe_router, moe_w1, moe_w3, moe_w2):
    W = dict(norm1_g=norm1_g, norm2_g=norm2_g, final_norm_g=final_norm_g, w_in=w_in, gla_wa2=gla_wa2,
             gla_ba=gla_ba, gla_norm_g=gla_norm_g, nsa_cmp_w=nsa_cmp_w, rw_mu=rw_mu, rw_w0=rw_w0, rw_w2=rw_w2,
             rw_a0=rw_a0, rw_a2=rw_a2, rw_g2=rw_g2, rw_kk=rw_kk, rw_ka=rw_ka, rw_rk=rw_rk, rw_norm_g=rw_norm_g,
             df_lam=df_lam, df_norm_g=df_norm_g, w_br=w_br, w_out=w_out, rel_bias=rel_bias, ffn_w1=ffn_w1,
             ffn_w3=ffn_w3, ffn_w2=ffn_w2, moe_router=moe_router, moe_w1=moe_w1, moe_w3=moe_w3, moe_w2=moe_w2)
    rows_last = lambda a: jnp.moveaxis(a, 2, -1)
    cache = dict(cmp_t=rows_last(cache_nsa_cmp)[:, :, :, None], sel_t=rows_last(cache_nsa_sel)[:, :, :, None],
                 diff_t=rows_last(cache_diff), win_t=rows_last(state_nsa_win)[:, :, :, None], win=state_nsa_win,
                 gla=state_gla, rwkv=state_rwkv, shift=state_rwkv_shift)
    Wc = prepare_weights(W)
    y_prompt, sp = trunk(x_prompt, W, Wc, None, None)
    y_sample, ss = trunk(x_sample, W, Wc, cache, page_table)
    return (y_prompt, y_sample,
            sp['cmp'], sp['sel'], sp['diff'], sp['win'], sp['gla'], sp['rwkv'], sp['shift'],
            ss['cmp'], ss['sel'], ss['diff'], ss['win'], ss['gla'], ss['rwkv'], ss['shift'])
```
